```python
import jax, jax.numpy as jnp
from jax import lax
import numpy as np

D_MODEL = 2048
BATCH = 8
SEQ = 2048
DEPTH = 2
DEC_BATCH = 16
DEC_SEQ = 16
PAST_LEN = 1024

CHUNK = 64
M_HEADS = 8
M_DK = D_MODEL // 16
M_DV = D_MODEL // 8
M_QK = M_HEADS * M_DK
M_V = M_HEADS * M_DV
G_CHUNK = 128
G_GROUPS = 4
G_HALF = D_MODEL // 2
G_CH = G_HALF // G_GROUPS
E_GROUPS = 4
E_PER_GROUP = 8
N_EXPERTS = E_GROUPS * E_PER_GROUP
TOP_K = 2
D_EXPERT = D_MODEL // 4
MOE_BLOCK = 128
EPS = 1e-6
IN_SPLITS = (M_QK, M_QK, M_V, M_V, M_HEADS, M_HEADS, 2 * G_HALF, D_MODEL, D_MODEL)
D_IN = 2 * M_QK + 2 * M_V + 2 * M_HEADS + 2 * G_HALF + 2 * D_MODEL

kernel_name = 'hybrid_mlstm_sgu_hmoe_stream_step'


def rmsnorm(x, g):
    xf = x.astype(jnp.float32)
    y = xf * lax.rsqrt(jnp.mean(xf * xf, axis=-1, keepdims=True) + EPS)
    return (y * g.astype(jnp.float32)).astype(x.dtype)


def layernorm(x, g, b):
    xf = x.astype(jnp.float32)
    xc = xf - jnp.mean(xf, axis=-1, keepdims=True)
    y = xc * lax.rsqrt(jnp.mean(xc * xc, axis=-1, keepdims=True) + EPS)
    return (y * g.astype(jnp.float32) + b.astype(jnp.float32)).astype(x.dtype)


def modulate(h, shift, scale):
    return h * (1 + scale[:, None, :]) + shift[:, None, :]


def mlstm_chunk(carry, inp):
    C, n, m = carry
    q, k, v, i_log, f_log = inp
    L = q.shape[1]
    b = jnp.cumsum(f_log, axis=1)
    causal = jnp.tril(jnp.ones((L, L), dtype=bool))
    log_d = b[:, :, None, :] - b[:, None, :, :] + i_log[:, None, :, :]
    log_d = jnp.where(causal[None, :, :, None], log_d, -jnp.inf)
    log_inter = b + m[:, None, :]
    m_t = jnp.maximum(log_inter, jnp.max(log_d, axis=2))
    d_w = jnp.exp(log_d - m_t[:, :, None, :])
    inter_w = jnp.exp(log_inter - m_t)
    s = jnp.einsum('bthd,bshd->btsh', q, k) * d_w
    num = jnp.einsum('btsh,bshv->bthv', s, v) + inter_w[..., None] * jnp.einsum('bhvd,bthd->bthv', C, q)
    den = jnp.sum(s, axis=2) + inter_w * jnp.einsum('bhd,bthd->bth', n, q)
    h = num / jnp.maximum(jnp.abs(den), jnp.exp(-m_t))[..., None]
    m_new = m_t[:, -1]
    w_end = jnp.exp(b[:, -1:, :] - b + i_log - m_new[:, None, :])
    decay = jnp.exp(b[:, -1] + m - m_new)
    C_new = decay[..., None, None] * C + jnp.einsum('bsh,bshv,bshd->bhvd', w_end, v, k)
    n_new = decay[..., None] * n + jnp.einsum('bsh,bshd->bhd', w_end, k)
    return (C_new, n_new, m_new), h


def mlstm_branch(q, k, v, i_pre, f_pre, o_pre, C, n, m, b_i, b_f, mh_g):
    dt = q.dtype
    f32 = jnp.float32
    B, S, _ = q.shape
    q = q.astype(f32).reshape(B, S, M_HEADS, M_DK)
    k = k.astype(f32).reshape(B, S, M_HEADS, M_DK) * (M_DK ** -0.5)
    v = v.astype(f32).reshape(B, S, M_HEADS, M_DV)
    i_log = i_pre.astype(f32) + b_i.astype(f32)
    f_log = jax.nn.log_sigmoid(f_pre.astype(f32) + b_f.astype(f32))
    L = min(CHUNK, S)
    nc = S // L

    def to_blocks(a):
        return jnp.moveaxis(a.reshape(B, nc, L, *a.shape[2:]), 1, 0)

    carry0 = (C.astype(f32), n.astype(f32), m.astype(f32))
    (C1, n1, m1), h = lax.scan(mlstm_chunk, carry0, tuple(to_blocks(a) for a in (q, k, v, i_log, f_log)))
    h = jnp.moveaxis(h, 0, 1).reshape(B, S, M_HEADS, M_DV)
    h = h * lax.rsqrt(jnp.mean(h * h, axis=-1, keepdims=True) + EPS)
    h = h.reshape(B, S, M_V) * mh_g.astype(f32) * jax.nn.sigmoid(o_pre.astype(f32))
    return h.astype(dt), C1, n1, m1


def chunk_mlp_branch(z, ln_g, ln_b, w_s, b_s):
    u, v = jnp.split(z, 2, axis=-1)
    v = layernorm(v, ln_g, ln_b)
    B, S, _ = v.shape
    L = min(G_CHUNK, S)
    nck = S // L
    mask = jnp.tril(jnp.ones((L, L), dtype=bool))
    ws = jnp.where(mask[None], w_s[:, :L, :L], 0.0).astype(v.dtype)
    vg = v.reshape(B, nck, L, G_GROUPS, G_CH)
    mixed = jnp.einsum('gts,bnsgc->bntgc', ws, vg) + b_s[:, :L].T.astype(v.dtype)[None, None, :, :, None]
    return u * mixed.reshape(B, S, G_HALF), v


def hier_moe(h, w_rg, b_rg, w_re, b_re, w_eg, w_eu, w_ed):
    f32 = jnp.float32
    B, S, D = h.shape
    T = B * S
    t = h.reshape(T, D)
    pg = jax.nn.softmax(jnp.dot(t, w_rg).astype(f32) + b_rg.astype(f32), axis=-1)
    p_grp, g_sel = lax.top_k(pg, 1)
    le = (jnp.dot(t, w_re).astype(f32) + b_re.astype(f32)).reshape(T, E_GROUPS, E_PER_GROUP)
    le = jnp.take_along_axis(le, g_sel[:, :, None], axis=1)[:, 0]
    pe_top, e_sel = lax.top_k(jax.nn.softmax(le, axis=-1), TOP_K)
    pe_top = pe_top / jnp.sum(pe_top, axis=-1, keepdims=True)
    gate = p_grp * pe_top
    expert_id = (g_sel * E_PER_GROUP + e_sel).astype(jnp.int32)
    TK = T * TOP_K
    n_blocks = -(-(TK + N_EXPERTS * (MOE_BLOCK - 1)) // MOE_BLOCK)
    cap = n_blocks * MOE_BLOCK
    flat_e = expert_id.reshape(TK)
    flat_tok = jnp.repeat(jnp.arange(T, dtype=jnp.int32), TOP_K)
    flat_w = gate.reshape(TK)
    order = jnp.argsort(flat_e)
    se = flat_e[order]
    counts = jnp.bincount(flat_e, length=N_EXPERTS)
    padded = (counts + MOE_BLOCK - 1) // MOE_BLOCK * MOE_BLOCK
    pad_end = jnp.cumsum(padded)
    pad_start = pad_end - padded
    start = jnp.cumsum(counts) - counts
    dest = pad_start[se] + jnp.arange(TK) - start[se]
    tok_buf = jnp.zeros((cap,), jnp.int32).at[dest].set(flat_tok[order])
    w_buf = jnp.zeros((cap,), h.dtype).at[dest].set(flat_w[order].astype(h.dtype))
    blk_e = jnp.minimum(jnp.searchsorted(pad_end, jnp.arange(n_blocks) * MOE_BLOCK, side='right'), N_EXPERTS - 1)
    xb = t[tok_buf].reshape(n_blocks, MOE_BLOCK, D)

    def expert_block(args):
        xe, e = args
        return jnp.dot(jax.nn.silu(jnp.dot(xe, w_eg[e])) * jnp.dot(xe, w_eu[e]), w_ed[e])

    yb = lax.map(expert_block, (xb, blk_e)).reshape(cap, D)
    out = jnp.zeros_like(t).at[tok_buf].add(yb * w_buf[:, None])
    return out.reshape(B, S, D)


def layer(x, c, C, n, m, w_ada, b_ada, norm1_g, norm2_g, w_in, b_igate, b_fgate, mh_norm_g,
          sgu_ln_g, sgu_ln_b, w_spatial, b_spatial, w_branch_a, w_branch_b, w_out,
          w_rg, b_rg, w_re, b_re, w_eg, w_eu, w_ed):
    mod = jnp.dot(jax.nn.silu(c), w_ada) + b_ada
    sh1, sc1, gt1, sh2, sc2, gt2 = jnp.split(mod, 6, axis=-1)
    h = modulate(rmsnorm(x, norm1_g), sh1, sc1)
    split_points = np.cumsum(IN_SPLITS)[:-1].tolist()
    q, k, v, o_pre, i_pre, f_pre, z, ga, gb = jnp.split(jnp.dot(h, w_in), split_points, axis=-1)
    a, C1, n1, m1 = mlstm_branch(q, k, v, i_pre, f_pre, o_pre, C, n, m, b_igate, b_fgate, mh_norm_g)
    g_out, v_rows = chunk_mlp_branch(jax.nn.gelu(z), sgu_ln_g, sgu_ln_b, w_spatial, b_spatial)
    merged = jax.nn.sigmoid(ga) * jnp.dot(a, w_branch_a) + jax.nn.sigmoid(gb) * jnp.dot(g_out, w_branch_b)
    x = x + gt1[:, None, :] * jnp.dot(merged, w_out)
    h2 = modulate(rmsnorm(x, norm2_g), sh2, sc2)
    x = x + gt2[:, None, :] * hier_moe(h2, w_rg, b_rg, w_re, b_re, w_eg, w_eu, w_ed)
    return x, C1, n1, m1, v_rows


def trunk(x, c, C0, n0, m0, layer_params, final_g):
    Cs, ns, ms, vs = [], [], [], []
    for l in range(DEPTH):
        x, C1, n1, m1, v_rows = layer(x, c, C0[l], n0[l], m0[l], *[p[l] for p in layer_params])
        Cs.append(C1)
        ns.append(n1)
        ms.append(m1)
        vs.append(v_rows)
    return rmsnorm(x, final_g), jnp.stack(Cs), jnp.stack(ns), jnp.stack(ms), jnp.stack(vs)


def setup_inputs(seed: int = 0) -> dict:
    key = jax.random.key(seed)
    ks = jax.random.split(key, 32)
    D = D_MODEL

    def nrm(k, shape, s):
        return s * jax.random.normal(k, shape, jnp.float32)

    return {
        'x_prompt': nrm(ks[0], (BATCH, SEQ, D), 1.0),
        'x_sample': nrm(ks[1], (DEC_BATCH, DEC_SEQ, D), 1.0),
        'c_prompt': nrm(ks[2], (BATCH, D), 1.0),
        'c_sample': nrm(ks[3], (DEC_BATCH, D), 1.0),
        'state_mlstm_C': nrm(ks[4], (DEPTH, DEC_BATCH, M_HEADS, M_DV, M_DK), 1.0),
        'state_mlstm_n': nrm(ks[5], (DEPTH, DEC_BATCH, M_HEADS, M_DK), 1.0),
        'state_mlstm_m': nrm(ks[6], (DEPTH, DEC_BATCH, M_HEADS), 0.5),
        'w_ada': nrm(ks[7], (DEPTH, D, 6 * D), 0.5 * D ** -0.5),
        'b_ada': nrm(ks[8], (DEPTH, 6 * D), 0.02),
        'norm1_g': 1.0 + nrm(ks[9], (DEPTH, D), 0.02),
        'norm2_g': 1.0 + nrm(ks[10], (DEPTH, D), 0.02),
        'w_in': nrm(ks[11], (DEPTH, D, D_IN), D ** -0.5),
        'b_igate': nrm(ks[12], (DEPTH, M_HEADS), 0.1),
        'b_fgate': jnp.linspace(3.0, 6.0, M_HEADS, dtype=jnp.float32)[None, :] + nrm(ks[13], (DEPTH, M_HEADS), 0.1),
        'mh_norm_g': 1.0 + nrm(ks[14], (DEPTH, M_V), 0.02),
        'sgu_ln_g': 1.0 + nrm(ks[15], (DEPTH, G_HALF), 0.02),
        'sgu_ln_b': nrm(ks[16], (DEPTH, G_HALF), 0.02),
        'w_spatial': nrm(ks[17], (DEPTH, G_GROUPS, G_CHUNK, G_CHUNK), G_CHUNK ** -0.5),
        'b_spatial': 1.0 + nrm(ks[18], (DEPTH, G_GROUPS, G_CHUNK), 0.1),
        'w_branch_a': nrm(ks[19], (DEPTH, M_V, D), M_V ** -0.5),
        'w_branch_b': nrm(ks[20], (DEPTH, G_HALF, D), G_HALF ** -0.5),
        'w_out': nrm(ks[21], (DEPTH, D, D), D ** -0.5),
        'w_router_group': nrm(ks[22], (DEPTH, D, E_GROUPS), D ** -0.5),
        'b_router_group': nrm(ks[23], (DEPTH, E_GROUPS), 0.01),
        'w_router_expert': nrm(ks[24], (DEPTH, D, N_EXPERTS), D ** -0.5),
        'b_router_expert': nrm(ks[25], (DEPTH, N_EXPERTS), 0.01),
        'w_expert_gate': nrm(ks[26], (DEPTH, N_EXPERTS, D, D_EXPERT), D ** -0.5),
        'w_expert_up': nrm(ks[27], (DEPTH, N_EXPERTS, D, D_EXPERT), D ** -0.5),
        'w_expert_down': nrm(ks[28], (DEPTH, N_EXPERTS, D_EXPERT, D), D_EXPERT ** -0.5),
        'final_norm_g': 1.0 + nrm(ks[29], (D,), 0.02),
    }


def reference(x_prompt, x_sample, c_prompt, c_sample, state_mlstm_C, state_mlstm_n, state_mlstm_m,
              w_ada, b_ada, norm1_g, norm2_g, w_in, b_igate, b_fgate, mh_norm_g, sgu_ln_g, sgu_ln_b,
              w_spatial, b_spatial, w_branch_a, w_branch_b, w_out, w_router_group, b_router_group,
              w_router_expert, b_router_expert, w_expert_gate, w_expert_up, w_expert_down, final_norm_g):
    layer_params = (w_ada, b_ada, norm1_g, norm2_g, w_in, b_igate, b_fgate, mh_norm_g, sgu_ln_g, sgu_ln_b,
                    w_spatial, b_spatial, w_branch_a, w_branch_b, w_out, w_router_group, b_router_group,
                    w_router_expert, b_router_expert, w_expert_gate, w_expert_up, w_expert_down)
    bp = x_prompt.shape[0]
    C0 = jnp.zeros((DEPTH, bp, M_HEADS, M_DV, M_DK), jnp.float32)
    n0 = jnp.zeros((DEPTH, bp, M_HEADS, M_DK), jnp.float32)
    m0 = jnp.zeros((DEPTH, bp, M_HEADS), jnp.float32)
    y_prompt, C_p, n_p, m_p, _ = trunk(x_prompt, c_prompt, C0, n0, m0, layer_params, final_norm_g)
    y_sample, C_s, n_s, m_s, v_s = trunk(x_sample, c_sample, state_mlstm_C, state_mlstm_n, state_mlstm_m,
                                         layer_params, final_norm_g)
    return (y_prompt, y_sample, C_p, n_p, m_p, C_s, n_s, m_s, v_s)
```

```python
import functools
from typing import NamedTuple

import jax
import jax.numpy as jnp
from jax import lax
from jax.experimental import pallas as pl
from jax.experimental.pallas import tpu as pltpu

F32, BF16, U32, I32 = jnp.float32, jnp.bfloat16, jnp.uint32, jnp.int32
EPS = 1e-6
NEG = -1e30
LANES = 128
ROW_TILE = 512
ROW_CHUNK = 16
MLSTM_CHUNK = 128
MOE_BLOCK = 256
TOP_K = 2
MIB = 1024 * 1024
SH1, SC1, GT1, SH2, SC2, GT2 = range(6)


class Dims(NamedTuple):
    Bp: int
    S: int
    Bs: int
    DS: int
    D: int
    DEPTH: int
    H: int
    DK: int
    DV: int
    GH: int
    G: int
    GC: int
    EG: int
    NE: int
    DE: int

    @property
    def QK(self):
        return self.H * self.DK

    @property
    def V(self):
        return self.H * self.DV

    @property
    def Tp(self):
        return self.Bp * self.S

    @property
    def Ts(self):
        return self.Bs * self.DS

    @property
    def Tc(self):
        return self.Tp + ROW_TILE

    @property
    def n_tiles(self):
        return self.Tc // ROW_TILE

    @property
    def n_mod(self):
        return self.Bp + ROW_TILE // ROW_CHUNK

    @property
    def off_v(self):
        return 2 * self.QK

    @property
    def off_o(self):
        return 2 * self.QK + self.V

    @property
    def off_z(self):
        return 2 * self.QK + 2 * self.V

    @property
    def off_ga(self):
        return self.off_z + 2 * self.GH

    @property
    def off_gb(self):
        return self.off_ga + self.D

    @property
    def n_main(self):
        return self.off_gb + self.D

    @property
    def cap(self):
        tk = self.Tc * TOP_K
        return -(-(tk + self.NE * (MOE_BLOCK - 1)) // MOE_BLOCK) * MOE_BLOCK


def _cparams(semantics, vmem_mib):
    return pltpu.CompilerParams(dimension_semantics=semantics, vmem_limit_bytes=int(vmem_mib * MIB))


def _pick_tile(n, cap, unit=256):
    best = None
    for t in range(unit, min(n, cap) + 1, unit):
        if n % t == 0:
            best = t
    assert best is not None, (n, cap)
    return best


def _dot(a, b):
    return jnp.dot(a, b, preferred_element_type=F32)


def _split_bf16(x):
    hi = x.astype(BF16)
    lo = (x - hi.astype(F32)).astype(BF16)
    return hi, lo


def _dot3(a_hi, a_lo, w_hi, w_lo):
    return _dot(a_hi, w_hi) + _dot(a_hi, w_lo) + _dot(a_lo, w_hi)


def _rms_mod(x, g, sc, sh):
    y = x * lax.rsqrt(jnp.mean(x * x, axis=-1, keepdims=True) + EPS) * g
    return y * (1.0 + sc) + sh


def _pack_pair(x):
    d2 = x.shape[-1] // 2
    bits = lax.bitcast_convert_type(x.astype(BF16).astype(F32), U32)
    return (bits[:, :d2] >> 16) | (bits[:, d2:] & jnp.uint32(0xFFFF0000))


def _unpack_pair(p):
    lo = lax.bitcast_convert_type(p << 16, F32)
    hi = lax.bitcast_convert_type(p & jnp.uint32(0xFFFF0000), F32)
    return lo, hi


def _adaln_kernel(c_ref, w_ref, b_ref, o_ref):
    c = c_ref[...]
    s = (c * jax.nn.sigmoid(c)).astype(BF16)
    o_ref[...] = _dot(s, w_ref[...].astype(BF16)) + b_ref[...]


def _adaln(c_all, w_ada, b_ada):
    depth, d, n6 = w_ada.shape
    r = c_all.shape[0]
    tn = _pick_tile(n6, 1024)
    return pl.pallas_call(
        _adaln_kernel,
        grid=(depth, n6 // tn),
        in_specs=[pl.BlockSpec((r, d), lambda l, j: (0, 0)),
                  pl.BlockSpec((None, d, tn), lambda l, j: (l, 0, j)),
                  pl.BlockSpec((None, 1, tn), lambda l, j: (l, 0, j))],
        out_specs=pl.BlockSpec((None, r, tn), lambda l, j: (l, 0, j)),
        out_shape=jax.ShapeDtypeStruct((depth, r, n6), F32),
        compiler_params=_cparams(("arbitrary", "arbitrary"), 2 * d * tn * 4 / MIB + 3 * d * tn * 2 / MIB + 8),
        name="adaln",
    )(c_all, w_ada, b_ada.reshape(depth, 1, n6))


def _norm_proj_kernel(*refs, first, dm, k_sc, k_sh):
    if first:
        xp_ref, xs_ref, g_ref, mod_ref, w_ref, h_ref, p_ref, xc_ref, lo_scr = refs
    else:
        x_ref, g_ref, mod_ref, w_ref, h_ref, p_ref, lo_scr = refs
    i = pl.program_id(0)
    is_s = i == dm.n_tiles - 1
    bp = jnp.minimum(i // (dm.S // ROW_TILE), dm.Bp - 1)
    g = g_ref[...]

    def chunk(j, carry):
        rows = pl.ds(pl.multiple_of(j * ROW_CHUNK, ROW_CHUNK), ROW_CHUNK)
        if first:
            x = jnp.where(is_s, xs_ref[rows, :], xp_ref[rows, :])
            xc_ref[rows, :] = x
        else:
            x = x_ref[rows, :]
        idx = jnp.where(is_s, dm.Bp + j, bp)
        h = _rms_mod(x, g, mod_ref[k_sc, pl.ds(idx, 1), :], mod_ref[k_sh, pl.ds(idx, 1), :])
        hi, lo = _split_bf16(h)
        h_ref[rows, :] = hi
        lo_scr[rows, :] = lo
        return carry

    lax.fori_loop(0, ROW_TILE // ROW_CHUNK, chunk, 0)
    w_hi, w_lo = _split_bf16(w_ref[...])
    p_ref[...] = _dot3(h_ref[...], lo_scr[...], w_hi, w_lo)


def _norm_proj(dm, l, x_in, g, mod, w_small, k_sc, k_sh):
    d, tc, nt = dm.D, dm.Tc, dm.n_tiles
    first = isinstance(x_in, tuple)
    np_tiles = dm.Tp // ROW_TILE
    tile = lambda i: (i, 0)
    if first:
        x_args = list(x_in)
        x_specs = [pl.BlockSpec((ROW_TILE, d), lambda i: (jnp.minimum(i, np_tiles - 1), 0)),
                   pl.BlockSpec((ROW_TILE, d), lambda i: (0, 0))]
    else:
        x_args = [x_in]
        x_specs = [pl.BlockSpec((ROW_TILE, d), tile)]
    out_shape = [jax.ShapeDtypeStruct((tc, d), BF16), jax.ShapeDtypeStruct((tc, LANES), F32)]
    out_specs = [pl.BlockSpec((ROW_TILE, d), tile), pl.BlockSpec((ROW_TILE, LANES), tile)]
    if first:
        out_shape.append(jax.ShapeDtypeStruct((tc, d), F32))
        out_specs.append(pl.BlockSpec((ROW_TILE, d), tile))
    return pl.pallas_call(
        functools.partial(_norm_proj_kernel, first=first, dm=dm, k_sc=k_sc, k_sh=k_sh),
        grid=(nt,),
        in_specs=x_specs + [pl.BlockSpec((None, 1, d), lambda i: (l, 0, 0)),
                            pl.BlockSpec((None, 6, dm.n_mod, d), lambda i: (l, 0, 0, 0)),
                            pl.BlockSpec((d, LANES), lambda i: (0, 0))],
        out_specs=out_specs,
        out_shape=out_shape,
        scratch_shapes=[pltpu.VMEM((ROW_TILE, d), BF16)],
        compiler_params=_cparams(("arbitrary",), ROW_TILE * d * 40 / MIB + 16),
        name="norm_proj",
    )(*x_args, g, mod, w_small)


def _matmul_kernel(a_ref, w_ref, o_ref):
    o_ref[...] = _dot(a_ref[...], w_ref[...]).astype(o_ref.dtype)


def _matmul(a, w, tm, tn, out_dtype):
    m, k = a.shape
    n = w.shape[1]
    blocks = tm * k * 2 + k * tn * 2 + tm * tn * 2
    return pl.pallas_call(
        _matmul_kernel,
        grid=(n // tn, m // tm),
        in_specs=[pl.BlockSpec((tm, k), lambda j, i: (i, 0)),
                  pl.BlockSpec((k, tn), lambda j, i: (0, j))],
        out_specs=pl.BlockSpec((tm, tn), lambda j, i: (i, j)),
        out_shape=jax.ShapeDtypeStruct((m, n), out_dtype),
        compiler_params=_cparams(("arbitrary", "arbitrary"), (2 * blocks + tm * tn * 4) / MIB + 8),
        name="in_proj",
    )(a, w)


def _pad_rows(x, rows):
    if x.shape[0] == rows:
        return x
    return jnp.concatenate([x, jnp.zeros((rows - x.shape[0], x.shape[1]), x.dtype)], axis=0)


def _mlstm_kernel(*refs, dm, lv, n_seq):
    a_ref = refs[10]
    seq_ok = pl.program_id(0) < n_seq

    @pl.when(seq_ok)
    def _run():
        _mlstm_step(*refs, dm=dm, lv=lv)

    @pl.when(jnp.logical_not(seq_ok))
    def _fill():
        a_ref[...] = jnp.zeros_like(a_ref)


def _mlstm_step(q_ref, k_ref, v_ref, o_ref, gt_ref, bias_ref, mhg_ref, c0_ref, n0_ref, m0_ref,
                a_ref, cout_ref, nout_ref, mout_ref, ct_scr, n_scr, m_scr, *, dm, lv):
    h_n, dk, dv = dm.H, dm.DK, dm.DV
    L = MLSTM_CHUNK
    c = pl.program_id(1)

    @pl.when(c == 0)
    def _init():
        for h in range(h_n):
            ct_scr[h] = c0_ref[h].T
        n_scr[...] = n0_ref[...]
        m_scr[...] = m0_ref[...]

    row = lax.broadcasted_iota(I32, (L, L), 0)
    col = lax.broadcasted_iota(I32, (L, L), 1)
    causal = col <= row
    eye = col == row
    lane = lax.broadcasted_iota(I32, (L, LANES), 1)
    rowl = lax.broadcasted_iota(I32, (L, LANES), 0)

    xg = _pad_rows(gt_ref[...], L) + bias_ref[...]
    f_log = jnp.minimum(xg, 0.0) - jnp.log1p(jnp.exp(-jnp.abs(xg)))
    gl = jnp.where(lane < h_n, xg, f_log)
    gl = jnp.where(rowl < lv, gl, 0.0)
    tri = causal.astype(BF16)
    g_hi = gl.astype(BF16)
    r1 = gl - g_hi.astype(F32)
    g_mid = r1.astype(BF16)
    g_lo = (r1 - g_mid.astype(F32)).astype(BF16)
    cum = _dot(tri, g_hi) + _dot(tri, g_mid) + _dot(tri, g_lo)
    b_all = pltpu.roll(cum, LANES - h_n, axis=1)
    a_all = jnp.where(rowl < lv, gl - b_all, NEG)

    scale = dk ** -0.5
    kf = _pad_rows(k_ref[...], L).astype(F32) * scale
    kt_all = kf.T.astype(BF16)
    q_all = _pad_rows(q_ref[...], L)
    v_all = _pad_rows(v_ref[...], L)
    m_old = m_scr[...]
    m_new = m_old
    lane1 = lax.broadcasted_iota(I32, (1, LANES), 1)

    for h in range(h_n):
        a_col = a_all[:, h:h + 1]
        b_col = b_all[:, h:h + 1]
        a_row = jnp.sum(jnp.where(eye, a_col, 0.0), axis=0, keepdims=True)
        m_prev = m_old[:, h:h + 1]
        big_m = jnp.maximum(jnp.max(jnp.where(causal, a_row, NEG), axis=1, keepdims=True), m_prev)
        d_w = jnp.where(causal, jnp.exp(jnp.minimum(a_row - big_m, 0.0)), 0.0)
        inter = jnp.exp(m_prev - big_m)
        qh = q_all[:, h * dk:(h + 1) * dk]
        vh = v_all[:, h * dv:(h + 1) * dv]
        kt = kt_all[h * dk:(h + 1) * dk, :]
        sd = _dot(qh, kt) * d_w
        ct = ct_scr[h]
        nvec = n_scr[h:h + 1, :]
        num = _dot(sd.astype(BF16), vh) + inter * _dot(qh, ct.astype(BF16))
        den = (jnp.sum(sd, axis=1, keepdims=True)
               + inter * jnp.sum(qh.astype(F32) * nvec, axis=1, keepdims=True))
        hh = num / jnp.maximum(jnp.abs(den), jnp.exp(-(b_col + big_m)))
        hn = hh * lax.rsqrt(jnp.mean(hh * hh, axis=1, keepdims=True) + EPS)
        gate_o = jax.nn.sigmoid(o_ref[:, h * dv:(h + 1) * dv].astype(F32))
        a_ref[:, h * dv:(h + 1) * dv] = (hn[:lv] * mhg_ref[:, h * dv:(h + 1) * dv] * gate_o).astype(BF16)
        m_last = big_m[L - 1:L, :]
        decay = jnp.exp(m_prev - m_last)
        w_col = jnp.exp(a_col - m_last)
        vw = (vh.astype(F32) * w_col).astype(BF16)
        ct_scr[h] = decay * ct + _dot(kt, vw)
        n_scr[h:h + 1, :] = decay * nvec + jnp.sum(kf[:, h * dk:(h + 1) * dk] * w_col, axis=0, keepdims=True)
        m_new = jnp.where(lane1 == h, b_col[L - 1:L, :] + m_last, m_new)
    m_scr[...] = m_new

    @pl.when(c == pl.num_programs(1) - 1)
    def _fin():
        for h in range(h_n):
            cout_ref[h] = ct_scr[h].T
        nout_ref[...] = n_scr[...]
        mout_ref[...] = m_scr[...]


def _mlstm(dm, proj, gates, bias, mhg, c0, n0, m0, *, sample):
    if sample:
        b_n, n_grid, nc, lv, row0, n_rows = dm.Bs, ROW_TILE // dm.DS, 1, dm.DS, dm.Tp // dm.DS, ROW_TILE
    else:
        b_n, n_grid, nc, lv, row0, n_rows = dm.Bp, dm.Bp, dm.S // MLSTM_CHUNK, MLSTM_CHUNK, 0, dm.Tp
    h_n, dk, dv, qk, v = dm.H, dm.DK, dm.DV, dm.QK, dm.V
    rb = lambda b, c: row0 + b * nc + c
    sq = lambda b: jnp.minimum(b, b_n - 1)
    in_specs = [
        pl.BlockSpec((lv, qk), lambda b, c: (rb(b, c), 0)),
        pl.BlockSpec((lv, qk), lambda b, c: (rb(b, c), 1)),
        pl.BlockSpec((lv, v), lambda b, c: (rb(b, c), dm.off_v // v)),
        pl.BlockSpec((lv, v), lambda b, c: (rb(b, c), dm.off_o // v)),
        pl.BlockSpec((lv, LANES), lambda b, c: (rb(b, c), 0)),
        pl.BlockSpec((1, LANES), lambda b, c: (0, 0)),
        pl.BlockSpec((1, v), lambda b, c: (0, 0)),
        pl.BlockSpec((None, h_n, dv, dk), lambda b, c: (sq(b), 0, 0, 0)),
        pl.BlockSpec((None, h_n, dk), lambda b, c: (sq(b), 0, 0)),
        pl.BlockSpec((None, 1, LANES), lambda b, c: (sq(b), 0, 0)),
    ]
    return pl.pallas_call(
        functools.partial(_mlstm_kernel, dm=dm, lv=lv, n_seq=b_n),
        grid=(n_grid, nc),
        in_specs=in_specs,
        out_specs=[pl.BlockSpec((lv, v), lambda b, c: (b * nc + c, 0)),
                   pl.BlockSpec((None, h_n, dv, dk), lambda b, c: (sq(b), 0, 0, 0)),
                   pl.BlockSpec((None, h_n, dk), lambda b, c: (sq(b), 0, 0)),
                   pl.BlockSpec((None, 1, LANES), lambda b, c: (sq(b), 0, 0))],
        out_shape=[jax.ShapeDtypeStruct((n_rows, v), BF16),
                   jax.ShapeDtypeStruct((b_n, h_n, dv, dk), F32),
                   jax.ShapeDtypeStruct((b_n, h_n, dk), F32),
                   jax.ShapeDtypeStruct((b_n, 1, LANES), F32)],
        scratch_shapes=[pltpu.VMEM((h_n, dk, dv), F32), pltpu.VMEM((h_n, dk), F32), pltpu.VMEM((1, LANES), F32)],
        compiler_params=_cparams(("arbitrary", "arbitrary"), 40),
        name="mlstm_sample" if sample else "mlstm_prompt",
    )(proj, proj, proj, proj, gates, bias, mhg, c0, n0, m0)


def _sgu_kernel(*refs, dm, lv, n_seq, emit_v):
    g_ref = refs[5]
    seq_ok = pl.program_id(0) < n_seq

    @pl.when(seq_ok)
    def _run():
        _sgu_step(*refs, dm=dm, lv=lv, emit_v=emit_v)

    @pl.when(jnp.logical_not(seq_ok))
    def _fill():
        g_ref[...] = jnp.zeros_like(g_ref)


def _sgu_step(*refs, dm, lv, emit_v):
    z_ref, lng_ref, lnb_ref, ws_ref, bst_ref, g_ref = refs[:6]
    gh, gch = dm.GH, dm.GH // dm.G
    z = jax.nn.gelu(z_ref[...].astype(F32))
    u, v = z[:, :gh], z[:, gh:]
    xc = v - jnp.mean(v, axis=-1, keepdims=True)
    vn = xc * lax.rsqrt(jnp.mean(xc * xc, axis=-1, keepdims=True) + EPS) * lng_ref[...] + lnb_ref[...]
    if emit_v:
        refs[6][...] = vn
    row = lax.broadcasted_iota(I32, (lv, lv), 0)
    col = lax.broadcasted_iota(I32, (lv, lv), 1)
    vb = vn.astype(BF16)
    for g in range(dm.G):
        w = jnp.where(col <= row, ws_ref[g, :lv, :lv], 0.0).astype(BF16)
        mixed = _dot(w, vb[:, g * gch:(g + 1) * gch]) + bst_ref[:lv, g:g + 1]
        g_ref[:, g * gch:(g + 1) * gch] = (u[:, g * gch:(g + 1) * gch] * mixed).astype(BF16)


def _sgu(dm, proj, lng, lnb, ws, bst, *, sample):
    if sample:
        b_n, n_grid, nc, lv, row0, n_rows = dm.Bs, ROW_TILE // dm.DS, 1, dm.DS, dm.Tp // dm.DS, ROW_TILE
    else:
        b_n, n_grid, nc, lv, row0, n_rows = dm.Bp, dm.Bp, dm.S // dm.GC, dm.GC, 0, dm.Tp
    gh = dm.GH
    rb = lambda b, c: row0 + b * nc + c
    in_specs = [
        pl.BlockSpec((lv, 2 * gh), lambda b, c: (rb(b, c), dm.off_z // (2 * gh))),
        pl.BlockSpec((1, gh), lambda b, c: (0, 0)),
        pl.BlockSpec((1, gh), lambda b, c: (0, 0)),
        pl.BlockSpec((dm.G, dm.GC, dm.GC), lambda b, c: (0, 0, 0)),
        pl.BlockSpec((dm.GC, LANES), lambda b, c: (0, 0)),
    ]
    out_specs = [pl.BlockSpec((lv, gh), lambda b, c: (b * nc + c, 0))]
    out_shape = [jax.ShapeDtypeStruct((n_rows, gh), BF16)]
    if sample:
        out_specs.append(pl.BlockSpec((None, lv, gh), lambda b, c: (jnp.minimum(b, b_n - 1), 0, 0)))
        out_shape.append(jax.ShapeDtypeStruct((b_n, lv, gh), F32))
    return pl.pallas_call(
        functools.partial(_sgu_kernel, dm=dm, lv=lv, n_seq=b_n, emit_v=sample),
        grid=(n_grid, nc),
        in_specs=in_specs,
        out_specs=out_specs,
        out_shape=out_shape,
        compiler_params=_cparams(("arbitrary", "arbitrary"), 24),
        name="sgu_sample" if sample else "sgu_prompt",
    )(proj, lng, lnb, ws, bst)


def _merge_kernel(ap_ref, as_ref, gp_ref, gs_ref, ga_ref, gb_ref, wa_ref, wb_ref, o_ref):
    is_s = pl.program_id(1) == pl.num_programs(1) - 1
    pa = _dot(jnp.where(is_s, as_ref[...], ap_ref[...]), wa_ref[...])
    pb = _dot(jnp.where(is_s, gs_ref[...], gp_ref[...]), wb_ref[...])
    o_ref[...] = (jax.nn.sigmoid(ga_ref[...].astype(F32)) * pa
                  + jax.nn.sigmoid(gb_ref[...].astype(F32)) * pb).astype(o_ref.dtype)


def _merge(dm, a_p, a_s, g_p, g_s, proj, wa, wb):
    tc, d, v, gh = dm.Tc, dm.D, dm.V, dm.GH
    tm = ROW_TILE
    tn = _pick_tile(d, 1024)
    np_tiles = dm.Tp // tm
    prompt = lambda j, i: (jnp.minimum(i, np_tiles - 1), 0)
    const = lambda j, i: (0, 0)
    blocks = 2 * tm * (v + gh) * 2 + 3 * tm * tn * 2 + (v + gh) * tn * 2
    return pl.pallas_call(
        _merge_kernel,
        grid=(d // tn, tc // tm),
        in_specs=[pl.BlockSpec((tm, v), prompt),
                  pl.BlockSpec((tm, v), const),
                  pl.BlockSpec((tm, gh), prompt),
                  pl.BlockSpec((tm, gh), const),
                  pl.BlockSpec((tm, tn), lambda j, i: (i, dm.off_ga // tn + j)),
                  pl.BlockSpec((tm, tn), lambda j, i: (i, dm.off_gb // tn + j)),
                  pl.BlockSpec((v, tn), lambda j, i: (0, j)),
                  pl.BlockSpec((gh, tn), lambda j, i: (0, j))],
        out_specs=pl.BlockSpec((tm, tn), lambda j, i: (i, j)),
        out_shape=jax.ShapeDtypeStruct((tc, d), BF16),
        compiler_params=_cparams(("arbitrary", "arbitrary"), (2 * blocks + 6 * tm * tn * 4) / MIB + 8),
        name="merge",
    )(a_p, a_s, g_p, g_s, proj, proj, wa, wb)


def _route(logits, eg, ne):
    epg = ne // eg
    lane = lax.broadcasted_iota(I32, logits.shape, 1)
    gmask = lane < eg
    gmax = jnp.max(jnp.where(gmask, logits, NEG), axis=1, keepdims=True)
    gexp = jnp.where(gmask, jnp.exp(jnp.minimum(logits - gmax, 0.0)), 0.0)
    pg = gexp / jnp.sum(gexp, axis=1, keepdims=True)
    p_grp = jnp.max(pg, axis=1, keepdims=True)
    g_sel = jnp.min(jnp.where(gmask & (pg == p_grp), lane, LANES), axis=1, keepdims=True)
    lo = eg + g_sel * epg
    emask = (lane >= lo) & (lane < lo + epg)
    emax = jnp.max(jnp.where(emask, logits, NEG), axis=1, keepdims=True)
    eexp = jnp.where(emask, jnp.exp(jnp.minimum(logits - emax, 0.0)), 0.0)
    pe = eexp / jnp.sum(eexp, axis=1, keepdims=True)
    p1 = jnp.max(jnp.where(emask, pe, -1.0), axis=1, keepdims=True)
    i1 = jnp.min(jnp.where(emask & (pe == p1), lane, LANES), axis=1, keepdims=True)
    rest = emask & (lane != i1)
    p2 = jnp.max(jnp.where(rest, pe, -1.0), axis=1, keepdims=True)
    i2 = jnp.min(jnp.where(rest & (pe == p2), lane, LANES), axis=1, keepdims=True)
    psum = p1 + p2
    eid = jnp.where(lane == 0, i1 - eg, jnp.where(lane == 1, i2 - eg, 0))
    gate = jnp.where(lane == 0, p_grp * (p1 / psum), jnp.where(lane == 1, p_grp * (p2 / psum), 0.0))
    return eid, gate


def _outproj_kernel(m_ref, x_ref, w_ref, g_ref, mod_ref, wr_ref, rb_ref,
                    x1_ref, hp_ref, eid_ref, gate_ref, acc_scr, hi_scr, lo_scr, *, dm):
    i = pl.program_id(0)
    is_s = i == dm.n_tiles - 1
    bp = jnp.minimum(i // (dm.S // ROW_TILE), dm.Bp - 1)
    acc_scr[...] = _dot(m_ref[...], w_ref[...])
    g = g_ref[...]

    def chunk(j, carry):
        rows = pl.ds(pl.multiple_of(j * ROW_CHUNK, ROW_CHUNK), ROW_CHUNK)
        idx = jnp.where(is_s, dm.Bp + j, bp)
        x1 = x_ref[rows, :] + mod_ref[GT1, pl.ds(idx, 1), :] * acc_scr[rows, :]
        x1_ref[rows, :] = x1
        h = _rms_mod(x1, g, mod_ref[SC2, pl.ds(idx, 1), :], mod_ref[SH2, pl.ds(idx, 1), :])
        hi, lo = _split_bf16(h)
        hi_scr[rows, :] = hi
        lo_scr[rows, :] = lo
        hp_ref[rows, :] = _pack_pair(h)
        return carry

    lax.fori_loop(0, ROW_TILE // ROW_CHUNK, chunk, 0)
    w_hi, w_lo = _split_bf16(wr_ref[...])
    logits = _dot3(hi_scr[...], lo_scr[...], w_hi, w_lo) + rb_ref[...]
    eid, gate = _route(logits, dm.EG, dm.NE)
    eid_ref[...] = eid
    gate_ref[...] = gate


def _outproj(dm, l, merged, x, w_out, g2, mod, w_router, b_router):
    d, tc = dm.D, dm.Tc
    tile = lambda i: (i, 0)
    const = lambda i: (0, 0)
    return pl.pallas_call(
        functools.partial(_outproj_kernel, dm=dm),
        grid=(dm.n_tiles,),
        in_specs=[pl.BlockSpec((ROW_TILE, d), tile),
                  pl.BlockSpec((ROW_TILE, d), tile),
                  pl.BlockSpec((d, d), const),
                  pl.BlockSpec((None, 1, d), lambda i: (l, 0, 0)),
                  pl.BlockSpec((None, 6, dm.n_mod, d), lambda i: (l, 0, 0, 0)),
                  pl.BlockSpec((d, LANES), const),
                  pl.BlockSpec((1, LANES), const)],
        out_specs=[pl.BlockSpec((ROW_TILE, d), tile),
                   pl.BlockSpec((ROW_TILE, d // 2), tile),
                   pl.BlockSpec((ROW_TILE, LANES), tile),
                   pl.BlockSpec((ROW_TILE, LANES), tile)],
        out_shape=[jax.ShapeDtypeStruct((tc, d), F32),
                   jax.ShapeDtypeStruct((tc, d // 2), U32),
                   jax.ShapeDtypeStruct((tc, LANES), I32),
                   jax.ShapeDtypeStruct((tc, LANES), F32)],
        scratch_shapes=[pltpu.VMEM((ROW_TILE, d), F32), pltpu.VMEM((ROW_TILE, d), BF16),
                        pltpu.VMEM((ROW_TILE, d), BF16)],
        compiler_params=_cparams(("arbitrary",), (4 * d * d + ROW_TILE * d * 40) / MIB + 12),
        name="out_proj",
    )(merged, x, w_out, g2, mod, w_router, b_router)


def _rank_kernel(eid_ref, rank_ref, cnt_ref, run_scr):
    i = pl.program_id(0)

    @pl.when(i == 0)
    def _init():
        run_scr[...] = jnp.zeros_like(run_scr)

    eid = eid_ref[...]
    lane = lax.broadcasted_iota(I32, eid.shape, 1)
    e0 = lane == eid[:, 0:1]
    e1 = lane == eid[:, 1:2]
    hot = (e0 | e1).astype(BF16)
    n = eid.shape[0]
    strict = (lax.broadcasted_iota(I32, (n, n), 1) < lax.broadcasted_iota(I32, (n, n), 0)).astype(BF16)
    before = _dot(strict, hot) + run_scr[...]
    r0 = jnp.sum(jnp.where(e0, before, 0.0), axis=1, keepdims=True)
    r1 = jnp.sum(jnp.where(e1, before, 0.0), axis=1, keepdims=True)
    rank_ref[...] = jnp.where(lane == 0, r0, jnp.where(lane == 1, r1, 0.0)).astype(I32)
    run_scr[...] = run_scr[...] + jnp.sum(hot.astype(F32), axis=0, keepdims=True)
    cnt_ref[...] = run_scr[...].astype(I32)


def _rank(dm, eid):
    tile = lambda i: (i, 0)
    return pl.pallas_call(
        _rank_kernel,
        grid=(dm.n_tiles,),
        in_specs=[pl.BlockSpec((ROW_TILE, LANES), tile)],
        out_specs=[pl.BlockSpec((ROW_TILE, LANES), tile), pl.BlockSpec((1, LANES), lambda i: (0, 0))],
        out_shape=[jax.ShapeDtypeStruct((dm.Tc, LANES), I32), jax.ShapeDtypeStruct((1, LANES), I32)],
        scratch_shapes=[pltpu.VMEM((1, LANES), F32)],
        compiler_params=_cparams(("arbitrary",), 16),
        name="moe_rank",
    )(eid)


def _dispatch_kernel(dest_ref, h_ref, xs_in_ref, xs_ref, sem):
    del xs_in_ref

    def row_copy(r, k):
        return pltpu.make_async_copy(h_ref.at[pl.ds(r, 1)], xs_ref.at[pl.ds(dest_ref[TOP_K * r + k], 1)], sem)

    def issue(r, carry):
        for k in range(TOP_K):
            row_copy(r, k).start()
        return carry

    def drain(r, carry):
        for k in range(TOP_K):
            row_copy(r, k).wait()
        return carry

    lax.fori_loop(0, ROW_TILE, issue, 0)
    lax.fori_loop(0, ROW_TILE, drain, 0)


def _dispatch(dm, dest, hp, xs_zero):
    d2 = dm.D // 2
    return pl.pallas_call(
        _dispatch_kernel,
        grid=(dm.n_tiles,),
        in_specs=[pl.BlockSpec((ROW_TILE * TOP_K,), lambda i: (i,), memory_space=pltpu.SMEM),
                  pl.BlockSpec((ROW_TILE, d2), lambda i: (i, 0)),
                  pl.BlockSpec(memory_space=pl.ANY)],
        out_specs=pl.BlockSpec(memory_space=pl.ANY),
        out_shape=jax.ShapeDtypeStruct((dm.cap, d2), U32),
        scratch_shapes=[pltpu.SemaphoreType.DMA],
        input_output_aliases={2: 0},
        compiler_params=_cparams(("arbitrary",), 16),
        name="moe_dispatch",
    )(dest, hp, xs_zero)


def _expert_kernel(be_ref, nu_ref, xs_ref, wg_ref, wu_ref, wd_ref, ys_ref, wg_s, wu_s, wd_s):
    j = pl.program_id(0)
    changed = (j == 0) | (be_ref[j] != be_ref[jnp.maximum(j - 1, 0)])

    @pl.when(changed)
    def _cast():
        wg_s[...] = wg_ref[...].astype(BF16)
        wu_s[...] = wu_ref[...].astype(BF16)
        wd_s[...] = wd_ref[...].astype(BF16)

    @pl.when(j < nu_ref[0])
    def _compute():
        d2 = xs_ref.shape[1]
        lo, hi = _unpack_pair(xs_ref[...])
        lo, hi = lo.astype(BF16), hi.astype(BF16)
        hg = _dot(lo, wg_s[:d2, :]) + _dot(hi, wg_s[d2:, :])
        hu = _dot(lo, wu_s[:d2, :]) + _dot(hi, wu_s[d2:, :])
        act = (hg * jax.nn.sigmoid(hg) * hu).astype(BF16)
        ys_ref[...] = _pack_pair(_dot(act, wd_s[...]))

    @pl.when(j >= nu_ref[0])
    def _idle():
        ys_ref[...] = jnp.zeros_like(ys_ref)


def _experts(dm, l, blk_e, n_used, xs, w_gate, w_up, w_down):
    d, de, d2 = dm.D, dm.DE, dm.D // 2
    nb = dm.cap // MOE_BLOCK
    wmap = lambda j, be, nu: (l, be[j], 0, 0)
    return pl.pallas_call(
        _expert_kernel,
        grid_spec=pltpu.PrefetchScalarGridSpec(
            num_scalar_prefetch=2,
            grid=(nb,),
            in_specs=[pl.BlockSpec((MOE_BLOCK, d2), lambda j, be, nu: (j, 0)),
                      pl.BlockSpec((None, None, d, de), wmap),
                      pl.BlockSpec((None, None, d, de), wmap),
                      pl.BlockSpec((None, None, de, d), wmap)],
            out_specs=pl.BlockSpec((MOE_BLOCK, d2), lambda j, be, nu: (j, 0)),
            scratch_shapes=[pltpu.VMEM((d, de), BF16), pltpu.VMEM((d, de), BF16), pltpu.VMEM((de, d), BF16)]),
        out_shape=jax.ShapeDtypeStruct((dm.cap, d2), U32),
        compiler_params=_cparams(("arbitrary",), (3 * d * de * (2 * 4 + 2) + MOE_BLOCK * d * 24) / MIB + 8),
        name="moe_experts",
    )(blk_e, n_used, xs, w_gate, w_up, w_down)


def _combine_kernel(dest_ref, ys_ref, x_ref, gate_ref, mod_ref, o_ref, ybuf, sem, *, dm):
    i = pl.program_id(0)
    is_s = i == dm.n_tiles - 1
    bp = jnp.minimum(i // (dm.S // ROW_TILE), dm.Bp - 1)
    d2 = dm.D // 2

    def row_copy(r, k):
        return pltpu.make_async_copy(ys_ref.at[pl.ds(dest_ref[TOP_K * r + k], 1)], ybuf.at[k, pl.ds(r, 1)], sem)

    def issue(r, carry):
        for k in range(TOP_K):
            row_copy(r, k).start()
        return carry

    def drain(r, carry):
        for k in range(TOP_K):
            row_copy(r, k).wait()
        return carry

    lax.fori_loop(0, ROW_TILE, issue, 0)
    lax.fori_loop(0, ROW_TILE, drain, 0)

    def chunk(j, carry):
        rows = pl.ds(pl.multiple_of(j * ROW_CHUNK, ROW_CHUNK), ROW_CHUNK)
        idx = jnp.where(is_s, dm.Bp + j, bp)
        gt = mod_ref[GT2, pl.ds(idx, 1), :]
        gates = gate_ref[rows, :]
        g0, g1 = gates[:, 0:1], gates[:, 1:2]
        lo0, hi0 = _unpack_pair(ybuf[0, rows, :])
        lo1, hi1 = _unpack_pair(ybuf[1, rows, :])
        o_ref[rows, :d2] = x_ref[rows, :d2] + gt[:, :d2] * (g0 * lo0 + g1 * lo1)
        o_ref[rows, d2:] = x_ref[rows, d2:] + gt[:, d2:] * (g0 * hi0 + g1 * hi1)
        return carry

    lax.fori_loop(0, ROW_TILE // ROW_CHUNK, chunk, 0)


def _combine(dm, l, dest, ys, x1, gate, mod):
    d = dm.D
    tile = lambda i: (i, 0)
    return pl.pallas_call(
        functools.partial(_combine_kernel, dm=dm),
        grid=(dm.n_tiles,),
        in_specs=[pl.BlockSpec((ROW_TILE * TOP_K,), lambda i: (i,), memory_space=pltpu.SMEM),
                  pl.BlockSpec(memory_space=pl.ANY),
                  pl.BlockSpec((ROW_TILE, d), tile),
                  pl.BlockSpec((ROW_TILE, LANES), tile),
                  pl.BlockSpec((None, 6, dm.n_mod, d), lambda i: (l, 0, 0, 0))],
        out_specs=pl.BlockSpec((ROW_TILE, d), tile),
        out_shape=jax.ShapeDtypeStruct((dm.Tc, d), F32),
        scratch_shapes=[pltpu.VMEM((TOP_K, ROW_TILE, d // 2), U32), pltpu.SemaphoreType.DMA],
        compiler_params=_cparams(("arbitrary",), ROW_TILE * d * 24 / MIB + 12),
        name="moe_combine",
    )(dest, ys, x1, gate, mod)


def _final_norm_kernel(x_ref, g_ref, yp_ref, ys_ref, *, dm):
    is_s = pl.program_id(0) == dm.n_tiles - 1
    x = x_ref[...]
    y = x * lax.rsqrt(jnp.mean(x * x, axis=-1, keepdims=True) + EPS) * g_ref[...]

    @pl.when(jnp.logical_not(is_s))
    def _p():
        yp_ref[...] = y

    @pl.when(is_s)
    def _s():
        ys_ref[...] = y


def _final_norm(dm, x, g):
    d = dm.D
    np_tiles = dm.Tp // ROW_TILE
    return pl.pallas_call(
        functools.partial(_final_norm_kernel, dm=dm),
        grid=(dm.n_tiles,),
        in_specs=[pl.BlockSpec((ROW_TILE, d), lambda i: (i, 0)),
                  pl.BlockSpec((1, d), lambda i: (0, 0))],
        out_specs=[pl.BlockSpec((ROW_TILE, d), lambda i: (jnp.minimum(i, np_tiles - 1), 0)),
                   pl.BlockSpec((ROW_TILE, d), lambda i: (0, 0))],
        out_shape=[jax.ShapeDtypeStruct((dm.Tp, d), F32), jax.ShapeDtypeStruct((ROW_TILE, d), F32)],
        compiler_params=_cparams(("arbitrary",), ROW_TILE * d * 40 / MIB + 8),
        name="final_norm",
    )(x, g)


def _pad_lanes(x, n=LANES):
    return jnp.pad(x, [(0, 0)] * (x.ndim - 1) + [(0, n - x.shape[-1])])


def _moe_plan(dm, eid, rank, counts):
    counts = counts[0, :dm.NE]
    padded = (counts + MOE_BLOCK - 1) // MOE_BLOCK * MOE_BLOCK
    pad_end = jnp.cumsum(padded)
    pad_start = pad_end - padded
    dest = (pad_start[eid[:, :TOP_K]] + rank[:, :TOP_K]).reshape(-1).astype(I32)
    nb = dm.cap // MOE_BLOCK
    n_used = (pad_end[-1] // MOE_BLOCK).astype(I32)
    starts = jnp.arange(nb, dtype=I32) * MOE_BLOCK
    blk_e = jnp.minimum(jnp.searchsorted(pad_end, starts, side="right"), dm.NE - 1).astype(I32)
    last = blk_e[jnp.maximum(n_used - 1, 0)]
    blk_e = jnp.where(jnp.arange(nb) < n_used, blk_e, last)
    return dest, blk_e, n_used.reshape(1)


def kernel(x_prompt, x_sample, c_prompt, c_sample, state_mlstm_C, state_mlstm_n, state_mlstm_m, w_ada, b_ada, norm1_g, norm2_g, w_in, b_igate, b_fgate, mh_norm_g, sgu_ln_g, sgu_ln_b, w_spatial, b_spatial, w_branch_a, w_branch_b, w_out, w_router_group, b_router_group, w_router_expert, b_router_expert, w_expert_gate, w_expert_up, w_expert_down, final_norm_g):
    bp, s, d = x_prompt.shape
    bs, ds_, _ = x_sample.shape
    depth = w_in.shape[0]
    h_n = b_igate.shape[1]
    dv, dk = state_mlstm_C.shape[-2:]
    qk, v = h_n * dk, h_n * dv
    gh = (w_in.shape[2] - (2 * qk + 2 * v + 2 * h_n + 2 * d)) // 2
    dm = Dims(Bp=bp, S=s, Bs=bs, DS=ds_, D=d, DEPTH=depth, H=h_n, DK=dk, DV=dv, GH=gh,
              G=w_spatial.shape[1], GC=w_spatial.shape[2], EG=w_router_group.shape[-1],
              NE=w_router_expert.shape[-1], DE=w_expert_gate.shape[-1])
    assert dm.DS == ROW_CHUNK and dm.Ts <= ROW_TILE and s % ROW_TILE == 0 and s % MLSTM_CHUNK == 0
    assert dm.EG + dm.NE <= LANES and 2 * h_n <= LANES and dm.Bp + dm.Bs <= 32
    assert dm.off_v % v == 0 and dm.off_o % v == 0 and dm.off_z % (2 * gh) == 0 and d % 256 == 0

    c_all = jnp.concatenate([c_prompt, c_sample, jnp.zeros((32 - bp - bs, d), F32)], axis=0)
    mod = _adaln(c_all, w_ada, b_ada)[:, :bp + bs].reshape(depth, bp + bs, 6, d)
    mod = jnp.pad(mod, ((0, 0), (0, dm.n_mod - bp - bs), (0, 0), (0, 0))).transpose(0, 2, 1, 3)

    xs_rows = jnp.pad(x_sample.reshape(dm.Ts, d), ((0, ROW_TILE - dm.Ts), (0, 0)))
    x = (x_prompt.reshape(dm.Tp, d), xs_rows)
    off_if = 2 * qk + 2 * v
    zeros_c = jnp.zeros((bp, h_n, dv, dk), F32)
    zeros_n = jnp.zeros((bp, h_n, dk), F32)
    zeros_m = jnp.zeros((bp, 1, LANES), F32)
    out_c_p, out_n_p, out_m_p, out_c_s, out_n_s, out_m_s, out_v = [], [], [], [], [], [], []

    for l in range(depth):
        w_l = w_in[l]
        w_main = jnp.concatenate([w_l[:, :off_if], w_l[:, off_if + 2 * h_n:]], axis=1).astype(BF16)
        w_if = _pad_lanes(w_l[:, off_if:off_if + 2 * h_n])
        res = _norm_proj(dm, l, x, norm1_g.reshape(depth, 1, d), mod, w_if, SC1, SH1)
        if l == 0:
            hb, gates, x = res
        else:
            hb, gates = res
        proj = _matmul(hb, w_main, _pick_tile(dm.Tc, 1536), _pick_tile(dm.n_main, 1024), BF16)

        bias = _pad_lanes(jnp.concatenate([b_igate[l], b_fgate[l]])[None, :])
        mhg = mh_norm_g[l][None, :]
        a_p, c_p, n_p, m_p = _mlstm(dm, proj, gates, bias, mhg, zeros_c, zeros_n, zeros_m, sample=False)
        m_in = _pad_lanes(state_mlstm_m[l])[:, None, :]
        a_s, c_s, n_s, m_s = _mlstm(dm, proj, gates, bias, mhg, state_mlstm_C[l], state_mlstm_n[l], m_in,
                                    sample=True)
        lng, lnb = sgu_ln_g[l][None, :], sgu_ln_b[l][None, :]
        bst = _pad_lanes(b_spatial[l].T)
        (g_p,) = _sgu(dm, proj, lng, lnb, w_spatial[l], bst, sample=False)
        g_s, v_rows = _sgu(dm, proj, lng, lnb, w_spatial[l], bst, sample=True)

        merged = _merge(dm, a_p, a_s, g_p, g_s, proj, w_branch_a[l].astype(BF16), w_branch_b[l].astype(BF16))
        w_router = _pad_lanes(jnp.concatenate([w_router_group[l], w_router_expert[l]], axis=1))
        b_router = _pad_lanes(jnp.concatenate([b_router_group[l], b_router_expert[l]])[None, :])
        x1, hp, eid, gate = _outproj(dm, l, merged, x, w_out[l].astype(BF16), norm2_g.reshape(depth, 1, d),
                                     mod, w_router, b_router)

        rank, counts = _rank(dm, eid)
        dest, blk_e, n_used = _moe_plan(dm, eid, rank, counts)
        xs = _dispatch(dm, dest, hp, jnp.zeros((dm.cap, d // 2), U32))
        ys = _experts(dm, l, blk_e, n_used, xs, w_expert_gate, w_expert_up, w_expert_down)
        x = _combine(dm, l, dest, ys, x1, gate, mod)

        out_c_p.append(c_p)
        out_n_p.append(n_p)
        out_m_p.append(m_p[:, 0, :h_n])
        out_c_s.append(c_s)
        out_n_s.append(n_s)
        out_m_s.append(m_s[:, 0, :h_n])
        out_v.append(v_rows)

    y_p, y_s = _final_norm(dm, x, final_norm_g[None, :])
    return (y_p.reshape(bp, s, d), y_s[:dm.Ts].reshape(bs, ds_, d),
            jnp.stack(out_c_p), jnp.stack(out_n_p), jnp.stack(out_m_p),
            jnp.stack(out_c_s), jnp.stack(out_n_s), jnp.stack(out_m_s), jnp.stack(out_v))
```

```python
import functools
from typing import NamedTuple

import jax
import jax.numpy as jnp
from jax import lax
from jax.experimental import pallas as pl
from jax.experimental.pallas import tpu as pltpu

F32, BF16, U32, I32 = jnp.float32, jnp.bfloat16, jnp.uint32, jnp.int32
EPS = 1e-6
NEG = -1e30
LANES = 128
ROW_TILE = 512
ROW_CHUNK = 16
MLSTM_CHUNK = 128
MOE_BLOCK = 512
TOP_K = 2
DMA_UNROLL = 8
CHUNK_UNROLL = 2
MIB = 1024 * 1024
SH1, SC1, GT1, SH2, SC2, GT2 = range(6)


class Dims(NamedTuple):
    Bp: int
    S: int
    Bs: int
    DS: int
    D: int
    DEPTH: int
    H: int
    DK: int
    DV: int
    GH: int
    G: int
    GC: int
    EG: int
    NE: int
    DE: int

    @property
    def QK(self):
        return self.H * self.DK

    @property
    def V(self):
        return self.H * self.DV

    @property
    def Tp(self):
        return self.Bp * self.S

    @property
    def Ts(self):
        return self.Bs * self.DS

    @property
    def Tc(self):
        return self.Tp + ROW_TILE

    @property
    def n_tiles(self):
        return self.Tc // ROW_TILE

    @property
    def n_mod(self):
        return self.Bp + ROW_TILE // ROW_CHUNK

    @property
    def off_v(self):
        return 2 * self.QK

    @property
    def off_o(self):
        return 2 * self.QK + self.V

    @property
    def off_z(self):
        return 2 * self.QK + 2 * self.V

    @property
    def off_ga(self):
        return self.off_z + 2 * self.GH

    @property
    def off_gb(self):
        return self.off_ga + self.D

    @property
    def n_main(self):
        return self.off_gb + self.D

    @property
    def cap(self):
        tk = self.Tc * TOP_K
        return -(-(tk + self.NE * (MOE_BLOCK - 1)) // MOE_BLOCK) * MOE_BLOCK


def _cparams(semantics, vmem_mib):
    return pltpu.CompilerParams(dimension_semantics=semantics, vmem_limit_bytes=int(vmem_mib * MIB))


def _pick_tile(n, cap, unit=256):
    best = None
    for t in range(unit, min(n, cap) + 1, unit):
        if n % t == 0:
            best = t
    assert best is not None, (n, cap)
    return best


def _dot(a, b):
    return jnp.dot(a, b, preferred_element_type=F32)


def _split_bf16(x):
    hi = x.astype(BF16)
    lo = (x - hi.astype(F32)).astype(BF16)
    return hi, lo


def _dot3(a_hi, a_lo, w_hi, w_lo):
    return _dot(a_hi, w_hi) + _dot(a_hi, w_lo) + _dot(a_lo, w_hi)


def _rms_mod(x, g, sc, sh):
    y = x * lax.rsqrt(jnp.mean(x * x, axis=-1, keepdims=True) + EPS) * g
    return y * (1.0 + sc) + sh


def _pack_pair(x):
    d2 = x.shape[-1] // 2
    bits = lax.bitcast_convert_type(x.astype(BF16).astype(F32), U32)
    return (bits[:, :d2] >> 16) | (bits[:, d2:] & jnp.uint32(0xFFFF0000))


def _unpack_pair(p):
    lo = lax.bitcast_convert_type(p << 16, F32)
    hi = lax.bitcast_convert_type(p & jnp.uint32(0xFFFF0000), F32)
    return lo, hi


def _adaln_kernel(c_ref, w_ref, b_ref, o_ref):
    c = c_ref[...]
    s = (c * jax.nn.sigmoid(c)).astype(BF16)
    o_ref[...] = _dot(s, w_ref[...].astype(BF16)) + b_ref[...]


def _adaln(c_all, w_ada, b_ada):
    depth, d, n6 = w_ada.shape
    r = c_all.shape[0]
    tn = _pick_tile(n6, 1024)
    return pl.pallas_call(
        _adaln_kernel,
        grid=(depth, n6 // tn),
        in_specs=[pl.BlockSpec((r, d), lambda l, j: (0, 0)),
                  pl.BlockSpec((None, d, tn), lambda l, j: (l, 0, j)),
                  pl.BlockSpec((None, 1, tn), lambda l, j: (l, 0, j))],
        out_specs=pl.BlockSpec((None, r, tn), lambda l, j: (l, 0, j)),
        out_shape=jax.ShapeDtypeStruct((depth, r, n6), F32),
        compiler_params=_cparams(("arbitrary", "arbitrary"), 2 * d * tn * 4 / MIB + 3 * d * tn * 2 / MIB + 8),
        name="adaln",
    )(c_all, w_ada, b_ada.reshape(depth, 1, n6))


def _norm_proj_kernel(*refs, first, dm, k_sc, k_sh):
    if first:
        xp_ref, xs_ref, g_ref, mod_ref, w_ref, h_ref, p_ref, xc_ref, lo_scr = refs
    else:
        x_ref, g_ref, mod_ref, w_ref, h_ref, p_ref, lo_scr = refs
    i = pl.program_id(0)
    is_s = i == dm.n_tiles - 1
    bp = jnp.minimum(i // (dm.S // ROW_TILE), dm.Bp - 1)
    g = g_ref[...]

    def chunk(j, carry):
        rows = pl.ds(pl.multiple_of(j * ROW_CHUNK, ROW_CHUNK), ROW_CHUNK)
        if first:
            x = jnp.where(is_s, xs_ref[rows, :], xp_ref[rows, :])
            xc_ref[rows, :] = x
        else:
            x = x_ref[rows, :]
        idx = jnp.where(is_s, dm.Bp + j, bp)
        h = _rms_mod(x, g, mod_ref[k_sc, pl.ds(idx, 1), :], mod_ref[k_sh, pl.ds(idx, 1), :])
        hi, lo = _split_bf16(h)
        h_ref[rows, :] = hi
        lo_scr[rows, :] = lo
        return carry

    lax.fori_loop(0, ROW_TILE // ROW_CHUNK, chunk, 0, unroll=CHUNK_UNROLL)
    w_hi, w_lo = _split_bf16(w_ref[...])
    p_ref[...] = _dot3(h_ref[...], lo_scr[...], w_hi, w_lo)


def _norm_proj(dm, l, x_in, g, mod, w_small, k_sc, k_sh):
    d, tc, nt = dm.D, dm.Tc, dm.n_tiles
    first = isinstance(x_in, tuple)
    np_tiles = dm.Tp // ROW_TILE
    tile = lambda i: (i, 0)
    if first:
        x_args = list(x_in)
        x_specs = [pl.BlockSpec((ROW_TILE, d), lambda i: (jnp.minimum(i, np_tiles - 1), 0)),
                   pl.BlockSpec((ROW_TILE, d), lambda i: (0, 0))]
    else:
        x_args = [x_in]
        x_specs = [pl.BlockSpec((ROW_TILE, d), tile)]
    out_shape = [jax.ShapeDtypeStruct((tc, d), BF16), jax.ShapeDtypeStruct((tc, LANES), F32)]
    out_specs = [pl.BlockSpec((ROW_TILE, d), tile), pl.BlockSpec((ROW_TILE, LANES), tile)]
    if first:
        out_shape.append(jax.ShapeDtypeStruct((tc, d), F32))
        out_specs.append(pl.BlockSpec((ROW_TILE, d), tile))
    return pl.pallas_call(
        functools.partial(_norm_proj_kernel, first=first, dm=dm, k_sc=k_sc, k_sh=k_sh),
        grid=(nt,),
        in_specs=x_specs + [pl.BlockSpec((None, 1, d), lambda i: (l, 0, 0)),
                            pl.BlockSpec((None, 6, dm.n_mod, d), lambda i: (l, 0, 0, 0)),
                            pl.BlockSpec((d, LANES), lambda i: (0, 0))],
        out_specs=out_specs,
        out_shape=out_shape,
        scratch_shapes=[pltpu.VMEM((ROW_TILE, d), BF16)],
        compiler_params=_cparams(("arbitrary",), ROW_TILE * d * 40 / MIB + 16),
        name="norm_proj",
    )(*x_args, g, mod, w_small)


def _in_proj_kernel(a_ref, w_ref, wn_ref, o_ref, w_scr, *, n_aligned, shift):
    j, i = pl.program_id(0), pl.program_id(1)

    @pl.when((i == 0) & (j < n_aligned))
    def _cast():
        w_scr[...] = w_ref[...].astype(BF16)

    @pl.when((i == 0) & (j >= n_aligned))
    def _cast_shifted():
        groups = w_ref.shape[1] // LANES
        keep = lax.broadcasted_iota(I32, (w_ref.shape[0], LANES), 1) < LANES - shift
        for g in range(groups):
            cur = w_ref[:, g * LANES:(g + 1) * LANES]
            nxt = w_ref[:, (g + 1) * LANES:(g + 2) * LANES] if g + 1 < groups else wn_ref[...]
            w_scr[:, g * LANES:(g + 1) * LANES] = jnp.where(
                keep, pltpu.roll(cur, LANES - shift, axis=1), pltpu.roll(nxt, LANES - shift, axis=1)).astype(BF16)

    o_ref[...] = _dot(a_ref[...], w_scr[...]).astype(o_ref.dtype)


def _in_proj(dm, l, a, w_in):
    m, k = a.shape
    off_if, n = dm.off_z, dm.n_main
    shift = 2 * dm.H
    tn = max(t for t in range(LANES, 1024 + 1, LANES) if off_if % t == 0 and n % t == 0)
    tm = _pick_tile(m, 1536)
    vmem = (2 * tm * k * 2 + 2 * k * (tn + LANES) * 4 + k * tn * 2 + 2 * tm * tn * 2 + 2 * tm * tn * 4) / MIB + 6
    return pl.pallas_call(
        functools.partial(_in_proj_kernel, n_aligned=off_if // tn, shift=shift),
        grid=(n // tn, m // tm),
        in_specs=[pl.BlockSpec((tm, k), lambda j, i: (i, 0)),
                  pl.BlockSpec((None, k, tn), lambda j, i: (l, 0, j)),
                  pl.BlockSpec((None, k, LANES), lambda j, i: (l, 0, (j + 1) * (tn // LANES)))],
        out_specs=pl.BlockSpec((tm, tn), lambda j, i: (i, j)),
        out_shape=jax.ShapeDtypeStruct((m, n), BF16),
        scratch_shapes=[pltpu.VMEM((k, tn), BF16)],
        compiler_params=_cparams(("arbitrary", "arbitrary"), vmem),
        name="in_proj",
    )(a, w_in, w_in)


def _pad_rows(x, rows):
    if x.shape[0] == rows:
        return x
    return jnp.concatenate([x, jnp.zeros((rows - x.shape[0], x.shape[1]), x.dtype)], axis=0)


def _mlstm_kernel(*refs, dm, lv, n_seq):
    a_ref = refs[10]
    seq_ok = pl.program_id(0) < n_seq

    @pl.when(seq_ok)
    def _run():
        _mlstm_step(*refs, dm=dm, lv=lv)

    @pl.when(jnp.logical_not(seq_ok))
    def _fill():
        a_ref[...] = jnp.zeros_like(a_ref)


def _mlstm_step(q_ref, k_ref, v_ref, o_ref, gt_ref, bias_ref, mhg_ref, c0_ref, n0_ref, m0_ref,
                a_ref, cout_ref, nout_ref, mout_ref, ct_scr, n_scr, m_scr, *, dm, lv):
    h_n, dk, dv = dm.H, dm.DK, dm.DV
    L = MLSTM_CHUNK
    c = pl.program_id(1)

    @pl.when(c == 0)
    def _init():
        for h in range(h_n):
            ct_scr[h] = c0_ref[h].T
        n_scr[...] = n0_ref[...]
        m_scr[...] = m0_ref[...]

    row = lax.broadcasted_iota(I32, (L, L), 0)
    col = lax.broadcasted_iota(I32, (L, L), 1)
    causal = col <= row
    eye = col == row
    lane = lax.broadcasted_iota(I32, (L, LANES), 1)
    rowl = lax.broadcasted_iota(I32, (L, LANES), 0)

    xg = _pad_rows(gt_ref[...], L) + bias_ref[...]
    f_log = jnp.minimum(xg, 0.0) - jnp.log1p(jnp.exp(-jnp.abs(xg)))
    gl = jnp.where(lane < h_n, xg, f_log)
    gl = jnp.where(rowl < lv, gl, 0.0)
    tri = causal.astype(BF16)
    g_hi = gl.astype(BF16)
    r1 = gl - g_hi.astype(F32)
    g_mid = r1.astype(BF16)
    g_lo = (r1 - g_mid.astype(F32)).astype(BF16)
    cum = _dot(tri, g_hi) + _dot(tri, g_mid) + _dot(tri, g_lo)
    b_all = pltpu.roll(cum, LANES - h_n, axis=1)
    a_all = jnp.where(rowl < lv, gl - b_all, NEG)

    scale = dk ** -0.5
    kf = _pad_rows(k_ref[...], L).astype(F32) * scale
    kt_all = kf.T.astype(BF16)
    q_all = _pad_rows(q_ref[...], L)
    v_all = _pad_rows(v_ref[...], L)
    m_old = m_scr[...]
    m_new = m_old
    lane1 = lax.broadcasted_iota(I32, (1, LANES), 1)

    for h in range(h_n):
        a_col = a_all[:, h:h + 1]
        b_col = b_all[:, h:h + 1]
        a_row = jnp.sum(jnp.where(eye, a_col, 0.0), axis=0, keepdims=True)
        m_prev = m_old[:, h:h + 1]
        big_m = jnp.maximum(jnp.max(jnp.where(causal, a_row, NEG), axis=1, keepdims=True), m_prev)
        d_w = jnp.where(causal, jnp.exp(jnp.minimum(a_row - big_m, 0.0)), 0.0)
        inter = jnp.exp(m_prev - big_m)
        qh = q_all[:, h * dk:(h + 1) * dk]
        vh = v_all[:, h * dv:(h + 1) * dv]
        kt = kt_all[h * dk:(h + 1) * dk, :]
        sd = _dot(qh, kt) * d_w
        ct = ct_scr[h]
        nvec = n_scr[h:h + 1, :]
        num = _dot(sd.astype(BF16), vh) + inter * _dot(qh, ct.astype(BF16))
        den = (jnp.sum(sd, axis=1, keepdims=True)
               + inter * jnp.sum(qh.astype(F32) * nvec, axis=1, keepdims=True))
        hh = num / jnp.maximum(jnp.abs(den), jnp.exp(-(b_col + big_m)))
        hn = hh * lax.rsqrt(jnp.mean(hh * hh, axis=1, keepdims=True) + EPS)
        gate_o = jax.nn.sigmoid(o_ref[:, h * dv:(h + 1) * dv].astype(F32))
        a_ref[:, h * dv:(h + 1) * dv] = (hn[:lv] * mhg_ref[:, h * dv:(h + 1) * dv] * gate_o).astype(BF16)
        m_last = big_m[L - 1:L, :]
        decay = jnp.exp(m_prev - m_last)
        w_col = jnp.exp(a_col - m_last)
        vw = (vh.astype(F32) * w_col).astype(BF16)
        ct_scr[h] = decay * ct + _dot(kt, vw)
        n_scr[h:h + 1, :] = decay * nvec + jnp.sum(kf[:, h * dk:(h + 1) * dk] * w_col, axis=0, keepdims=True)
        m_new = jnp.where(lane1 == h, b_col[L - 1:L, :] + m_last, m_new)
    m_scr[...] = m_new

    @pl.when(c == pl.num_programs(1) - 1)
    def _fin():
        for h in range(h_n):
            cout_ref[h] = ct_scr[h].T
        nout_ref[...] = n_scr[...]
        mout_ref[...] = m_scr[...]


def _mlstm(dm, proj, gates, bias, mhg, c0, n0, m0, ls, *, sample):
    if sample:
        b_n, n_grid, nc, lv, row0, n_rows = dm.Bs, ROW_TILE // dm.DS, 1, dm.DS, dm.Tp // dm.DS, ROW_TILE
    else:
        b_n, n_grid, nc, lv, row0, n_rows = dm.Bp, dm.Bp, dm.S // MLSTM_CHUNK, MLSTM_CHUNK, 0, dm.Tp
    h_n, dk, dv, qk, v = dm.H, dm.DK, dm.DV, dm.QK, dm.V
    rb = lambda b, c: row0 + b * nc + c
    sq = lambda b: jnp.minimum(b, b_n - 1)
    in_specs = [
        pl.BlockSpec((lv, qk), lambda b, c: (rb(b, c), 0)),
        pl.BlockSpec((lv, qk), lambda b, c: (rb(b, c), 1)),
        pl.BlockSpec((lv, v), lambda b, c: (rb(b, c), dm.off_v // v)),
        pl.BlockSpec((lv, v), lambda b, c: (rb(b, c), dm.off_o // v)),
        pl.BlockSpec((lv, LANES), lambda b, c: (rb(b, c), 0)),
        pl.BlockSpec((1, LANES), lambda b, c: (0, 0)),
        pl.BlockSpec((1, v), lambda b, c: (0, 0)),
        pl.BlockSpec((None, None, h_n, dv, dk), lambda b, c: (ls, sq(b), 0, 0, 0)),
        pl.BlockSpec((None, None, h_n, dk), lambda b, c: (ls, sq(b), 0, 0)),
        pl.BlockSpec((None, None, 1, LANES), lambda b, c: (ls, sq(b), 0, 0)),
    ]
    return pl.pallas_call(
        functools.partial(_mlstm_kernel, dm=dm, lv=lv, n_seq=b_n),
        grid=(n_grid, nc),
        in_specs=in_specs,
        out_specs=[pl.BlockSpec((lv, v), lambda b, c: (b * nc + c, 0)),
                   pl.BlockSpec((None, h_n, dv, dk), lambda b, c: (sq(b), 0, 0, 0)),
                   pl.BlockSpec((None, h_n, dk), lambda b, c: (sq(b), 0, 0)),
                   pl.BlockSpec((None, 1, LANES), lambda b, c: (sq(b), 0, 0))],
        out_shape=[jax.ShapeDtypeStruct((n_rows, v), BF16),
                   jax.ShapeDtypeStruct((b_n, h_n, dv, dk), F32),
                   jax.ShapeDtypeStruct((b_n, h_n, dk), F32),
                   jax.ShapeDtypeStruct((b_n, 1, LANES), F32)],
        scratch_shapes=[pltpu.VMEM((h_n, dk, dv), F32), pltpu.VMEM((h_n, dk), F32), pltpu.VMEM((1, LANES), F32)],
        compiler_params=_cparams(("arbitrary", "arbitrary"), 40),
        name="mlstm_sample" if sample else "mlstm_prompt",
    )(proj, proj, proj, proj, gates, bias, mhg, c0, n0, m0)


def _sgu_kernel(*refs, dm, lv, n_seq, emit_v):
    g_ref = refs[5]
    seq_ok = pl.program_id(0) < n_seq

    @pl.when(seq_ok)
    def _run():
        _sgu_step(*refs, dm=dm, lv=lv, emit_v=emit_v)

    @pl.when(jnp.logical_not(seq_ok))
    def _fill():
        g_ref[...] = jnp.zeros_like(g_ref)


def _sgu_step(*refs, dm, lv, emit_v):
    z_ref, lng_ref, lnb_ref, ws_ref, bst_ref, g_ref = refs[:6]
    gh, gch = dm.GH, dm.GH // dm.G
    z = jax.nn.gelu(z_ref[...].astype(F32))
    u, v = z[:, :gh], z[:, gh:]
    xc = v - jnp.mean(v, axis=-1, keepdims=True)
    vn = xc * lax.rsqrt(jnp.mean(xc * xc, axis=-1, keepdims=True) + EPS) * lng_ref[...] + lnb_ref[...]
    if emit_v:
        refs[6][...] = vn
    row = lax.broadcasted_iota(I32, (lv, lv), 0)
    col = lax.broadcasted_iota(I32, (lv, lv), 1)
    vb = vn.astype(BF16)
    for g in range(dm.G):
        w = jnp.where(col <= row, ws_ref[g, :lv, :lv], 0.0).astype(BF16)
        mixed = _dot(w, vb[:, g * gch:(g + 1) * gch]) + bst_ref[:lv, g:g + 1]
        g_ref[:, g * gch:(g + 1) * gch] = (u[:, g * gch:(g + 1) * gch] * mixed).astype(BF16)


def _sgu(dm, proj, lng, lnb, ws, bst, *, sample):
    if sample:
        b_n, n_grid, nc, lv, row0, n_rows = dm.Bs, ROW_TILE // dm.DS, 1, dm.DS, dm.Tp // dm.DS, ROW_TILE
    else:
        b_n, n_grid, nc, lv, row0, n_rows = dm.Bp, dm.Bp, dm.S // dm.GC, dm.GC, 0, dm.Tp
    gh = dm.GH
    rb = lambda b, c: row0 + b * nc + c
    in_specs = [
        pl.BlockSpec((lv, 2 * gh), lambda b, c: (rb(b, c), dm.off_z // (2 * gh))),
        pl.BlockSpec((1, gh), lambda b, c: (0, 0)),
        pl.BlockSpec((1, gh), lambda b, c: (0, 0)),
        pl.BlockSpec((dm.G, dm.GC, dm.GC), lambda b, c: (0, 0, 0)),
        pl.BlockSpec((dm.GC, LANES), lambda b, c: (0, 0)),
    ]
    out_specs = [pl.BlockSpec((lv, gh), lambda b, c: (b * nc + c, 0))]
    out_shape = [jax.ShapeDtypeStruct((n_rows, gh), BF16)]
    if sample:
        out_specs.append(pl.BlockSpec((None, lv, gh), lambda b, c: (jnp.minimum(b, b_n - 1), 0, 0)))
        out_shape.append(jax.ShapeDtypeStruct((b_n, lv, gh), F32))
    return pl.pallas_call(
        functools.partial(_sgu_kernel, dm=dm, lv=lv, n_seq=b_n, emit_v=sample),
        grid=(n_grid, nc),
        in_specs=in_specs,
        out_specs=out_specs,
        out_shape=out_shape,
        compiler_params=_cparams(("arbitrary", "arbitrary"), 24),
        name="sgu_sample" if sample else "sgu_prompt",
    )(proj, lng, lnb, ws, bst)


def _merge_kernel(ap_ref, as_ref, gp_ref, gs_ref, ga_ref, gb_ref, wa_ref, wb_ref, o_ref):
    is_s = pl.program_id(1) == pl.num_programs(1) - 1
    pa = _dot(jnp.where(is_s, as_ref[...], ap_ref[...]), wa_ref[...])
    pb = _dot(jnp.where(is_s, gs_ref[...], gp_ref[...]), wb_ref[...])
    o_ref[...] = (jax.nn.sigmoid(ga_ref[...].astype(F32)) * pa
                  + jax.nn.sigmoid(gb_ref[...].astype(F32)) * pb).astype(o_ref.dtype)


def _merge(dm, a_p, a_s, g_p, g_s, proj, wa, wb):
    tc, d, v, gh = dm.Tc, dm.D, dm.V, dm.GH
    tm = ROW_TILE
    tn = _pick_tile(d, 1024)
    np_tiles = dm.Tp // tm
    prompt = lambda j, i: (jnp.minimum(i, np_tiles - 1), 0)
    const = lambda j, i: (0, 0)
    blocks = 2 * tm * (v + gh) * 2 + 3 * tm * tn * 2 + (v + gh) * tn * 2
    return pl.pallas_call(
        _merge_kernel,
        grid=(d // tn, tc // tm),
        in_specs=[pl.BlockSpec((tm, v), prompt),
                  pl.BlockSpec((tm, v), const),
                  pl.BlockSpec((tm, gh), prompt),
                  pl.BlockSpec((tm, gh), const),
                  pl.BlockSpec((tm, tn), lambda j, i: (i, dm.off_ga // tn + j)),
                  pl.BlockSpec((tm, tn), lambda j, i: (i, dm.off_gb // tn + j)),
                  pl.BlockSpec((v, tn), lambda j, i: (0, j)),
                  pl.BlockSpec((gh, tn), lambda j, i: (0, j))],
        out_specs=pl.BlockSpec((tm, tn), lambda j, i: (i, j)),
        out_shape=jax.ShapeDtypeStruct((tc, d), BF16),
        compiler_params=_cparams(("arbitrary", "arbitrary"), (2 * blocks + 6 * tm * tn * 4) / MIB + 8),
        name="merge",
    )(a_p, a_s, g_p, g_s, proj, proj, wa, wb)


def _route(logits, eg, ne):
    epg = ne // eg
    lane = lax.broadcasted_iota(I32, logits.shape, 1)
    gmask = lane < eg
    gmax = jnp.max(jnp.where(gmask, logits, NEG), axis=1, keepdims=True)
    gexp = jnp.where(gmask, jnp.exp(jnp.minimum(logits - gmax, 0.0)), 0.0)
    pg = gexp / jnp.sum(gexp, axis=1, keepdims=True)
    p_grp = jnp.max(pg, axis=1, keepdims=True)
    g_sel = jnp.min(jnp.where(gmask & (pg == p_grp), lane, LANES), axis=1, keepdims=True)
    lo = eg + g_sel * epg
    emask = (lane >= lo) & (lane < lo + epg)
    emax = jnp.max(jnp.where(emask, logits, NEG), axis=1, keepdims=True)
    eexp = jnp.where(emask, jnp.exp(jnp.minimum(logits - emax, 0.0)), 0.0)
    pe = eexp / jnp.sum(eexp, axis=1, keepdims=True)
    p1 = jnp.max(jnp.where(emask, pe, -1.0), axis=1, keepdims=True)
    i1 = jnp.min(jnp.where(emask & (pe == p1), lane, LANES), axis=1, keepdims=True)
    rest = emask & (lane != i1)
    p2 = jnp.max(jnp.where(rest, pe, -1.0), axis=1, keepdims=True)
    i2 = jnp.min(jnp.where(rest & (pe == p2), lane, LANES), axis=1, keepdims=True)
    psum = p1 + p2
    eid = jnp.where(lane == 0, i1 - eg, jnp.where(lane == 1, i2 - eg, 0))
    gate = jnp.where(lane == 0, p_grp * (p1 / psum), jnp.where(lane == 1, p_grp * (p2 / psum), 0.0))
    return eid, gate


def _outproj_kernel(m_ref, x_ref, w_ref, g_ref, mod_ref, wr_ref, rb_ref,
                    x1_ref, hp_ref, eid_ref, gate_ref, acc_scr, hi_scr, lo_scr, *, dm):
    i = pl.program_id(0)
    is_s = i == dm.n_tiles - 1
    bp = jnp.minimum(i // (dm.S // ROW_TILE), dm.Bp - 1)
    acc_scr[...] = _dot(m_ref[...], w_ref[...])
    g = g_ref[...]

    def chunk(j, carry):
        rows = pl.ds(pl.multiple_of(j * ROW_CHUNK, ROW_CHUNK), ROW_CHUNK)
        idx = jnp.where(is_s, dm.Bp + j, bp)
        x1 = x_ref[rows, :] + mod_ref[GT1, pl.ds(idx, 1), :] * acc_scr[rows, :]
        x1_ref[rows, :] = x1
        h = _rms_mod(x1, g, mod_ref[SC2, pl.ds(idx, 1), :], mod_ref[SH2, pl.ds(idx, 1), :])
        hi, lo = _split_bf16(h)
        hi_scr[rows, :] = hi
        lo_scr[rows, :] = lo
        hp_ref[rows, :] = _pack_pair(h)
        return carry

    lax.fori_loop(0, ROW_TILE // ROW_CHUNK, chunk, 0, unroll=CHUNK_UNROLL)
    w_hi, w_lo = _split_bf16(wr_ref[...])
    logits = _dot3(hi_scr[...], lo_scr[...], w_hi, w_lo) + rb_ref[...]
    eid, gate = _route(logits, dm.EG, dm.NE)
    eid_ref[...] = eid
    gate_ref[...] = gate


def _outproj(dm, l, merged, x, w_out, g2, mod, w_router, b_router):
    d, tc = dm.D, dm.Tc
    tile = lambda i: (i, 0)
    const = lambda i: (0, 0)
    return pl.pallas_call(
        functools.partial(_outproj_kernel, dm=dm),
        grid=(dm.n_tiles,),
        in_specs=[pl.BlockSpec((ROW_TILE, d), tile),
                  pl.BlockSpec((ROW_TILE, d), tile),
                  pl.BlockSpec((d, d), const),
                  pl.BlockSpec((None, 1, d), lambda i: (l, 0, 0)),
                  pl.BlockSpec((None, 6, dm.n_mod, d), lambda i: (l, 0, 0, 0)),
                  pl.BlockSpec((d, LANES), const),
                  pl.BlockSpec((1, LANES), const)],
        out_specs=[pl.BlockSpec((ROW_TILE, d), tile),
                   pl.BlockSpec((ROW_TILE, d // 2), tile),
                   pl.BlockSpec((ROW_TILE, LANES), tile),
                   pl.BlockSpec((ROW_TILE, LANES), tile)],
        out_shape=[jax.ShapeDtypeStruct((tc, d), F32),
                   jax.ShapeDtypeStruct((tc, d // 2), U32),
                   jax.ShapeDtypeStruct((tc, LANES), I32),
                   jax.ShapeDtypeStruct((tc, LANES), F32)],
        scratch_shapes=[pltpu.VMEM((ROW_TILE, d), F32), pltpu.VMEM((ROW_TILE, d), BF16),
                        pltpu.VMEM((ROW_TILE, d), BF16)],
        compiler_params=_cparams(("arbitrary",), (4 * d * d + ROW_TILE * d * 40) / MIB + 12),
        name="out_proj",
    )(merged, x, w_out, g2, mod, w_router, b_router)


def _rank_kernel(eid_ref, rank_ref, cnt_ref, run_scr):
    i = pl.program_id(0)

    @pl.when(i == 0)
    def _init():
        run_scr[...] = jnp.zeros_like(run_scr)

    eid = eid_ref[...]
    lane = lax.broadcasted_iota(I32, eid.shape, 1)
    e0 = lane == eid[:, 0:1]
    e1 = lane == eid[:, 1:2]
    hot = (e0 | e1).astype(BF16)
    n = eid.shape[0]
    strict = (lax.broadcasted_iota(I32, (n, n), 1) < lax.broadcasted_iota(I32, (n, n), 0)).astype(BF16)
    before = _dot(strict, hot) + run_scr[...]
    r0 = jnp.sum(jnp.where(e0, before, 0.0), axis=1, keepdims=True)
    r1 = jnp.sum(jnp.where(e1, before, 0.0), axis=1, keepdims=True)
    rank_ref[...] = jnp.where(lane == 0, r0, jnp.where(lane == 1, r1, 0.0)).astype(I32)
    run_scr[...] = run_scr[...] + jnp.sum(hot.astype(F32), axis=0, keepdims=True)
    cnt_ref[...] = run_scr[...].astype(I32)


def _rank(dm, eid):
    tile = lambda i: (i, 0)
    return pl.pallas_call(
        _rank_kernel,
        grid=(dm.n_tiles,),
        in_specs=[pl.BlockSpec((ROW_TILE, LANES), tile)],
        out_specs=[pl.BlockSpec((ROW_TILE, LANES), tile), pl.BlockSpec((1, LANES), lambda i: (0, 0))],
        out_shape=[jax.ShapeDtypeStruct((dm.Tc, LANES), I32), jax.ShapeDtypeStruct((1, LANES), I32)],
        scratch_shapes=[pltpu.VMEM((1, LANES), F32)],
        compiler_params=_cparams(("arbitrary",), 16),
        name="moe_rank",
    )(eid)


def _dispatch_kernel(dest_ref, h_ref, xs_in_ref, xs_ref, sem):
    del xs_in_ref

    def issue(r, carry):
        for k in range(TOP_K):
            pltpu.make_async_copy(h_ref.at[pl.ds(r, 1)], xs_ref.at[pl.ds(dest_ref[TOP_K * r + k], 1)],
                                  sem).start(priority=k)
        return carry

    lax.fori_loop(0, ROW_TILE, issue, 0, unroll=DMA_UNROLL)
    for k in range(TOP_K):
        pltpu.make_async_copy(h_ref, xs_ref.at[pl.ds(0, ROW_TILE)], sem).wait()


def _dispatch(dm, dest, hp, xs_zero):
    d2 = dm.D // 2
    return pl.pallas_call(
        _dispatch_kernel,
        grid=(dm.n_tiles,),
        in_specs=[pl.BlockSpec((ROW_TILE * TOP_K,), lambda i: (i,), memory_space=pltpu.SMEM),
                  pl.BlockSpec((ROW_TILE, d2), lambda i: (i, 0)),
                  pl.BlockSpec(memory_space=pl.ANY)],
        out_specs=pl.BlockSpec(memory_space=pl.ANY),
        out_shape=jax.ShapeDtypeStruct((dm.cap, d2), U32),
        scratch_shapes=[pltpu.SemaphoreType.DMA],
        input_output_aliases={2: 0},
        compiler_params=_cparams(("arbitrary",), 16),
        name="moe_dispatch",
    )(dest, hp, xs_zero)


def _expert_kernel(be_ref, nu_ref, xs_ref, wg_ref, wu_ref, wd_ref, ys_ref, wg_s, wu_s, wd_s):
    j = pl.program_id(0)
    changed = (j == 0) | (be_ref[j] != be_ref[jnp.maximum(j - 1, 0)])

    @pl.when(changed)
    def _cast():
        wg_s[...] = wg_ref[...].astype(BF16)
        wu_s[...] = wu_ref[...].astype(BF16)
        wd_s[...] = wd_ref[...].astype(BF16)

    @pl.when(j < nu_ref[0])
    def _compute():
        d2 = xs_ref.shape[1]
        lo, hi = _unpack_pair(xs_ref[...])
        lo, hi = lo.astype(BF16), hi.astype(BF16)
        hg = _dot(lo, wg_s[:d2, :]) + _dot(hi, wg_s[d2:, :])
        hu = _dot(lo, wu_s[:d2, :]) + _dot(hi, wu_s[d2:, :])
        act = (hg * jax.nn.sigmoid(hg) * hu).astype(BF16)
        ys_ref[...] = _pack_pair(_dot(act, wd_s[...]))

    @pl.when(j >= nu_ref[0])
    def _idle():
        ys_ref[...] = jnp.zeros_like(ys_ref)


def _experts(dm, l, blk_e, n_used, xs, w_gate, w_up, w_down):
    d, de, d2 = dm.D, dm.DE, dm.D // 2
    nb = dm.cap // MOE_BLOCK
    wmap = lambda j, be, nu: (l, be[j], 0, 0)
    return pl.pallas_call(
        _expert_kernel,
        grid_spec=pltpu.PrefetchScalarGridSpec(
            num_scalar_prefetch=2,
            grid=(nb,),
            in_specs=[pl.BlockSpec((MOE_BLOCK, d2), lambda j, be, nu: (j, 0)),
                      pl.BlockSpec((None, None, d, de), wmap),
                      pl.BlockSpec((None, None, d, de), wmap),
                      pl.BlockSpec((None, None, de, d), wmap)],
            out_specs=pl.BlockSpec((MOE_BLOCK, d2), lambda j, be, nu: (j, 0)),
            scratch_shapes=[pltpu.VMEM((d, de), BF16), pltpu.VMEM((d, de), BF16), pltpu.VMEM((de, d), BF16)]),
        out_shape=jax.ShapeDtypeStruct((dm.cap, d2), U32),
        compiler_params=_cparams(("arbitrary",), (3 * d * de * (2 * 4 + 2) + MOE_BLOCK * d * 24) / MIB + 8),
        name="moe_experts",
    )(blk_e, n_used, xs, w_gate, w_up, w_down)


def _combine_kernel(dest_ref, ys_ref, x_ref, gate_ref, mod_ref, o_ref, ybuf, sem, *, dm):
    i = pl.program_id(0)
    is_s = i == dm.n_tiles - 1
    bp = jnp.minimum(i // (dm.S // ROW_TILE), dm.Bp - 1)
    d2 = dm.D // 2

    def issue(r, carry):
        for k in range(TOP_K):
            pltpu.make_async_copy(ys_ref.at[pl.ds(dest_ref[TOP_K * r + k], 1)], ybuf.at[k, pl.ds(r, 1)],
                                  sem).start(priority=k)
        return carry

    lax.fori_loop(0, ROW_TILE, issue, 0, unroll=DMA_UNROLL)
    for k in range(TOP_K):
        pltpu.make_async_copy(ys_ref.at[pl.ds(0, ROW_TILE)], ybuf.at[k], sem).wait()

    def chunk(j, carry):
        rows = pl.ds(pl.multiple_of(j * ROW_CHUNK, ROW_CHUNK), ROW_CHUNK)
        idx = jnp.where(is_s, dm.Bp + j, bp)
        gt = mod_ref[GT2, pl.ds(idx, 1), :]
        gates = gate_ref[rows, :]
        g0, g1 = gates[:, 0:1], gates[:, 1:2]
        lo0, hi0 = _unpack_pair(ybuf[0, rows, :])
        lo1, hi1 = _unpack_pair(ybuf[1, rows, :])
        o_ref[rows, :d2] = x_ref[rows, :d2] + gt[:, :d2] * (g0 * lo0 + g1 * lo1)
        o_ref[rows, d2:] = x_ref[rows, d2:] + gt[:, d2:] * (g0 * hi0 + g1 * hi1)
        return carry

    lax.fori_loop(0, ROW_TILE // ROW_CHUNK, chunk, 0, unroll=CHUNK_UNROLL)


def _combine(dm, l, dest, ys, x1, gate, mod):
    d = dm.D
    tile = lambda i: (i, 0)
    return pl.pallas_call(
        functools.partial(_combine_kernel, dm=dm),
        grid=(dm.n_tiles,),
        in_specs=[pl.BlockSpec((ROW_TILE * TOP_K,), lambda i: (i,), memory_space=pltpu.SMEM),
                  pl.BlockSpec(memory_space=pl.ANY),
                  pl.BlockSpec((ROW_TILE, d), tile),
                  pl.BlockSpec((ROW_TILE, LANES), tile),
                  pl.BlockSpec((None, 6, dm.n_mod, d), lambda i: (l, 0, 0, 0))],
        out_specs=pl.BlockSpec((ROW_TILE, d), tile),
        out_shape=jax.ShapeDtypeStruct((dm.Tc, d), F32),
        scratch_shapes=[pltpu.VMEM((TOP_K, ROW_TILE, d // 2), U32), pltpu.SemaphoreType.DMA],
        compiler_params=_cparams(("arbitrary",), ROW_TILE * d * 24 / MIB + 12),
        name="moe_combine",
    )(dest, ys, x1, gate, mod)


def _final_norm_kernel(x_ref, g_ref, yp_ref, ys_ref, *, dm):
    is_s = pl.program_id(0) == dm.n_tiles - 1
    x = x_ref[...]
    y = x * lax.rsqrt(jnp.mean(x * x, axis=-1, keepdims=True) + EPS) * g_ref[...]

    @pl.when(jnp.logical_not(is_s))
    def _p():
        yp_ref[...] = y

    @pl.when(is_s)
    def _s():
        ys_ref[...] = y


def _final_norm(dm, x, g):
    d = dm.D
    np_tiles = dm.Tp // ROW_TILE
    return pl.pallas_call(
        functools.partial(_final_norm_kernel, dm=dm),
        grid=(dm.n_tiles,),
        in_specs=[pl.BlockSpec((ROW_TILE, d), lambda i: (i, 0)),
                  pl.BlockSpec((1, d), lambda i: (0, 0))],
        out_specs=[pl.BlockSpec((ROW_TILE, d), lambda i: (jnp.minimum(i, np_tiles - 1), 0)),
                   pl.BlockSpec((ROW_TILE, d), lambda i: (0, 0))],
        out_shape=[jax.ShapeDtypeStruct((dm.Tp, d), F32), jax.ShapeDtypeStruct((ROW_TILE, d), F32)],
        compiler_params=_cparams(("arbitrary",), ROW_TILE * d * 40 / MIB + 8),
        name="final_norm",
    )(x, g)


def _pad_lanes(x, n=LANES):
    return jnp.pad(x, [(0, 0)] * (x.ndim - 1) + [(0, n - x.shape[-1])])


def _moe_plan(dm, eid, rank, counts):
    counts = counts[0, :dm.NE]
    padded = (counts + MOE_BLOCK - 1) // MOE_BLOCK * MOE_BLOCK
    pad_end = jnp.cumsum(padded)
    pad_start = pad_end - padded
    dest = (pad_start[eid[:, :TOP_K]] + rank[:, :TOP_K]).reshape(-1).astype(I32)
    nb = dm.cap // MOE_BLOCK
    n_used = (pad_end[-1] // MOE_BLOCK).astype(I32)
    starts = jnp.minimum(jnp.arange(nb, dtype=I32), n_used - 1) * MOE_BLOCK
    blk_e = jnp.minimum(jnp.sum(pad_end[None, :] <= starts[:, None], axis=1), dm.NE - 1).astype(I32)
    return dest, blk_e, n_used.reshape(1)


def kernel(x_prompt, x_sample, c_prompt, c_sample, state_mlstm_C, state_mlstm_n, state_mlstm_m, w_ada, b_ada, norm1_g, norm2_g, w_in, b_igate, b_fgate, mh_norm_g, sgu_ln_g, sgu_ln_b, w_spatial, b_spatial, w_branch_a, w_branch_b, w_out, w_router_group, b_router_group, w_router_expert, b_router_expert, w_expert_gate, w_expert_up, w_expert_down, final_norm_g):
    bp, s, d = x_prompt.shape
    bs, ds_, _ = x_sample.shape
    depth = w_in.shape[0]
    h_n = b_igate.shape[1]
    dv, dk = state_mlstm_C.shape[-2:]
    qk, v = h_n * dk, h_n * dv
    gh = (w_in.shape[2] - (2 * qk + 2 * v + 2 * h_n + 2 * d)) // 2
    dm = Dims(Bp=bp, S=s, Bs=bs, DS=ds_, D=d, DEPTH=depth, H=h_n, DK=dk, DV=dv, GH=gh,
              G=w_spatial.shape[1], GC=w_spatial.shape[2], EG=w_router_group.shape[-1],
              NE=w_router_expert.shape[-1], DE=w_expert_gate.shape[-1])
    assert dm.DS == ROW_CHUNK and dm.Ts <= ROW_TILE and s % ROW_TILE == 0 and s % MLSTM_CHUNK == 0
    assert dm.EG + dm.NE <= LANES and 2 * h_n <= LANES and dm.Bp + dm.Bs <= 32
    assert dm.off_v % v == 0 and dm.off_o % v == 0 and dm.off_z % (2 * gh) == 0 and d % 256 == 0

    c_all = jnp.concatenate([c_prompt, c_sample, jnp.zeros((32 - bp - bs, d), F32)], axis=0)
    mod = _adaln(c_all, w_ada, b_ada)[:, :bp + bs].reshape(depth, bp + bs, 6, d)
    mod = jnp.pad(mod, ((0, 0), (0, dm.n_mod - bp - bs), (0, 0), (0, 0))).transpose(0, 2, 1, 3)

    xs_rows = jnp.pad(x_sample.reshape(dm.Ts, d), ((0, ROW_TILE - dm.Ts), (0, 0)))
    x = (x_prompt.reshape(dm.Tp, d), xs_rows)
    off_if = 2 * qk + 2 * v
    zeros_c = jnp.zeros((1, bp, h_n, dv, dk), F32)
    zeros_n = jnp.zeros((1, bp, h_n, dk), F32)
    zeros_m = jnp.zeros((1, bp, 1, LANES), F32)
    m_in = _pad_lanes(state_mlstm_m)[:, :, None, :]
    out_c_p, out_n_p, out_m_p, out_c_s, out_n_s, out_m_s, out_v = [], [], [], [], [], [], []

    for l in range(depth):
        w_if = _pad_lanes(w_in[l, :, off_if:off_if + 2 * h_n])
        res = _norm_proj(dm, l, x, norm1_g.reshape(depth, 1, d), mod, w_if, SC1, SH1)
        if l == 0:
            hb, gates, x = res
        else:
            hb, gates = res
        proj = _in_proj(dm, l, hb, w_in)

        bias = _pad_lanes(jnp.concatenate([b_igate[l], b_fgate[l]])[None, :])
        mhg = mh_norm_g[l][None, :]
        a_p, c_p, n_p, m_p = _mlstm(dm, proj, gates, bias, mhg, zeros_c, zeros_n, zeros_m, 0, sample=False)
        a_s, c_s, n_s, m_s = _mlstm(dm, proj, gates, bias, mhg, state_mlstm_C, state_mlstm_n, m_in, l,
                                    sample=True)
        lng, lnb = sgu_ln_g[l][None, :], sgu_ln_b[l][None, :]
        bst = _pad_lanes(b_spatial[l].T)
        (g_p,) = _sgu(dm, proj, lng, lnb, w_spatial[l], bst, sample=False)
        g_s, v_rows = _sgu(dm, proj, lng, lnb, w_spatial[l], bst, sample=True)

        merged = _merge(dm, a_p, a_s, g_p, g_s, proj, w_branch_a[l].astype(BF16), w_branch_b[l].astype(BF16))
        w_router = _pad_lanes(jnp.concatenate([w_router_group[l], w_router_expert[l]], axis=1))
        b_router = _pad_lanes(jnp.concatenate([b_router_group[l], b_router_expert[l]])[None, :])
        x1, hp, eid, gate = _outproj(dm, l, merged, x, w_out[l].astype(BF16), norm2_g.reshape(depth, 1, d),
                                     mod, w_router, b_router)

        rank, counts = _rank(dm, eid)
        dest, blk_e, n_used = _moe_plan(dm, eid, rank, counts)
        xs = _dispatch(dm, dest, hp, jnp.zeros((dm.cap, d // 2), U32))
        ys = _experts(dm, l, blk_e, n_used, xs, w_expert_gate, w_expert_up, w_expert_down)
        x = _combine(dm, l, dest, ys, x1, gate, mod)

        out_c_p.append(c_p)
        out_n_p.append(n_p)
        out_m_p.append(m_p[:, 0, :h_n])
        out_c_s.append(c_s)
        out_n_s.append(n_s)
        out_m_s.append(m_s[:, 0, :h_n])
        out_v.append(v_rows)

    y_p, y_s = _final_norm(dm, x, final_norm_g[None, :])
    return (y_p.reshape(bp, s, d), y_s[:dm.Ts].reshape(bs, ds_, d),
            jnp.stack(out_c_p), jnp.stack(out_n_p), jnp.stack(out_m_p),
            jnp.stack(out_c_s), jnp.stack(out_n_s), jnp.stack(out_m_s), jnp.stack(out_v))
```

```python
import functools
from typing import NamedTuple

import jax
import jax.numpy as jnp
from jax import lax
from jax.experimental import pallas as pl
from jax.experimental.pallas import tpu as pltpu

F32, BF16, U32, I32 = jnp.float32, jnp.bfloat16, jnp.uint32, jnp.int32
EPS = 1e-6
NEG = -1e30
LANES = 128
ROW_TILE = 512
ROW_CHUNK = 16
MLSTM_CHUNK = 128
MOE_BLOCK = 512
TOP_K = 2
DMA_UNROLL = 8
CHUNK_UNROLL = 2
MIB = 1024 * 1024
SH1, SC1, GT1, SH2, SC2, GT2 = range(6)


class Dims(NamedTuple):
    Bp: int
    S: int
    Bs: int
    DS: int
    D: int
    DEPTH: int
    H: int
    DK: int
    DV: int
    GH: int
    G: int
    GC: int
    EG: int
    NE: int
    DE: int

    @property
    def QK(self):
        return self.H * self.DK

    @property
    def V(self):
        return self.H * self.DV

    @property
    def Tp(self):
        return self.Bp * self.S

    @property
    def Ts(self):
        return self.Bs * self.DS

    @property
    def Tc(self):
        return self.Tp + ROW_TILE

    @property
    def n_tiles(self):
        return self.Tc // ROW_TILE

    @property
    def n_mod(self):
        return self.Bp + ROW_TILE // ROW_CHUNK

    @property
    def off_v(self):
        return 2 * self.QK

    @property
    def off_o(self):
        return 2 * self.QK + self.V

    @property
    def off_z(self):
        return 2 * self.QK + 2 * self.V

    @property
    def off_ga(self):
        return self.off_z + 2 * self.GH

    @property
    def off_gb(self):
        return self.off_ga + self.D

    @property
    def n_main(self):
        return self.off_gb + self.D

    @property
    def cap(self):
        tk = self.Tc * TOP_K
        return -(-(tk + self.NE * (MOE_BLOCK - 1)) // MOE_BLOCK) * MOE_BLOCK


def _cparams(semantics, vmem_mib):
    return pltpu.CompilerParams(dimension_semantics=semantics, vmem_limit_bytes=int(vmem_mib * MIB))


def _pick_tile(n, cap, unit=256):
    best = None
    for t in range(unit, min(n, cap) + 1, unit):
        if n % t == 0:
            best = t
    assert best is not None, (n, cap)
    return best


def _dot(a, b):
    return jnp.dot(a, b, preferred_element_type=F32)


def _split_bf16(x):
    hi = x.astype(BF16)
    lo = (x - hi.astype(F32)).astype(BF16)
    return hi, lo


def _dot3(a_hi, a_lo, w_hi, w_lo):
    return _dot(a_hi, w_hi) + _dot(a_hi, w_lo) + _dot(a_lo, w_hi)


def _rms_mod(x, g, sc, sh):
    y = x * lax.rsqrt(jnp.mean(x * x, axis=-1, keepdims=True) + EPS) * g
    return y * (1.0 + sc) + sh


def _pack_pair(x):
    d2 = x.shape[-1] // 2
    bits = lax.bitcast_convert_type(x.astype(BF16).astype(F32), U32)
    return (bits[:, :d2] >> 16) | (bits[:, d2:] & jnp.uint32(0xFFFF0000))


def _unpack_pair(p):
    lo = lax.bitcast_convert_type(p << 16, F32)
    hi = lax.bitcast_convert_type(p & jnp.uint32(0xFFFF0000), F32)
    return lo, hi


def _adaln_kernel(c_ref, w_ref, b_ref, o_ref):
    c = c_ref[...]
    s = (c * jax.nn.sigmoid(c)).astype(BF16)
    o_ref[...] = _dot(s, w_ref[...].astype(BF16)) + b_ref[...]


def _adaln(c_all, w_ada, b_ada):
    depth, d, n6 = w_ada.shape
    r = c_all.shape[0]
    tn = _pick_tile(n6, 1024)
    return pl.pallas_call(
        _adaln_kernel,
        grid=(depth, n6 // tn),
        in_specs=[pl.BlockSpec((r, d), lambda l, j: (0, 0)),
                  pl.BlockSpec((None, d, tn), lambda l, j: (l, 0, j)),
                  pl.BlockSpec((None, 1, tn), lambda l, j: (l, 0, j))],
        out_specs=pl.BlockSpec((None, r, tn), lambda l, j: (l, 0, j)),
        out_shape=jax.ShapeDtypeStruct((depth, r, n6), F32),
        compiler_params=_cparams(("arbitrary", "arbitrary"), 2 * d * tn * 4 / MIB + 3 * d * tn * 2 / MIB + 8),
        name="adaln",
    )(c_all, w_ada, b_ada.reshape(depth, 1, n6))


def _norm_proj_kernel(*refs, first, dm, k_sc, k_sh):
    if first:
        xp_ref, xs_ref, g_ref, mod_ref, w_ref, h_ref, p_ref, xc_ref, lo_scr = refs
    else:
        x_ref, g_ref, mod_ref, w_ref, h_ref, p_ref, lo_scr = refs
    i = pl.program_id(0)
    is_s = i == dm.n_tiles - 1
    bp = jnp.minimum(i // (dm.S // ROW_TILE), dm.Bp - 1)
    g = g_ref[...]

    def chunk(j, carry):
        rows = pl.ds(pl.multiple_of(j * ROW_CHUNK, ROW_CHUNK), ROW_CHUNK)
        if first:
            x = jnp.where(is_s, xs_ref[rows, :], xp_ref[rows, :])
            xc_ref[rows, :] = x
        else:
            x = x_ref[rows, :]
        idx = jnp.where(is_s, dm.Bp + j, bp)
        h = _rms_mod(x, g, mod_ref[k_sc, pl.ds(idx, 1), :], mod_ref[k_sh, pl.ds(idx, 1), :])
        hi, lo = _split_bf16(h)
        h_ref[rows, :] = hi
        lo_scr[rows, :] = lo
        return carry

    lax.fori_loop(0, ROW_TILE // ROW_CHUNK, chunk, 0, unroll=CHUNK_UNROLL)
    lane = lax.broadcasted_iota(I32, w_ref.shape, 1)
    w_hi, w_lo = _split_bf16(jnp.where(lane < 2 * dm.H, w_ref[...], 0.0))
    p_ref[...] = _dot3(h_ref[...], lo_scr[...], w_hi, w_lo)


def _norm_proj(dm, l, x_in, g, mod, w_in, k_sc, k_sh):
    d, tc, nt = dm.D, dm.Tc, dm.n_tiles
    first = isinstance(x_in, tuple)
    np_tiles = dm.Tp // ROW_TILE
    tile = lambda i: (i, 0)
    if first:
        x_args = list(x_in)
        x_specs = [pl.BlockSpec((ROW_TILE, d), lambda i: (jnp.minimum(i, np_tiles - 1), 0)),
                   pl.BlockSpec((ROW_TILE, d), lambda i: (0, 0))]
    else:
        x_args = [x_in]
        x_specs = [pl.BlockSpec((ROW_TILE, d), tile)]
    out_shape = [jax.ShapeDtypeStruct((tc, d), BF16), jax.ShapeDtypeStruct((tc, LANES), F32)]
    out_specs = [pl.BlockSpec((ROW_TILE, d), tile), pl.BlockSpec((ROW_TILE, LANES), tile)]
    if first:
        out_shape.append(jax.ShapeDtypeStruct((tc, d), F32))
        out_specs.append(pl.BlockSpec((ROW_TILE, d), tile))
    return pl.pallas_call(
        functools.partial(_norm_proj_kernel, first=first, dm=dm, k_sc=k_sc, k_sh=k_sh),
        grid=(nt,),
        in_specs=x_specs + [pl.BlockSpec((None, 1, d), lambda i: (l, 0, 0)),
                            pl.BlockSpec((None, 6, dm.n_mod, d), lambda i: (l, 0, 0, 0)),
                            pl.BlockSpec((None, d, LANES), lambda i: (l, 0, dm.off_z // LANES))],
        out_specs=out_specs,
        out_shape=out_shape,
        scratch_shapes=[pltpu.VMEM((ROW_TILE, d), BF16)],
        compiler_params=_cparams(("arbitrary",), ROW_TILE * d * 40 / MIB + 16),
        name="norm_proj",
    )(*x_args, g, mod, w_in)


def _in_proj_kernel(a_ref, w_ref, wn_ref, o_ref, w_scr, *, n_aligned, shift):
    j, i = pl.program_id(0), pl.program_id(1)

    @pl.when((i == 0) & (j < n_aligned))
    def _cast():
        w_scr[...] = w_ref[...].astype(BF16)

    @pl.when((i == 0) & (j >= n_aligned))
    def _cast_shifted():
        groups = w_ref.shape[1] // LANES
        keep = lax.broadcasted_iota(I32, (w_ref.shape[0], LANES), 1) < LANES - shift
        for g in range(groups):
            cur = w_ref[:, g * LANES:(g + 1) * LANES]
            nxt = w_ref[:, (g + 1) * LANES:(g + 2) * LANES] if g + 1 < groups else wn_ref[...]
            w_scr[:, g * LANES:(g + 1) * LANES] = jnp.where(
                keep, pltpu.roll(cur, LANES - shift, axis=1), pltpu.roll(nxt, LANES - shift, axis=1)).astype(BF16)

    o_ref[...] = _dot(a_ref[...], w_scr[...]).astype(o_ref.dtype)


def _in_proj(dm, l, a, w_in):
    m, k = a.shape
    off_if, n = dm.off_z, dm.n_main
    shift = 2 * dm.H
    tn = max(t for t in range(LANES, 1024 + 1, LANES) if off_if % t == 0 and n % t == 0)
    tm = _pick_tile(m, 1536)
    vmem = (2 * tm * k * 2 + 2 * k * (tn + LANES) * 4 + k * tn * 2 + 2 * tm * tn * 2 + 2 * tm * tn * 4) / MIB + 6
    return pl.pallas_call(
        functools.partial(_in_proj_kernel, n_aligned=off_if // tn, shift=shift),
        grid=(n // tn, m // tm),
        in_specs=[pl.BlockSpec((tm, k), lambda j, i: (i, 0)),
                  pl.BlockSpec((None, k, tn), lambda j, i: (l, 0, j)),
                  pl.BlockSpec((None, k, LANES), lambda j, i: (l, 0, (j + 1) * (tn // LANES)))],
        out_specs=pl.BlockSpec((tm, tn), lambda j, i: (i, j)),
        out_shape=jax.ShapeDtypeStruct((m, n), BF16),
        scratch_shapes=[pltpu.VMEM((k, tn), BF16)],
        compiler_params=_cparams(("arbitrary", "arbitrary"), vmem),
        name="in_proj",
    )(a, w_in, w_in)


def _pad_rows(x, rows):
    if x.shape[0] == rows:
        return x
    return jnp.concatenate([x, jnp.zeros((rows - x.shape[0], x.shape[1]), x.dtype)], axis=0)


def _mlstm_kernel(*refs, dm, lv, n_seq):
    a_ref = refs[10]
    seq_ok = pl.program_id(0) < n_seq

    @pl.when(seq_ok)
    def _run():
        _mlstm_step(*refs, dm=dm, lv=lv)

    @pl.when(jnp.logical_not(seq_ok))
    def _fill():
        a_ref[...] = jnp.zeros_like(a_ref)


def _mlstm_step(q_ref, k_ref, v_ref, o_ref, gt_ref, bias_ref, mhg_ref, c0_ref, n0_ref, m0_ref,
                a_ref, cout_ref, nout_ref, mout_ref, st_scr, m_scr, *, dm, lv):
    h_n, dk, dv = dm.H, dm.DK, dm.DV
    L = MLSTM_CHUNK
    c = pl.program_id(1)

    @pl.when(c == 0)
    def _init():
        for h in range(h_n):
            st_scr[h, :, :dv] = c0_ref[h].T
            st_scr[h, :, dv:] = jnp.broadcast_to(n0_ref[h:h + 1, :], (LANES, dk)).T
        m_scr[...] = m0_ref[...]

    row = lax.broadcasted_iota(I32, (L, L), 0)
    col = lax.broadcasted_iota(I32, (L, L), 1)
    causal = col <= row
    lane = lax.broadcasted_iota(I32, (L, LANES), 1)
    rowl = lax.broadcasted_iota(I32, (L, LANES), 0)

    xg = _pad_rows(gt_ref[...], L) + bias_ref[...]
    f_log = jnp.minimum(xg, 0.0) - jnp.log1p(jnp.exp(-jnp.abs(xg)))
    gl = jnp.where(lane < h_n, xg, f_log)
    gl = jnp.where(rowl < lv, gl, 0.0)
    tri = causal.astype(BF16)
    g_hi = gl.astype(BF16)
    r1 = gl - g_hi.astype(F32)
    g_mid = r1.astype(BF16)
    g_lo = (r1 - g_mid.astype(F32)).astype(BF16)
    cum = _dot(tri, g_hi) + _dot(tri, g_mid) + _dot(tri, g_lo)
    b_all = pltpu.roll(cum, LANES - h_n, axis=1)
    a_all = jnp.where(rowl < lv, gl - b_all, NEG)

    scale = dk ** -0.5
    kf = _pad_rows(k_ref[...], L).astype(F32) * scale
    kt_all = kf.T.astype(BF16)
    q_all = _pad_rows(q_ref[...], L)
    v_all = _pad_rows(v_ref[...], L)
    a_rows = a_all.T
    m_old = m_scr[...]
    m_new = m_old
    lane1 = lax.broadcasted_iota(I32, (1, LANES), 1)
    ones_blk = jnp.ones((L, LANES), BF16)
    v_blocks = dv // LANES

    for h in range(h_n):
        a_row = a_rows[h:h + 1, :]
        a_rep = jnp.broadcast_to(a_all[:, h:h + 1], (L, LANES))
        b_rep = jnp.broadcast_to(b_all[:, h:h + 1], (L, LANES))
        m_prev = m_old[:, h:h + 1]
        big_m = jnp.maximum(jnp.max(jnp.where(causal, a_row, NEG), axis=1, keepdims=True), m_prev)
        d_w = jnp.where(causal, jnp.exp(jnp.minimum(a_row - big_m, 0.0)), 0.0)
        inter = jnp.exp(m_prev - big_m)
        qh = q_all[:, h * dk:(h + 1) * dk]
        v_parts = [v_all[:, h * dv + j * LANES:h * dv + (j + 1) * LANES] for j in range(v_blocks)]
        kt = kt_all[h * dk:(h + 1) * dk, :]
        sd = (_dot(qh, kt) * d_w).astype(BF16)
        st = st_scr[h]
        x = _dot(sd, jnp.concatenate(v_parts + [ones_blk], axis=1)) + inter * _dot(qh, st.astype(BF16))
        den = x[:, dv:]
        inv = 1.0 / jnp.maximum(jnp.abs(den), jnp.exp(-(b_rep + big_m)))
        sq = jnp.sum(x[:, :dv] * x[:, :dv], axis=1, keepdims=True) * (1.0 / dv)
        sc = (inv * lax.rsqrt(inv * inv * sq + EPS))[:lv]
        for j in range(v_blocks):
            cols = slice(h * dv + j * LANES, h * dv + (j + 1) * LANES)
            gate_o = jax.nn.sigmoid(o_ref[:, cols].astype(F32))
            a_ref[:, cols] = (x[:lv, j * LANES:(j + 1) * LANES] * sc * (mhg_ref[:, cols] * gate_o)).astype(BF16)
        m_last = big_m[L - 1:L, :]
        decay = jnp.exp(m_prev - m_last)
        w_rep = jnp.exp(a_rep - m_last)
        vw = jnp.concatenate([(p.astype(F32) * w_rep).astype(BF16) for p in v_parts] + [w_rep.astype(BF16)], axis=1)
        st_scr[h] = decay * st + _dot(kt, vw)
        m_new = jnp.where(lane1 == h, b_all[L - 1:L, h:h + 1] + m_last, m_new)
    m_scr[...] = m_new

    @pl.when(c == pl.num_programs(1) - 1)
    def _fin():
        for h in range(h_n):
            cout_ref[h] = st_scr[h, :, :dv].T
            nout_ref[h:h + 1, :] = st_scr[h, :, dv:].T[0:1, :]
        mout_ref[...] = m_scr[...]


def _mlstm(dm, proj, gates, bias, mhg, c0, n0, m0, ls, *, sample):
    if sample:
        b_n, n_grid, nc, lv, row0, n_rows = dm.Bs, ROW_TILE // dm.DS, 1, dm.DS, dm.Tp // dm.DS, ROW_TILE
    else:
        b_n, n_grid, nc, lv, row0, n_rows = dm.Bp, dm.Bp, dm.S // MLSTM_CHUNK, MLSTM_CHUNK, 0, dm.Tp
    h_n, dk, dv, qk, v = dm.H, dm.DK, dm.DV, dm.QK, dm.V
    rb = lambda b, c: row0 + b * nc + c
    sq = lambda b: jnp.minimum(b, b_n - 1)
    in_specs = [
        pl.BlockSpec((lv, qk), lambda b, c: (rb(b, c), 0)),
        pl.BlockSpec((lv, qk), lambda b, c: (rb(b, c), 1)),
        pl.BlockSpec((lv, v), lambda b, c: (rb(b, c), dm.off_v // v)),
        pl.BlockSpec((lv, v), lambda b, c: (rb(b, c), dm.off_o // v)),
        pl.BlockSpec((lv, LANES), lambda b, c: (rb(b, c), 0)),
        pl.BlockSpec((1, LANES), lambda b, c: (0, 0)),
        pl.BlockSpec((1, v), lambda b, c: (0, 0)),
        pl.BlockSpec((None, None, h_n, dv, dk), lambda b, c: (ls, sq(b), 0, 0, 0)),
        pl.BlockSpec((None, None, h_n, dk), lambda b, c: (ls, sq(b), 0, 0)),
        pl.BlockSpec((None, None, 1, LANES), lambda b, c: (ls, sq(b), 0, 0)),
    ]
    return pl.pallas_call(
        functools.partial(_mlstm_kernel, dm=dm, lv=lv, n_seq=b_n),
        grid=(n_grid, nc),
        in_specs=in_specs,
        out_specs=[pl.BlockSpec((lv, v), lambda b, c: (b * nc + c, 0)),
                   pl.BlockSpec((None, h_n, dv, dk), lambda b, c: (sq(b), 0, 0, 0)),
                   pl.BlockSpec((None, h_n, dk), lambda b, c: (sq(b), 0, 0)),
                   pl.BlockSpec((None, 1, LANES), lambda b, c: (sq(b), 0, 0))],
        out_shape=[jax.ShapeDtypeStruct((n_rows, v), BF16),
                   jax.ShapeDtypeStruct((b_n, h_n, dv, dk), F32),
                   jax.ShapeDtypeStruct((b_n, h_n, dk), F32),
                   jax.ShapeDtypeStruct((b_n, 1, LANES), F32)],
        scratch_shapes=[pltpu.VMEM((h_n, dk, dv + LANES), F32), pltpu.VMEM((1, LANES), F32)],
        compiler_params=_cparams(("arbitrary", "arbitrary"), 40),
        name="mlstm_sample" if sample else "mlstm_prompt",
    )(proj, proj, proj, proj, gates, bias, mhg, c0, n0, m0)


def _sgu_kernel(*refs, dm, lv, n_seq, emit_v):
    g_ref = refs[5]
    seq_ok = pl.program_id(0) < n_seq

    @pl.when(seq_ok)
    def _run():
        _sgu_step(*refs, dm=dm, lv=lv, emit_v=emit_v)

    @pl.when(jnp.logical_not(seq_ok))
    def _fill():
        g_ref[...] = jnp.zeros_like(g_ref)


def _sgu_step(*refs, dm, lv, emit_v):
    z_ref, lng_ref, lnb_ref, ws_ref, bst_ref, g_ref = refs[:6]
    gh, gch = dm.GH, dm.GH // dm.G
    z = jax.nn.gelu(z_ref[...].astype(F32))
    u, v = z[:, :gh], z[:, gh:]
    xc = v - jnp.mean(v, axis=-1, keepdims=True)
    vn = xc * lax.rsqrt(jnp.mean(xc * xc, axis=-1, keepdims=True) + EPS) * lng_ref[...] + lnb_ref[...]
    if emit_v:
        refs[6][...] = vn
    row = lax.broadcasted_iota(I32, (lv, lv), 0)
    col = lax.broadcasted_iota(I32, (lv, lv), 1)
    vb = vn.astype(BF16)
    for g in range(dm.G):
        w = jnp.where(col <= row, ws_ref[g, :lv, :lv], 0.0).astype(BF16)
        mixed = _dot(w, vb[:, g * gch:(g + 1) * gch]) + bst_ref[:lv, g:g + 1]
        g_ref[:, g * gch:(g + 1) * gch] = (u[:, g * gch:(g + 1) * gch] * mixed).astype(BF16)


def _sgu(dm, proj, lng, lnb, ws, bst, *, sample):
    if sample:
        b_n, n_grid, nc, lv, row0, n_rows = dm.Bs, ROW_TILE // dm.DS, 1, dm.DS, dm.Tp // dm.DS, ROW_TILE
    else:
        b_n, n_grid, nc, lv, row0, n_rows = dm.Bp, dm.Bp, dm.S // dm.GC, dm.GC, 0, dm.Tp
    gh = dm.GH
    rb = lambda b, c: row0 + b * nc + c
    in_specs = [
        pl.BlockSpec((lv, 2 * gh), lambda b, c: (rb(b, c), dm.off_z // (2 * gh))),
        pl.BlockSpec((1, gh), lambda b, c: (0, 0)),
        pl.BlockSpec((1, gh), lambda b, c: (0, 0)),
        pl.BlockSpec((dm.G, dm.GC, dm.GC), lambda b, c: (0, 0, 0)),
        pl.BlockSpec((dm.GC, LANES), lambda b, c: (0, 0)),
    ]
    out_specs = [pl.BlockSpec((lv, gh), lambda b, c: (b * nc + c, 0))]
    out_shape = [jax.ShapeDtypeStruct((n_rows, gh), BF16)]
    if sample:
        out_specs.append(pl.BlockSpec((None, lv, gh), lambda b, c: (jnp.minimum(b, b_n - 1), 0, 0)))
        out_shape.append(jax.ShapeDtypeStruct((b_n, lv, gh), F32))
    return pl.pallas_call(
        functools.partial(_sgu_kernel, dm=dm, lv=lv, n_seq=b_n, emit_v=sample),
        grid=(n_grid, nc),
        in_specs=in_specs,
        out_specs=out_specs,
        out_shape=out_shape,
        compiler_params=_cparams(("arbitrary", "arbitrary"), 24),
        name="sgu_sample" if sample else "sgu_prompt",
    )(proj, lng, lnb, ws, bst)


def _merge_kernel(ap_ref, as_ref, gp_ref, gs_ref, ga_ref, gb_ref, wa_ref, wb_ref, o_ref):
    is_s = pl.program_id(1) == pl.num_programs(1) - 1
    pa = _dot(jnp.where(is_s, as_ref[...], ap_ref[...]), wa_ref[...])
    pb = _dot(jnp.where(is_s, gs_ref[...], gp_ref[...]), wb_ref[...])
    o_ref[...] = (jax.nn.sigmoid(ga_ref[...].astype(F32)) * pa
                  + jax.nn.sigmoid(gb_ref[...].astype(F32)) * pb).astype(o_ref.dtype)


def _merge(dm, a_p, a_s, g_p, g_s, proj, wa, wb):
    tc, d, v, gh = dm.Tc, dm.D, dm.V, dm.GH
    tm = ROW_TILE
    tn = _pick_tile(d, 1024)
    np_tiles = dm.Tp // tm
    prompt = lambda j, i: (jnp.minimum(i, np_tiles - 1), 0)
    const = lambda j, i: (0, 0)
    blocks = 2 * tm * (v + gh) * 2 + 3 * tm * tn * 2 + (v + gh) * tn * 2
    return pl.pallas_call(
        _merge_kernel,
        grid=(d // tn, tc // tm),
        in_specs=[pl.BlockSpec((tm, v), prompt),
                  pl.BlockSpec((tm, v), const),
                  pl.BlockSpec((tm, gh), prompt),
                  pl.BlockSpec((tm, gh), const),
                  pl.BlockSpec((tm, tn), lambda j, i: (i, dm.off_ga // tn + j)),
                  pl.BlockSpec((tm, tn), lambda j, i: (i, dm.off_gb // tn + j)),
                  pl.BlockSpec((v, tn), lambda j, i: (0, j)),
                  pl.BlockSpec((gh, tn), lambda j, i: (0, j))],
        out_specs=pl.BlockSpec((tm, tn), lambda j, i: (i, j)),
        out_shape=jax.ShapeDtypeStruct((tc, d), BF16),
        compiler_params=_cparams(("arbitrary", "arbitrary"), (2 * blocks + 6 * tm * tn * 4) / MIB + 8),
        name="merge",
    )(a_p, a_s, g_p, g_s, proj, proj, wa, wb)


def _route(logits, eg, ne):
    epg = ne // eg
    lane = lax.broadcasted_iota(I32, logits.shape, 1)
    gmask = lane < eg
    gmax = jnp.max(jnp.where(gmask, logits, NEG), axis=1, keepdims=True)
    gexp = jnp.where(gmask, jnp.exp(jnp.minimum(logits - gmax, 0.0)), 0.0)
    pg = gexp / jnp.sum(gexp, axis=1, keepdims=True)
    p_grp = jnp.max(pg, axis=1, keepdims=True)
    g_sel = jnp.min(jnp.where(gmask & (pg == p_grp), lane, LANES), axis=1, keepdims=True)
    lo = eg + g_sel * epg
    emask = (lane >= lo) & (lane < lo + epg)
    emax = jnp.max(jnp.where(emask, logits, NEG), axis=1, keepdims=True)
    eexp = jnp.where(emask, jnp.exp(jnp.minimum(logits - emax, 0.0)), 0.0)
    pe = eexp / jnp.sum(eexp, axis=1, keepdims=True)
    p1 = jnp.max(jnp.where(emask, pe, -1.0), axis=1, keepdims=True)
    i1 = jnp.min(jnp.where(emask & (pe == p1), lane, LANES), axis=1, keepdims=True)
    rest = emask & (lane != i1)
    p2 = jnp.max(jnp.where(rest, pe, -1.0), axis=1, keepdims=True)
    i2 = jnp.min(jnp.where(rest & (pe == p2), lane, LANES), axis=1, keepdims=True)
    psum = p1 + p2
    eid = jnp.where(lane == 0, i1 - eg, jnp.where(lane == 1, i2 - eg, 0))
    gate = jnp.where(lane == 0, p_grp * (p1 / psum), jnp.where(lane == 1, p_grp * (p2 / psum), 0.0))
    return eid, gate


def _outproj_kernel(m_ref, x_ref, w_ref, g_ref, mod_ref, wr_ref, rb_ref,
                    x1_ref, hp_ref, eid_ref, gate_ref, acc_scr, hi_scr, lo_scr, *, dm):
    i = pl.program_id(0)
    is_s = i == dm.n_tiles - 1
    bp = jnp.minimum(i // (dm.S // ROW_TILE), dm.Bp - 1)
    acc_scr[...] = _dot(m_ref[...], w_ref[...])
    g = g_ref[...]

    def chunk(j, carry):
        rows = pl.ds(pl.multiple_of(j * ROW_CHUNK, ROW_CHUNK), ROW_CHUNK)
        idx = jnp.where(is_s, dm.Bp + j, bp)
        x1 = x_ref[rows, :] + mod_ref[GT1, pl.ds(idx, 1), :] * acc_scr[rows, :]
        x1_ref[rows, :] = x1
        h = _rms_mod(x1, g, mod_ref[SC2, pl.ds(idx, 1), :], mod_ref[SH2, pl.ds(idx, 1), :])
        hi, lo = _split_bf16(h)
        hi_scr[rows, :] = hi
        lo_scr[rows, :] = lo
        hp_ref[rows, :] = _pack_pair(h)
        return carry

    lax.fori_loop(0, ROW_TILE // ROW_CHUNK, chunk, 0, unroll=CHUNK_UNROLL)
    w_hi, w_lo = _split_bf16(wr_ref[...])
    logits = _dot3(hi_scr[...], lo_scr[...], w_hi, w_lo) + rb_ref[...]
    eid, gate = _route(logits, dm.EG, dm.NE)
    eid_ref[...] = eid
    gate_ref[...] = gate


def _outproj(dm, l, merged, x, w_out, g2, mod, w_router, b_router):
    d, tc = dm.D, dm.Tc
    tile = lambda i: (i, 0)
    const = lambda i: (0, 0)
    return pl.pallas_call(
        functools.partial(_outproj_kernel, dm=dm),
        grid=(dm.n_tiles,),
        in_specs=[pl.BlockSpec((ROW_TILE, d), tile),
                  pl.BlockSpec((ROW_TILE, d), tile),
                  pl.BlockSpec((d, d), const),
                  pl.BlockSpec((None, 1, d), lambda i: (l, 0, 0)),
                  pl.BlockSpec((None, 6, dm.n_mod, d), lambda i: (l, 0, 0, 0)),
                  pl.BlockSpec((d, LANES), const),
                  pl.BlockSpec((1, LANES), const)],
        out_specs=[pl.BlockSpec((ROW_TILE, d), tile),
                   pl.BlockSpec((ROW_TILE, d // 2), tile),
                   pl.BlockSpec((ROW_TILE, LANES), tile),
                   pl.BlockSpec((ROW_TILE, LANES), tile)],
        out_shape=[jax.ShapeDtypeStruct((tc, d), F32),
                   jax.ShapeDtypeStruct((tc, d // 2), U32),
                   jax.ShapeDtypeStruct((tc, LANES), I32),
                   jax.ShapeDtypeStruct((tc, LANES), F32)],
        scratch_shapes=[pltpu.VMEM((ROW_TILE, d), F32), pltpu.VMEM((ROW_TILE, d), BF16),
                        pltpu.VMEM((ROW_TILE, d), BF16)],
        compiler_params=_cparams(("arbitrary",), (4 * d * d + ROW_TILE * d * 40) / MIB + 12),
        name="out_proj",
    )(merged, x, w_out, g2, mod, w_router, b_router)


def _rank_kernel(eid_ref, rank_ref, cnt_ref, run_scr):
    i = pl.program_id(0)

    @pl.when(i == 0)
    def _init():
        run_scr[...] = jnp.zeros_like(run_scr)

    eid = eid_ref[...]
    lane = lax.broadcasted_iota(I32, eid.shape, 1)
    e0 = lane == eid[:, 0:1]
    e1 = lane == eid[:, 1:2]
    hot = (e0 | e1).astype(BF16)
    n = eid.shape[0]
    strict = (lax.broadcasted_iota(I32, (n, n), 1) < lax.broadcasted_iota(I32, (n, n), 0)).astype(BF16)
    before = _dot(strict, hot) + run_scr[...]
    r0 = jnp.sum(jnp.where(e0, before, 0.0), axis=1, keepdims=True)
    r1 = jnp.sum(jnp.where(e1, before, 0.0), axis=1, keepdims=True)
    rank_ref[...] = jnp.where(lane == 0, r0, jnp.where(lane == 1, r1, 0.0)).astype(I32)
    run_scr[...] = run_scr[...] + jnp.sum(hot.astype(F32), axis=0, keepdims=True)
    cnt_ref[...] = run_scr[...].astype(I32)


def _rank(dm, eid):
    tile = lambda i: (i, 0)
    return pl.pallas_call(
        _rank_kernel,
        grid=(dm.n_tiles,),
        in_specs=[pl.BlockSpec((ROW_TILE, LANES), tile)],
        out_specs=[pl.BlockSpec((ROW_TILE, LANES), tile), pl.BlockSpec((1, LANES), lambda i: (0, 0))],
        out_shape=[jax.ShapeDtypeStruct((dm.Tc, LANES), I32), jax.ShapeDtypeStruct((1, LANES), I32)],
        scratch_shapes=[pltpu.VMEM((1, LANES), F32)],
        compiler_params=_cparams(("arbitrary",), 16),
        name="moe_rank",
    )(eid)


def _dispatch_kernel(dest_ref, h_ref, xs_in_ref, xs_ref, sem):
    del xs_in_ref

    def issue(r, carry):
        for k in range(TOP_K):
            pltpu.make_async_copy(h_ref.at[pl.ds(r, 1)], xs_ref.at[pl.ds(dest_ref[TOP_K * r + k], 1)],
                                  sem).start(priority=k)
        return carry

    lax.fori_loop(0, ROW_TILE, issue, 0, unroll=DMA_UNROLL)
    for k in range(TOP_K):
        pltpu.make_async_copy(h_ref, xs_ref.at[pl.ds(0, ROW_TILE)], sem).wait()


def _dispatch(dm, dest, hp, xs_zero):
    d2 = dm.D // 2
    return pl.pallas_call(
        _dispatch_kernel,
        grid=(dm.n_tiles,),
        in_specs=[pl.BlockSpec((ROW_TILE * TOP_K,), lambda i: (i,), memory_space=pltpu.SMEM),
                  pl.BlockSpec((ROW_TILE, d2), lambda i: (i, 0)),
                  pl.BlockSpec(memory_space=pl.ANY)],
        out_specs=pl.BlockSpec(memory_space=pl.ANY),
        out_shape=jax.ShapeDtypeStruct((dm.cap, d2), U32),
        scratch_shapes=[pltpu.SemaphoreType.DMA],
        input_output_aliases={2: 0},
        compiler_params=_cparams(("arbitrary",), 16),
        name="moe_dispatch",
    )(dest, hp, xs_zero)


def _expert_kernel(be_ref, nu_ref, xs_ref, wg_ref, wu_ref, wd_ref, ys_ref, wg_s, wu_s, wd_s):
    j = pl.program_id(0)
    changed = (j == 0) | (be_ref[j] != be_ref[jnp.maximum(j - 1, 0)])

    @pl.when(changed)
    def _cast():
        wg_s[...] = wg_ref[...].astype(BF16)
        wu_s[...] = wu_ref[...].astype(BF16)
        wd_s[...] = wd_ref[...].astype(BF16)

    @pl.when(j < nu_ref[0])
    def _compute():
        d2 = xs_ref.shape[1]
        lo, hi = _unpack_pair(xs_ref[...])
        lo, hi = lo.astype(BF16), hi.astype(BF16)
        hg = _dot(lo, wg_s[:d2, :]) + _dot(hi, wg_s[d2:, :])
        hu = _dot(lo, wu_s[:d2, :]) + _dot(hi, wu_s[d2:, :])
        act = (hg * jax.nn.sigmoid(hg) * hu).astype(BF16)
        ys_ref[...] = _pack_pair(_dot(act, wd_s[...]))

    @pl.when(j >= nu_ref[0])
    def _idle():
        ys_ref[...] = jnp.zeros_like(ys_ref)


def _experts(dm, l, blk_e, n_used, xs, w_gate, w_up, w_down):
    d, de, d2 = dm.D, dm.DE, dm.D // 2
    nb = dm.cap // MOE_BLOCK
    wmap = lambda j, be, nu: (l, be[j], 0, 0)
    return pl.pallas_call(
        _expert_kernel,
        grid_spec=pltpu.PrefetchScalarGridSpec(
            num_scalar_prefetch=2,
            grid=(nb,),
            in_specs=[pl.BlockSpec((MOE_BLOCK, d2), lambda j, be, nu: (j, 0)),
                      pl.BlockSpec((None, None, d, de), wmap),
                      pl.BlockSpec((None, None, d, de), wmap),
                      pl.BlockSpec((None, None, de, d), wmap)],
            out_specs=pl.BlockSpec((MOE_BLOCK, d2), lambda j, be, nu: (j, 0)),
            scratch_shapes=[pltpu.VMEM((d, de), BF16), pltpu.VMEM((d, de), BF16), pltpu.VMEM((de, d), BF16)]),
        out_shape=jax.ShapeDtypeStruct((dm.cap, d2), U32),
        compiler_params=_cparams(("arbitrary",), (3 * d * de * (2 * 4 + 2) + MOE_BLOCK * d * 24) / MIB + 8),
        name="moe_experts",
    )(blk_e, n_used, xs, w_gate, w_up, w_down)


def _combine_kernel(*refs, dm, last):
    if last:
        dest_ref, ys_ref, x_ref, gate_ref, mod_ref, g_ref, yp_ref, ysm_ref, ybuf, sem = refs
    else:
        (dest_ref, ys_ref, x_ref, gate_ref, mod_ref, g_ref, modn_ref, w_ref,
         o_ref, h_ref, p_ref, ybuf, lo_scr, sem) = refs
    i = pl.program_id(0)
    is_s = i == dm.n_tiles - 1
    bp = jnp.minimum(i // (dm.S // ROW_TILE), dm.Bp - 1)
    d2 = dm.D // 2
    g = g_ref[...]

    def issue(r, carry):
        for k in range(TOP_K):
            pltpu.make_async_copy(ys_ref.at[pl.ds(dest_ref[TOP_K * r + k], 1)], ybuf.at[k, pl.ds(r, 1)],
                                  sem).start(priority=k)
        return carry

    lax.fori_loop(0, ROW_TILE, issue, 0, unroll=DMA_UNROLL)
    for k in range(TOP_K):
        pltpu.make_async_copy(ys_ref.at[pl.ds(0, ROW_TILE)], ybuf.at[k], sem).wait()

    def chunk(j, carry):
        rows = pl.ds(pl.multiple_of(j * ROW_CHUNK, ROW_CHUNK), ROW_CHUNK)
        idx = jnp.where(is_s, dm.Bp + j, bp)
        gt = mod_ref[GT2, pl.ds(idx, 1), :]
        gates = gate_ref[rows, :]
        g0, g1 = gates[:, 0:1], gates[:, 1:2]
        lo0, hi0 = _unpack_pair(ybuf[0, rows, :])
        lo1, hi1 = _unpack_pair(ybuf[1, rows, :])
        x_lo = x_ref[rows, :d2] + gt[:, :d2] * (g0 * lo0 + g1 * lo1)
        x_hi = x_ref[rows, d2:] + gt[:, d2:] * (g0 * hi0 + g1 * hi1)
        ms = (jnp.sum(x_lo * x_lo, axis=-1, keepdims=True)
              + jnp.sum(x_hi * x_hi, axis=-1, keepdims=True)) * (1.0 / dm.D)
        rs = lax.rsqrt(ms + EPS)
        if last:
            @pl.when(is_s)
            def _sample_rows():
                ysm_ref[rows, :d2] = x_lo * rs * g[:, :d2]
                ysm_ref[rows, d2:] = x_hi * rs * g[:, d2:]

            @pl.when(jnp.logical_not(is_s))
            def _prompt_rows():
                yp_ref[rows, :d2] = x_lo * rs * g[:, :d2]
                yp_ref[rows, d2:] = x_hi * rs * g[:, d2:]
        else:
            o_ref[rows, :d2] = x_lo
            o_ref[rows, d2:] = x_hi
            sc = modn_ref[SC1, pl.ds(idx, 1), :]
            sh = modn_ref[SH1, pl.ds(idx, 1), :]
            for half, xh in ((slice(0, d2), x_lo), (slice(d2, dm.D), x_hi)):
                h = xh * rs * g[:, half] * (1.0 + sc[:, half]) + sh[:, half]
                hi, lo = _split_bf16(h)
                h_ref[rows, half] = hi
                lo_scr[rows, half] = lo
        return carry

    lax.fori_loop(0, ROW_TILE // ROW_CHUNK, chunk, 0, unroll=CHUNK_UNROLL)
    if not last:
        lane = lax.broadcasted_iota(I32, w_ref.shape, 1)
        w_hi, w_lo = _split_bf16(jnp.where(lane < 2 * dm.H, w_ref[...], 0.0))
        p_ref[...] = _dot3(h_ref[...], lo_scr[...], w_hi, w_lo)


def _combine(dm, l, dest, ys, x1, gate, mod, g_next, w_in=None):
    d = dm.D
    last = w_in is None
    tile = lambda i: (i, 0)
    np_tiles = dm.Tp // ROW_TILE
    in_specs = [pl.BlockSpec((ROW_TILE * TOP_K,), lambda i: (i,), memory_space=pltpu.SMEM),
                pl.BlockSpec(memory_space=pl.ANY),
                pl.BlockSpec((ROW_TILE, d), tile),
                pl.BlockSpec((ROW_TILE, LANES), tile),
                pl.BlockSpec((None, 6, dm.n_mod, d), lambda i: (l, 0, 0, 0))]
    scratch = [pltpu.VMEM((TOP_K, ROW_TILE, d // 2), U32)]
    if last:
        args = (dest, ys, x1, gate, mod, g_next)
        in_specs.append(pl.BlockSpec((1, d), lambda i: (0, 0)))
        out_specs = [pl.BlockSpec((ROW_TILE, d), lambda i: (jnp.minimum(i, np_tiles - 1), 0)),
                     pl.BlockSpec((ROW_TILE, d), lambda i: (0, 0))]
        out_shape = [jax.ShapeDtypeStruct((dm.Tp, d), F32), jax.ShapeDtypeStruct((ROW_TILE, d), F32)]
    else:
        args = (dest, ys, x1, gate, mod, g_next, mod, w_in)
        in_specs += [pl.BlockSpec((None, 1, d), lambda i: (l + 1, 0, 0)),
                     pl.BlockSpec((None, 6, dm.n_mod, d), lambda i: (l + 1, 0, 0, 0)),
                     pl.BlockSpec((None, d, LANES), lambda i: (l + 1, 0, dm.off_z // LANES))]
        out_specs = [pl.BlockSpec((ROW_TILE, d), tile), pl.BlockSpec((ROW_TILE, d), tile),
                     pl.BlockSpec((ROW_TILE, LANES), tile)]
        out_shape = [jax.ShapeDtypeStruct((dm.Tc, d), F32), jax.ShapeDtypeStruct((dm.Tc, d), BF16),
                     jax.ShapeDtypeStruct((dm.Tc, LANES), F32)]
        scratch.append(pltpu.VMEM((ROW_TILE, d), BF16))
    scratch.append(pltpu.SemaphoreType.DMA)
    return pl.pallas_call(
        functools.partial(_combine_kernel, dm=dm, last=last),
        grid=(dm.n_tiles,),
        in_specs=in_specs,
        out_specs=out_specs,
        out_shape=out_shape,
        scratch_shapes=scratch,
        compiler_params=_cparams(("arbitrary",), ROW_TILE * d * 40 / MIB + 16),
        name="moe_combine_last" if last else "moe_combine",
    )(*args)


def _pad_lanes(x, n=LANES):
    return jnp.pad(x, [(0, 0)] * (x.ndim - 1) + [(0, n - x.shape[-1])])


def _moe_plan(dm, eid, rank, counts):
    counts = counts[0, :dm.NE]
    padded = (counts + MOE_BLOCK - 1) // MOE_BLOCK * MOE_BLOCK
    pad_end = jnp.cumsum(padded)
    pad_start = pad_end - padded
    dest = (pad_start[eid[:, :TOP_K]] + rank[:, :TOP_K]).reshape(-1).astype(I32)
    nb = dm.cap // MOE_BLOCK
    n_used = (pad_end[-1] // MOE_BLOCK).astype(I32)
    starts = jnp.minimum(jnp.arange(nb, dtype=I32), n_used - 1) * MOE_BLOCK
    blk_e = jnp.minimum(jnp.sum(pad_end[None, :] <= starts[:, None], axis=1), dm.NE - 1).astype(I32)
    return dest, blk_e, n_used.reshape(1)


def kernel(x_prompt, x_sample, c_prompt, c_sample, state_mlstm_C, state_mlstm_n, state_mlstm_m, w_ada, b_ada, norm1_g, norm2_g, w_in, b_igate, b_fgate, mh_norm_g, sgu_ln_g, sgu_ln_b, w_spatial, b_spatial, w_branch_a, w_branch_b, w_out, w_router_group, b_router_group, w_router_expert, b_router_expert, w_expert_gate, w_expert_up, w_expert_down, final_norm_g):
    bp, s, d = x_prompt.shape
    bs, ds_, _ = x_sample.shape
    depth = w_in.shape[0]
    h_n = b_igate.shape[1]
    dv, dk = state_mlstm_C.shape[-2:]
    qk, v = h_n * dk, h_n * dv
    gh = (w_in.shape[2] - (2 * qk + 2 * v + 2 * h_n + 2 * d)) // 2
    dm = Dims(Bp=bp, S=s, Bs=bs, DS=ds_, D=d, DEPTH=depth, H=h_n, DK=dk, DV=dv, GH=gh,
              G=w_spatial.shape[1], GC=w_spatial.shape[2], EG=w_router_group.shape[-1],
              NE=w_router_expert.shape[-1], DE=w_expert_gate.shape[-1])
    assert dm.DS == ROW_CHUNK and dm.Ts <= ROW_TILE and s % ROW_TILE == 0 and s % MLSTM_CHUNK == 0
    assert dm.EG + dm.NE <= LANES and 2 * h_n <= LANES and dm.Bp + dm.Bs <= 32
    assert dm.off_v % v == 0 and dm.off_o % v == 0 and dm.off_z % (2 * gh) == 0 and d % 256 == 0

    c_all = jnp.concatenate([c_prompt, c_sample, jnp.zeros((32 - bp - bs, d), F32)], axis=0)
    mod = _adaln(c_all, w_ada, b_ada)[:, :bp + bs].reshape(depth, bp + bs, 6, d)
    mod = jnp.pad(mod, ((0, 0), (0, dm.n_mod - bp - bs), (0, 0), (0, 0))).transpose(0, 2, 1, 3)

    xs_rows = jnp.pad(x_sample.reshape(dm.Ts, d), ((0, ROW_TILE - dm.Ts), (0, 0)))
    x = (x_prompt.reshape(dm.Tp, d), xs_rows)
    zeros_c = jnp.zeros((1, bp, h_n, dv, dk), F32)
    zeros_n = jnp.zeros((1, bp, h_n, dk), F32)
    zeros_m = jnp.zeros((1, bp, 1, LANES), F32)
    m_in = _pad_lanes(state_mlstm_m)[:, :, None, :]
    g1 = norm1_g.reshape(depth, 1, d)
    out_c_p, out_n_p, out_m_p, out_c_s, out_n_s, out_m_s, out_v = [], [], [], [], [], [], []

    for l in range(depth):
        if l == 0:
            hb, gates, x = _norm_proj(dm, l, x, g1, mod, w_in, SC1, SH1)
        proj = _in_proj(dm, l, hb, w_in)

        bias = _pad_lanes(jnp.concatenate([b_igate[l], b_fgate[l]])[None, :])
        mhg = mh_norm_g[l][None, :]
        a_p, c_p, n_p, m_p = _mlstm(dm, proj, gates, bias, mhg, zeros_c, zeros_n, zeros_m, 0, sample=False)
        a_s, c_s, n_s, m_s = _mlstm(dm, proj, gates, bias, mhg, state_mlstm_C, state_mlstm_n, m_in, l,
                                    sample=True)
        lng, lnb = sgu_ln_g[l][None, :], sgu_ln_b[l][None, :]
        bst = _pad_lanes(b_spatial[l].T)
        (g_p,) = _sgu(dm, proj, lng, lnb, w_spatial[l], bst, sample=False)
        g_s, v_rows = _sgu(dm, proj, lng, lnb, w_spatial[l], bst, sample=True)

        merged = _merge(dm, a_p, a_s, g_p, g_s, proj, w_branch_a[l].astype(BF16), w_branch_b[l].astype(BF16))
        w_router = _pad_lanes(jnp.concatenate([w_router_group[l], w_router_expert[l]], axis=1))
        b_router = _pad_lanes(jnp.concatenate([b_router_group[l], b_router_expert[l]])[None, :])
        x1, hp, eid, gate = _outproj(dm, l, merged, x, w_out[l].astype(BF16), norm2_g.reshape(depth, 1, d),
                                     mod, w_router, b_router)

        rank, counts = _rank(dm, eid)
        dest, blk_e, n_used = _moe_plan(dm, eid, rank, counts)
        xs = _dispatch(dm, dest, hp, jnp.zeros((dm.cap, d // 2), U32))
        ys = _experts(dm, l, blk_e, n_used, xs, w_expert_gate, w_expert_up, w_expert_down)
        if l + 1 < depth:
            x, hb, gates = _combine(dm, l, dest, ys, x1, gate, mod, g1, w_in)
        else:
            y_p, y_s = _combine(dm, l, dest, ys, x1, gate, mod, final_norm_g[None, :])

        out_c_p.append(c_p)
        out_n_p.append(n_p)
        out_m_p.append(m_p[:, 0, :h_n])
        out_c_s.append(c_s)
        out_n_s.append(n_s)
        out_m_s.append(m_s[:, 0, :h_n])
        out_v.append(v_rows)

    return (y_p.reshape(bp, s, d), y_s[:dm.Ts].reshape(bs, ds_, d),
            jnp.stack(out_c_p), jnp.stack(out_n_p), jnp.stack(out_m_p),
            jnp.stack(out_c_s), jnp.stack(out_n_s), jnp.stack(out_m_s), jnp.stack(out_v))
```

```python
import functools
from typing import NamedTuple

import jax
import jax.numpy as jnp
from jax import lax
from jax.experimental import pallas as pl
from jax.experimental.pallas import tpu as pltpu

F32, BF16, U32, I32 = jnp.float32, jnp.bfloat16, jnp.uint32, jnp.int32
EPS = 1e-6
NEG = -1e30
LANES = 128
ROW_TILE = 512
ROW_CHUNK = 16
MLSTM_CHUNK = 128
MOE_BLOCK = 512
TOP_K = 2
DMA_UNROLL = 8
CHUNK_UNROLL = 2
MIB = 1024 * 1024
SH1, SC1, GT1, SH2, SC2, GT2 = range(6)


class Dims(NamedTuple):
    Bp: int
    S: int
    Bs: int
    DS: int
    D: int
    DEPTH: int
    H: int
    DK: int
    DV: int
    GH: int
    G: int
    GC: int
    EG: int
    NE: int
    DE: int

    @property
    def QK(self):
        return self.H * self.DK

    @property
    def V(self):
        return self.H * self.DV

    @property
    def Tp(self):
        return self.Bp * self.S

    @property
    def Ts(self):
        return self.Bs * self.DS

    @property
    def Tc(self):
        return self.Tp + ROW_TILE

    @property
    def n_tiles(self):
        return self.Tc // ROW_TILE

    @property
    def n_mod(self):
        return self.Bp + ROW_TILE // ROW_CHUNK

    @property
    def off_v(self):
        return 2 * self.QK

    @property
    def off_o(self):
        return 2 * self.QK + self.V

    @property
    def off_z(self):
        return 2 * self.QK + 2 * self.V

    @property
    def off_ga(self):
        return self.off_z + 2 * self.GH

    @property
    def off_gb(self):
        return self.off_ga + self.D

    @property
    def n_main(self):
        return self.off_gb + self.D

    @property
    def cap(self):
        tk = self.Tc * TOP_K
        return -(-(tk + self.NE * (MOE_BLOCK - 1)) // MOE_BLOCK) * MOE_BLOCK


def _cparams(semantics, vmem_mib):
    return pltpu.CompilerParams(dimension_semantics=semantics, vmem_limit_bytes=int(vmem_mib * MIB))


def _pick_tile(n, cap, unit=256):
    best = None
    for t in range(unit, min(n, cap) + 1, unit):
        if n % t == 0:
            best = t
    assert best is not None, (n, cap)
    return best


def _dot(a, b):
    return jnp.dot(a, b, preferred_element_type=F32)


def _split_bf16(x):
    hi = x.astype(BF16)
    lo = (x - hi.astype(F32)).astype(BF16)
    return hi, lo


def _dot3(a_hi, a_lo, w_hi, w_lo):
    return _dot(a_hi, w_hi) + _dot(a_hi, w_lo) + _dot(a_lo, w_hi)


def _gate_weight(w_ref):
    rows, d = w_ref.shape
    w = jnp.concatenate([w_ref[...], jnp.zeros((LANES - rows, d), F32)], axis=0).T
    return _split_bf16(w)


def _rms_mod(x, g, sc, sh):
    y = x * lax.rsqrt(jnp.mean(x * x, axis=-1, keepdims=True) + EPS) * g
    return y * (1.0 + sc) + sh


def _pack_pair(x):
    d2 = x.shape[-1] // 2
    bits = lax.bitcast_convert_type(x.astype(BF16).astype(F32), U32)
    return (bits[:, :d2] >> 16) | (bits[:, d2:] & jnp.uint32(0xFFFF0000))


def _unpack_pair(p):
    lo = lax.bitcast_convert_type(p << 16, F32)
    hi = lax.bitcast_convert_type(p & jnp.uint32(0xFFFF0000), F32)
    return lo, hi


def _adaln_kernel(c_ref, w_ref, b_ref, o_ref):
    c = c_ref[...]
    s = (c * jax.nn.sigmoid(c)).astype(BF16)
    o_ref[...] = _dot(s, w_ref[...].astype(BF16)) + b_ref[...]


def _adaln(c_all, w_ada, b_ada):
    depth, d, n6 = w_ada.shape
    r = c_all.shape[0]
    tn = _pick_tile(n6, 1024)
    return pl.pallas_call(
        _adaln_kernel,
        grid=(depth, n6 // tn),
        in_specs=[pl.BlockSpec((r, d), lambda l, j: (0, 0)),
                  pl.BlockSpec((None, d, tn), lambda l, j: (l, 0, j)),
                  pl.BlockSpec((None, 1, tn), lambda l, j: (l, 0, j))],
        out_specs=pl.BlockSpec((None, r, tn), lambda l, j: (l, 0, j)),
        out_shape=jax.ShapeDtypeStruct((depth, r, n6), F32),
        compiler_params=_cparams(("arbitrary", "arbitrary"), 2 * d * tn * 4 / MIB + 3 * d * tn * 2 / MIB + 8),
        name="adaln",
    )(c_all, w_ada, b_ada.reshape(depth, 1, n6))


def _norm_proj_kernel(*refs, first, dm, k_sc, k_sh):
    if first:
        xp_ref, xs_ref, g_ref, mod_ref, w_ref, h_ref, p_ref, xc_ref, lo_scr = refs
    else:
        x_ref, g_ref, mod_ref, w_ref, h_ref, p_ref, lo_scr = refs
    i = pl.program_id(0)
    is_s = i == dm.n_tiles - 1
    bp = jnp.minimum(i // (dm.S // ROW_TILE), dm.Bp - 1)
    g = g_ref[...]

    def chunk(j, carry):
        rows = pl.ds(pl.multiple_of(j * ROW_CHUNK, ROW_CHUNK), ROW_CHUNK)
        if first:
            x = jnp.where(is_s, xs_ref[rows, :], xp_ref[rows, :])
            xc_ref[rows, :] = x
        else:
            x = x_ref[rows, :]
        idx = jnp.where(is_s, dm.Bp + j, bp)
        h = _rms_mod(x, g, mod_ref[k_sc, pl.ds(idx, 1), :], mod_ref[k_sh, pl.ds(idx, 1), :])
        hi, lo = _split_bf16(h)
        h_ref[rows, :] = hi
        lo_scr[rows, :] = lo
        return carry

    lax.fori_loop(0, ROW_TILE // ROW_CHUNK, chunk, 0, unroll=CHUNK_UNROLL)
    w_hi, w_lo = _gate_weight(w_ref)
    p_ref[...] = _dot3(h_ref[...], lo_scr[...], w_hi, w_lo)


def _norm_proj(dm, l, x_in, g, mod, w_in, k_sc, k_sh):
    d, tc, nt = dm.D, dm.Tc, dm.n_tiles
    first = isinstance(x_in, tuple)
    np_tiles = dm.Tp // ROW_TILE
    tile = lambda i: (i, 0)
    if first:
        x_args = list(x_in)
        x_specs = [pl.BlockSpec((ROW_TILE, d), lambda i: (jnp.minimum(i, np_tiles - 1), 0)),
                   pl.BlockSpec((ROW_TILE, d), lambda i: (0, 0))]
    else:
        x_args = [x_in]
        x_specs = [pl.BlockSpec((ROW_TILE, d), tile)]
    out_shape = [jax.ShapeDtypeStruct((tc, d), BF16), jax.ShapeDtypeStruct((tc, LANES), F32)]
    out_specs = [pl.BlockSpec((ROW_TILE, d), tile), pl.BlockSpec((ROW_TILE, LANES), tile)]
    if first:
        out_shape.append(jax.ShapeDtypeStruct((tc, d), F32))
        out_specs.append(pl.BlockSpec((ROW_TILE, d), tile))
    return pl.pallas_call(
        functools.partial(_norm_proj_kernel, first=first, dm=dm, k_sc=k_sc, k_sh=k_sh),
        grid=(nt,),
        in_specs=x_specs + [pl.BlockSpec((None, 1, d), lambda i: (l, 0, 0)),
                            pl.BlockSpec((None, 6, dm.n_mod, d), lambda i: (l, 0, 0, 0)),
                            pl.BlockSpec((None, 2 * dm.H, d), lambda i: (l, dm.off_z // (2 * dm.H), 0))],
        out_specs=out_specs,
        out_shape=out_shape,
        scratch_shapes=[pltpu.VMEM((ROW_TILE, d), BF16)],
        compiler_params=_cparams(("arbitrary",), ROW_TILE * d * 40 / MIB + 16),
        name="norm_proj",
    )(*x_args, g, mod, w_in)


def _in_proj_kernel(a_ref, w_ref, wn_ref, o_ref, w_scr, *, n_aligned, shift):
    j, i = pl.program_id(0), pl.program_id(1)

    groups = w_ref.shape[0] // LANES

    @pl.when((i == 0) & (j < n_aligned))
    def _cast():
        for g in range(groups):
            w_scr[:, g * LANES:(g + 1) * LANES] = w_ref[g * LANES:(g + 1) * LANES, :].T.astype(BF16)

    @pl.when((i == 0) & (j >= n_aligned))
    def _cast_shifted():
        for g in range(groups):
            if g + 1 < groups:
                rows = w_ref[g * LANES + shift:(g + 1) * LANES + shift, :]
            else:
                rows = jnp.concatenate([w_ref[g * LANES + shift:, :], wn_ref[...]], axis=0)
            w_scr[:, g * LANES:(g + 1) * LANES] = rows.T.astype(BF16)

    o_ref[...] = _dot(a_ref[...], w_scr[...]).astype(o_ref.dtype)


def _in_proj(dm, l, a, w_in_t):
    m, k = a.shape
    off_if, n = dm.off_z, dm.n_main
    shift = 2 * dm.H
    tn = max(t for t in range(LANES, 1024 + 1, LANES) if off_if % t == 0 and n % t == 0)
    tm = _pick_tile(m, 1536)
    vmem = (2 * tm * k * 2 + 2 * k * (tn + shift) * 4 + k * tn * 2 + 2 * tm * tn * 2 + 2 * tm * tn * 4) / MIB + 8
    return pl.pallas_call(
        functools.partial(_in_proj_kernel, n_aligned=off_if // tn, shift=shift),
        grid=(n // tn, m // tm),
        in_specs=[pl.BlockSpec((tm, k), lambda j, i: (i, 0)),
                  pl.BlockSpec((None, tn, k), lambda j, i: (l, j, 0)),
                  pl.BlockSpec((None, shift, k), lambda j, i: (l, (j + 1) * (tn // shift), 0))],
        out_specs=pl.BlockSpec((tm, tn), lambda j, i: (i, j)),
        out_shape=jax.ShapeDtypeStruct((m, n), BF16),
        scratch_shapes=[pltpu.VMEM((k, tn), BF16)],
        compiler_params=_cparams(("arbitrary", "arbitrary"), vmem),
        name="in_proj",
    )(a, w_in_t, w_in_t)


def _pad_rows(x, rows):
    if x.shape[0] == rows:
        return x
    return jnp.concatenate([x, jnp.zeros((rows - x.shape[0], x.shape[1]), x.dtype)], axis=0)


def _mlstm_kernel(*refs, dm, lv, n_seq):
    a_ref = refs[10]
    seq_ok = pl.program_id(0) < n_seq

    @pl.when(seq_ok)
    def _run():
        _mlstm_step(*refs, dm=dm, lv=lv)

    @pl.when(jnp.logical_not(seq_ok))
    def _fill():
        a_ref[...] = jnp.zeros_like(a_ref)


def _mlstm_step(q_ref, k_ref, v_ref, o_ref, gt_ref, bias_ref, mhg_ref, c0_ref, n0_ref, m0_ref,
                a_ref, cout_ref, nout_ref, mout_ref, st_scr, m_scr, *, dm, lv):
    h_n, dk, dv = dm.H, dm.DK, dm.DV
    L = MLSTM_CHUNK
    c = pl.program_id(1)

    @pl.when(c == 0)
    def _init():
        for h in range(h_n):
            st_scr[h, :, :dv] = c0_ref[h].T
            st_scr[h, :, dv:] = jnp.broadcast_to(n0_ref[h:h + 1, :], (LANES, dk)).T
        m_scr[...] = m0_ref[...]

    row = lax.broadcasted_iota(I32, (L, L), 0)
    col = lax.broadcasted_iota(I32, (L, L), 1)
    causal = col <= row
    lane = lax.broadcasted_iota(I32, (L, LANES), 1)
    rowl = lax.broadcasted_iota(I32, (L, LANES), 0)

    xg = _pad_rows(gt_ref[...], L) + bias_ref[...]
    f_log = jnp.minimum(xg, 0.0) - jnp.log1p(jnp.exp(-jnp.abs(xg)))
    gl = jnp.where(lane < h_n, xg, f_log)
    gl = jnp.where(rowl < lv, gl, 0.0)
    tri = causal.astype(BF16)
    g_hi = gl.astype(BF16)
    r1 = gl - g_hi.astype(F32)
    g_mid = r1.astype(BF16)
    g_lo = (r1 - g_mid.astype(F32)).astype(BF16)
    cum = _dot(tri, g_hi) + _dot(tri, g_mid) + _dot(tri, g_lo)
    b_all = pltpu.roll(cum, LANES - h_n, axis=1)
    a_all = jnp.where(rowl < lv, gl - b_all, NEG)

    scale = dk ** -0.5
    kf = _pad_rows(k_ref[...], L).astype(F32) * scale
    kt_all = kf.T.astype(BF16)
    q_all = _pad_rows(q_ref[...], L)
    v_all = _pad_rows(v_ref[...], L)
    a_rows = a_all.T
    m_old = m_scr[...]
    m_new = m_old
    lane1 = lax.broadcasted_iota(I32, (1, LANES), 1)
    ones_blk = jnp.ones((L, LANES), BF16)
    v_blocks = dv // LANES

    for h in range(h_n):
        a_row = a_rows[h:h + 1, :]
        a_rep = jnp.broadcast_to(a_all[:, h:h + 1], (L, LANES))
        b_rep = jnp.broadcast_to(b_all[:, h:h + 1], (L, LANES))
        m_prev = m_old[:, h:h + 1]
        big_m = jnp.maximum(jnp.max(jnp.where(causal, a_row, NEG), axis=1, keepdims=True), m_prev)
        d_w = jnp.where(causal, jnp.exp(jnp.minimum(a_row - big_m, 0.0)), 0.0)
        inter = jnp.exp(m_prev - big_m)
        qh = q_all[:, h * dk:(h + 1) * dk]
        v_parts = [v_all[:, h * dv + j * LANES:h * dv + (j + 1) * LANES] for j in range(v_blocks)]
        kt = kt_all[h * dk:(h + 1) * dk, :]
        sd = (_dot(qh, kt) * d_w).astype(BF16)
        st = st_scr[h]
        x = _dot(sd, jnp.concatenate(v_parts + [ones_blk], axis=1)) + inter * _dot(qh, st.astype(BF16))
        den = x[:, dv:]
        inv = 1.0 / jnp.maximum(jnp.abs(den), jnp.exp(-(b_rep + big_m)))
        sq = jnp.sum(x[:, :dv] * x[:, :dv], axis=1, keepdims=True) * (1.0 / dv)
        sc = (inv * lax.rsqrt(inv * inv * sq + EPS))[:lv]
        for j in range(v_blocks):
            cols = slice(h * dv + j * LANES, h * dv + (j + 1) * LANES)
            gate_o = jax.nn.sigmoid(o_ref[:, cols].astype(F32))
            a_ref[:, cols] = (x[:lv, j * LANES:(j + 1) * LANES] * sc * (mhg_ref[:, cols] * gate_o)).astype(BF16)
        m_last = big_m[L - 1:L, :]
        decay = jnp.exp(m_prev - m_last)
        w_rep = jnp.exp(a_rep - m_last)
        vw = jnp.concatenate([(p.astype(F32) * w_rep).astype(BF16) for p in v_parts] + [w_rep.astype(BF16)], axis=1)
        st_scr[h] = decay * st + _dot(kt, vw)
        m_new = jnp.where(lane1 == h, b_all[L - 1:L, h:h + 1] + m_last, m_new)
    m_scr[...] = m_new

    @pl.when(c == pl.num_programs(1) - 1)
    def _fin():
        for h in range(h_n):
            cout_ref[h] = st_scr[h, :, :dv].T
            nout_ref[h:h + 1, :] = st_scr[h, :, dv:].T[0:1, :]
        mout_ref[...] = m_scr[...]


def _mlstm(dm, proj, gates, bias, mhg, c0, n0, m0, ls, *, sample):
    if sample:
        b_n, n_grid, nc, lv, row0, n_rows = dm.Bs, ROW_TILE // dm.DS, 1, dm.DS, dm.Tp // dm.DS, ROW_TILE
    else:
        b_n, n_grid, nc, lv, row0, n_rows = dm.Bp, dm.Bp, dm.S // MLSTM_CHUNK, MLSTM_CHUNK, 0, dm.Tp
    h_n, dk, dv, qk, v = dm.H, dm.DK, dm.DV, dm.QK, dm.V
    rb = lambda b, c: row0 + b * nc + c
    sq = lambda b: jnp.minimum(b, b_n - 1)
    in_specs = [
        pl.BlockSpec((lv, qk), lambda b, c: (rb(b, c), 0)),
        pl.BlockSpec((lv, qk), lambda b, c: (rb(b, c), 1)),
        pl.BlockSpec((lv, v), lambda b, c: (rb(b, c), dm.off_v // v)),
        pl.BlockSpec((lv, v), lambda b, c: (rb(b, c), dm.off_o // v)),
        pl.BlockSpec((lv, LANES), lambda b, c: (rb(b, c), 0)),
        pl.BlockSpec((1, LANES), lambda b, c: (0, 0)),
        pl.BlockSpec((1, v), lambda b, c: (0, 0)),
        pl.BlockSpec((None, None, h_n, dv, dk), lambda b, c: (ls, sq(b), 0, 0, 0)),
        pl.BlockSpec((None, None, h_n, dk), lambda b, c: (ls, sq(b), 0, 0)),
        pl.BlockSpec((None, None, 1, LANES), lambda b, c: (ls, sq(b), 0, 0)),
    ]
    return pl.pallas_call(
        functools.partial(_mlstm_kernel, dm=dm, lv=lv, n_seq=b_n),
        grid=(n_grid, nc),
        in_specs=in_specs,
        out_specs=[pl.BlockSpec((lv, v), lambda b, c: (b * nc + c, 0)),
                   pl.BlockSpec((None, h_n, dv, dk), lambda b, c: (sq(b), 0, 0, 0)),
                   pl.BlockSpec((None, h_n, dk), lambda b, c: (sq(b), 0, 0)),
                   pl.BlockSpec((None, 1, LANES), lambda b, c: (sq(b), 0, 0))],
        out_shape=[jax.ShapeDtypeStruct((n_rows, v), BF16),
                   jax.ShapeDtypeStruct((b_n, h_n, dv, dk), F32),
                   jax.ShapeDtypeStruct((b_n, h_n, dk), F32),
                   jax.ShapeDtypeStruct((b_n, 1, LANES), F32)],
        scratch_shapes=[pltpu.VMEM((h_n, dk, dv + LANES), F32), pltpu.VMEM((1, LANES), F32)],
        compiler_params=_cparams(("arbitrary", "arbitrary"), 40),
        name="mlstm_sample" if sample else "mlstm_prompt",
    )(proj, proj, proj, proj, gates, bias, mhg, c0, n0, m0)


def _sgu_kernel(*refs, dm, lv, n_seq, emit_v):
    g_ref = refs[5]
    seq_ok = pl.program_id(0) < n_seq

    @pl.when(seq_ok)
    def _run():
        _sgu_step(*refs, dm=dm, lv=lv, emit_v=emit_v)

    @pl.when(jnp.logical_not(seq_ok))
    def _fill():
        g_ref[...] = jnp.zeros_like(g_ref)


def _sgu_step(*refs, dm, lv, emit_v):
    z_ref, lng_ref, lnb_ref, ws_ref, bst_ref, g_ref = refs[:6]
    gh, gch = dm.GH, dm.GH // dm.G
    z = jax.nn.gelu(z_ref[...].astype(F32))
    u, v = z[:, :gh], z[:, gh:]
    xc = v - jnp.mean(v, axis=-1, keepdims=True)
    vn = xc * lax.rsqrt(jnp.mean(xc * xc, axis=-1, keepdims=True) + EPS) * lng_ref[...] + lnb_ref[...]
    if emit_v:
        refs[6][...] = vn
    row = lax.broadcasted_iota(I32, (lv, lv), 0)
    col = lax.broadcasted_iota(I32, (lv, lv), 1)
    vb = vn.astype(BF16)
    for g in range(dm.G):
        w = jnp.where(col <= row, ws_ref[g, :lv, :lv], 0.0).astype(BF16)
        mixed = _dot(w, vb[:, g * gch:(g + 1) * gch]) + bst_ref[:lv, g:g + 1]
        g_ref[:, g * gch:(g + 1) * gch] = (u[:, g * gch:(g + 1) * gch] * mixed).astype(BF16)


def _sgu(dm, proj, lng, lnb, ws, bst, *, sample):
    if sample:
        b_n, n_grid, nc, lv, row0, n_rows = dm.Bs, ROW_TILE // dm.DS, 1, dm.DS, dm.Tp // dm.DS, ROW_TILE
    else:
        b_n, n_grid, nc, lv, row0, n_rows = dm.Bp, dm.Bp, dm.S // dm.GC, dm.GC, 0, dm.Tp
    gh = dm.GH
    rb = lambda b, c: row0 + b * nc + c
    in_specs = [
        pl.BlockSpec((lv, 2 * gh), lambda b, c: (rb(b, c), dm.off_z // (2 * gh))),
        pl.BlockSpec((1, gh), lambda b, c: (0, 0)),
        pl.BlockSpec((1, gh), lambda b, c: (0, 0)),
        pl.BlockSpec((dm.G, dm.GC, dm.GC), lambda b, c: (0, 0, 0)),
        pl.BlockSpec((dm.GC, LANES), lambda b, c: (0, 0)),
    ]
    out_specs = [pl.BlockSpec((lv, gh), lambda b, c: (b * nc + c, 0))]
    out_shape = [jax.ShapeDtypeStruct((n_rows, gh), BF16)]
    if sample:
        out_specs.append(pl.BlockSpec((None, lv, gh), lambda b, c: (jnp.minimum(b, b_n - 1), 0, 0)))
        out_shape.append(jax.ShapeDtypeStruct((b_n, lv, gh), F32))
    return pl.pallas_call(
        functools.partial(_sgu_kernel, dm=dm, lv=lv, n_seq=b_n, emit_v=sample),
        grid=(n_grid, nc),
        in_specs=in_specs,
        out_specs=out_specs,
        out_shape=out_shape,
        compiler_params=_cparams(("arbitrary", "arbitrary"), 24),
        name="sgu_sample" if sample else "sgu_prompt",
    )(proj, lng, lnb, ws, bst)


def _merge_kernel(ap_ref, as_ref, gp_ref, gs_ref, ga_ref, gb_ref, wa_ref, wb_ref, o_ref):
    is_s = pl.program_id(1) == pl.num_programs(1) - 1
    pa = _dot(jnp.where(is_s, as_ref[...], ap_ref[...]), wa_ref[...])
    pb = _dot(jnp.where(is_s, gs_ref[...], gp_ref[...]), wb_ref[...])
    o_ref[...] = (jax.nn.sigmoid(ga_ref[...].astype(F32)) * pa
                  + jax.nn.sigmoid(gb_ref[...].astype(F32)) * pb).astype(o_ref.dtype)


def _merge(dm, a_p, a_s, g_p, g_s, proj, wa, wb):
    tc, d, v, gh = dm.Tc, dm.D, dm.V, dm.GH
    tm = ROW_TILE
    tn = _pick_tile(d, 1024)
    np_tiles = dm.Tp // tm
    prompt = lambda j, i: (jnp.minimum(i, np_tiles - 1), 0)
    const = lambda j, i: (0, 0)
    blocks = 2 * tm * (v + gh) * 2 + 3 * tm * tn * 2 + (v + gh) * tn * 2
    return pl.pallas_call(
        _merge_kernel,
        grid=(d // tn, tc // tm),
        in_specs=[pl.BlockSpec((tm, v), prompt),
                  pl.BlockSpec((tm, v), const),
                  pl.BlockSpec((tm, gh), prompt),
                  pl.BlockSpec((tm, gh), const),
                  pl.BlockSpec((tm, tn), lambda j, i: (i, dm.off_ga // tn + j)),
                  pl.BlockSpec((tm, tn), lambda j, i: (i, dm.off_gb // tn + j)),
                  pl.BlockSpec((v, tn), lambda j, i: (0, j)),
                  pl.BlockSpec((gh, tn), lambda j, i: (0, j))],
        out_specs=pl.BlockSpec((tm, tn), lambda j, i: (i, j)),
        out_shape=jax.ShapeDtypeStruct((tc, d), BF16),
        compiler_params=_cparams(("arbitrary", "arbitrary"), (2 * blocks + 6 * tm * tn * 4) / MIB + 8),
        name="merge",
    )(a_p, a_s, g_p, g_s, proj, proj, wa, wb)


def _route(logits, eg, ne):
    epg = ne // eg
    lane = lax.broadcasted_iota(I32, logits.shape, 1)
    gmask = lane < eg
    gmax = jnp.max(jnp.where(gmask, logits, NEG), axis=1, keepdims=True)
    gexp = jnp.where(gmask, jnp.exp(jnp.minimum(logits - gmax, 0.0)), 0.0)
    pg = gexp / jnp.sum(gexp, axis=1, keepdims=True)
    p_grp = jnp.max(pg, axis=1, keepdims=True)
    g_sel = jnp.min(jnp.where(gmask & (pg == p_grp), lane, LANES), axis=1, keepdims=True)
    lo = eg + g_sel * epg
    emask = (lane >= lo) & (lane < lo + epg)
    emax = jnp.max(jnp.where(emask, logits, NEG), axis=1, keepdims=True)
    eexp = jnp.where(emask, jnp.exp(jnp.minimum(logits - emax, 0.0)), 0.0)
    pe = eexp / jnp.sum(eexp, axis=1, keepdims=True)
    p1 = jnp.max(jnp.where(emask, pe, -1.0), axis=1, keepdims=True)
    i1 = jnp.min(jnp.where(emask & (pe == p1), lane, LANES), axis=1, keepdims=True)
    rest = emask & (lane != i1)
    p2 = jnp.max(jnp.where(rest, pe, -1.0), axis=1, keepdims=True)
    i2 = jnp.min(jnp.where(rest & (pe == p2), lane, LANES), axis=1, keepdims=True)
    psum = p1 + p2
    eid = jnp.where(lane == 0, i1 - eg, jnp.where(lane == 1, i2 - eg, 0))
    gate = jnp.where(lane == 0, p_grp * (p1 / psum), jnp.where(lane == 1, p_grp * (p2 / psum), 0.0))
    return eid, gate


def _outproj_kernel(m_ref, x_ref, w_ref, g_ref, mod_ref, wr_ref, rb_ref,
                    x1_ref, hp_ref, eid_ref, gate_ref, acc_scr, hi_scr, lo_scr, *, dm):
    i = pl.program_id(0)
    is_s = i == dm.n_tiles - 1
    bp = jnp.minimum(i // (dm.S // ROW_TILE), dm.Bp - 1)
    acc_scr[...] = _dot(m_ref[...], w_ref[...])
    g = g_ref[...]

    def chunk(j, carry):
        rows = pl.ds(pl.multiple_of(j * ROW_CHUNK, ROW_CHUNK), ROW_CHUNK)
        idx = jnp.where(is_s, dm.Bp + j, bp)
        x1 = x_ref[rows, :] + mod_ref[GT1, pl.ds(idx, 1), :] * acc_scr[rows, :]
        x1_ref[rows, :] = x1
        h = _rms_mod(x1, g, mod_ref[SC2, pl.ds(idx, 1), :], mod_ref[SH2, pl.ds(idx, 1), :])
        hi, lo = _split_bf16(h)
        hi_scr[rows, :] = hi
        lo_scr[rows, :] = lo
        hp_ref[rows, :] = _pack_pair(h)
        return carry

    lax.fori_loop(0, ROW_TILE // ROW_CHUNK, chunk, 0, unroll=CHUNK_UNROLL)
    w_hi, w_lo = _split_bf16(wr_ref[...])
    logits = _dot3(hi_scr[...], lo_scr[...], w_hi, w_lo) + rb_ref[...]
    eid, gate = _route(logits, dm.EG, dm.NE)
    eid_ref[...] = eid
    gate_ref[...] = gate


def _outproj(dm, l, merged, x, w_out, g2, mod, w_router, b_router):
    d, tc = dm.D, dm.Tc
    tile = lambda i: (i, 0)
    const = lambda i: (0, 0)
    return pl.pallas_call(
        functools.partial(_outproj_kernel, dm=dm),
        grid=(dm.n_tiles,),
        in_specs=[pl.BlockSpec((ROW_TILE, d), tile),
                  pl.BlockSpec((ROW_TILE, d), tile),
                  pl.BlockSpec((d, d), const),
                  pl.BlockSpec((None, 1, d), lambda i: (l, 0, 0)),
                  pl.BlockSpec((None, 6, dm.n_mod, d), lambda i: (l, 0, 0, 0)),
                  pl.BlockSpec((d, LANES), const),
                  pl.BlockSpec((1, LANES), const)],
        out_specs=[pl.BlockSpec((ROW_TILE, d), tile),
                   pl.BlockSpec((ROW_TILE, d // 2), tile),
                   pl.BlockSpec((ROW_TILE, LANES), tile),
                   pl.BlockSpec((ROW_TILE, LANES), tile)],
        out_shape=[jax.ShapeDtypeStruct((tc, d), F32),
                   jax.ShapeDtypeStruct((tc, d // 2), U32),
                   jax.ShapeDtypeStruct((tc, LANES), I32),
                   jax.ShapeDtypeStruct((tc, LANES), F32)],
        scratch_shapes=[pltpu.VMEM((ROW_TILE, d), F32), pltpu.VMEM((ROW_TILE, d), BF16),
                        pltpu.VMEM((ROW_TILE, d), BF16)],
        compiler_params=_cparams(("arbitrary",), (4 * d * d + ROW_TILE * d * 40) / MIB + 12),
        name="out_proj",
    )(merged, x, w_out, g2, mod, w_router, b_router)


def _rank_kernel(eid_ref, rank_ref, cnt_ref, run_scr):
    i = pl.program_id(0)

    @pl.when(i == 0)
    def _init():
        run_scr[...] = jnp.zeros_like(run_scr)

    eid = eid_ref[...]
    lane = lax.broadcasted_iota(I32, eid.shape, 1)
    e0 = lane == eid[:, 0:1]
    e1 = lane == eid[:, 1:2]
    hot = (e0 | e1).astype(BF16)
    n = eid.shape[0]
    strict = (lax.broadcasted_iota(I32, (n, n), 1) < lax.broadcasted_iota(I32, (n, n), 0)).astype(BF16)
    before = _dot(strict, hot) + run_scr[...]
    r0 = jnp.sum(jnp.where(e0, before, 0.0), axis=1, keepdims=True)
    r1 = jnp.sum(jnp.where(e1, before, 0.0), axis=1, keepdims=True)
    rank_ref[...] = jnp.where(lane == 0, r0, jnp.where(lane == 1, r1, 0.0)).astype(I32)
    run_scr[...] = run_scr[...] + jnp.sum(hot.astype(F32), axis=0, keepdims=True)
    cnt_ref[...] = run_scr[...].astype(I32)


def _rank(dm, eid):
    tile = lambda i: (i, 0)
    return pl.pallas_call(
        _rank_kernel,
        grid=(dm.n_tiles,),
        in_specs=[pl.BlockSpec((ROW_TILE, LANES), tile)],
        out_specs=[pl.BlockSpec((ROW_TILE, LANES), tile), pl.BlockSpec((1, LANES), lambda i: (0, 0))],
        out_shape=[jax.ShapeDtypeStruct((dm.Tc, LANES), I32), jax.ShapeDtypeStruct((1, LANES), I32)],
        scratch_shapes=[pltpu.VMEM((1, LANES), F32)],
        compiler_params=_cparams(("arbitrary",), 16),
        name="moe_rank",
    )(eid)


def _dispatch_kernel(tail_ref, dest_ref, h_ref, xs_ref, zero_scr, sem):
    @pl.when(pl.program_id(0) == 0)
    def _clear_tails():
        zero_scr[...] = jnp.zeros_like(zero_scr)
        n_e = tail_ref.shape[0]

        def tail_copy(e):
            start = pl.multiple_of(tail_ref[e], MOE_BLOCK)
            return pltpu.make_async_copy(zero_scr, xs_ref.at[pl.ds(start, MOE_BLOCK)], sem)

        for e in range(n_e):
            tail_copy(e).start()
        for e in range(n_e):
            tail_copy(e).wait()

    def issue(r, carry):
        for k in range(TOP_K):
            pltpu.make_async_copy(h_ref.at[pl.ds(r, 1)], xs_ref.at[pl.ds(dest_ref[TOP_K * r + k], 1)],
                                  sem).start(priority=k)
        return carry

    lax.fori_loop(0, ROW_TILE, issue, 0, unroll=DMA_UNROLL)
    for k in range(TOP_K):
        pltpu.make_async_copy(h_ref, xs_ref.at[pl.ds(0, ROW_TILE)], sem).wait()


def _dispatch(dm, tails, dest, hp):
    d2 = dm.D // 2
    return pl.pallas_call(
        _dispatch_kernel,
        grid_spec=pltpu.PrefetchScalarGridSpec(
            num_scalar_prefetch=1,
            grid=(dm.n_tiles,),
            in_specs=[pl.BlockSpec((ROW_TILE * TOP_K,), lambda i, t: (i,), memory_space=pltpu.SMEM),
                      pl.BlockSpec((ROW_TILE, d2), lambda i, t: (i, 0))],
            out_specs=pl.BlockSpec(memory_space=pl.ANY),
            scratch_shapes=[pltpu.VMEM((MOE_BLOCK, d2), U32), pltpu.SemaphoreType.DMA]),
        out_shape=jax.ShapeDtypeStruct((dm.cap + dm.NE * MOE_BLOCK, d2), U32),
        compiler_params=_cparams(("arbitrary",), 16),
        name="moe_dispatch",
    )(tails, dest, hp)


def _expert_kernel(be_ref, nu_ref, xs_ref, wg_ref, wu_ref, wd_ref, ys_ref, wg_s, wu_s, wd_s):
    j = pl.program_id(0)
    changed = (j == 0) | (be_ref[j] != be_ref[jnp.maximum(j - 1, 0)])

    @pl.when(changed)
    def _cast():
        wg_s[...] = wg_ref[...].astype(BF16)
        wu_s[...] = wu_ref[...].astype(BF16)
        wd_s[...] = wd_ref[...].astype(BF16)

    @pl.when(j < nu_ref[0])
    def _compute():
        d2 = xs_ref.shape[1]
        lo, hi = _unpack_pair(xs_ref[...])
        lo, hi = lo.astype(BF16), hi.astype(BF16)
        hg = _dot(lo, wg_s[:d2, :]) + _dot(hi, wg_s[d2:, :])
        hu = _dot(lo, wu_s[:d2, :]) + _dot(hi, wu_s[d2:, :])
        act = (hg * jax.nn.sigmoid(hg) * hu).astype(BF16)
        ys_ref[...] = _pack_pair(_dot(act, wd_s[...]))


def _experts(dm, l, blk_e, n_used, xs, w_gate, w_up, w_down):
    d, de, d2 = dm.D, dm.DE, dm.D // 2
    nb = dm.cap // MOE_BLOCK
    wmap = lambda j, be, nu: (l, be[j], 0, 0)
    rows = lambda j, be, nu: (jnp.minimum(j, nu[0] - 1), 0)
    return pl.pallas_call(
        _expert_kernel,
        grid_spec=pltpu.PrefetchScalarGridSpec(
            num_scalar_prefetch=2,
            grid=(nb,),
            in_specs=[pl.BlockSpec((MOE_BLOCK, d2), rows),
                      pl.BlockSpec((None, None, d, de), wmap),
                      pl.BlockSpec((None, None, d, de), wmap),
                      pl.BlockSpec((None, None, de, d), wmap)],
            out_specs=pl.BlockSpec((MOE_BLOCK, d2), rows),
            scratch_shapes=[pltpu.VMEM((d, de), BF16), pltpu.VMEM((d, de), BF16), pltpu.VMEM((de, d), BF16)]),
        out_shape=jax.ShapeDtypeStruct((dm.cap, d2), U32),
        compiler_params=_cparams(("arbitrary",), (3 * d * de * (2 * 4 + 2) + MOE_BLOCK * d * 24) / MIB + 8),
        name="moe_experts",
    )(blk_e, n_used, xs, w_gate, w_up, w_down)


def _combine_kernel(*refs, dm, last):
    if last:
        dest_ref, destn_ref, ys_ref, x_ref, gate_ref, mod_ref, g_ref, yp_ref, ysm_ref, ybuf, y_scr, sem = refs
    else:
        (dest_ref, destn_ref, ys_ref, x_ref, gate_ref, mod_ref, g_ref, modn_ref, w_ref,
         o_ref, h_ref, p_ref, ybuf, lo_scr, sem) = refs
    i = pl.program_id(0)
    is_s = i == dm.n_tiles - 1
    bp = jnp.minimum(i // (dm.S // ROW_TILE), dm.Bp - 1)
    d2 = dm.D // 2
    g = g_ref[...]
    slot = i % 2

    def issue(idx_ref, to_slot, r0, n):
        for r in range(n):
            for k in range(TOP_K):
                pltpu.make_async_copy(ys_ref.at[pl.ds(idx_ref[TOP_K * (r0 + r) + k], 1)],
                                      ybuf.at[to_slot, k, pl.ds(r0 + r, 1)], sem.at[to_slot]).start(priority=k)

    def wait_slot(s):
        for k in range(TOP_K):
            pltpu.make_async_copy(ys_ref.at[pl.ds(0, ROW_TILE)], ybuf.at[s, k], sem.at[s]).wait()

    @pl.when(i == 0)
    def _first_tile():
        lax.fori_loop(0, ROW_TILE // DMA_UNROLL,
                      lambda t, c: (issue(dest_ref, 0, t * DMA_UNROLL, DMA_UNROLL), c)[1], 0)

    wait_slot(slot)

    def chunk(j, carry):
        rows = pl.ds(pl.multiple_of(j * ROW_CHUNK, ROW_CHUNK), ROW_CHUNK)
        issue(destn_ref, 1 - slot, j * ROW_CHUNK, ROW_CHUNK)
        idx = jnp.where(is_s, dm.Bp + j, bp)
        gt = mod_ref[GT2, pl.ds(idx, 1), :]
        gates = gate_ref[rows, :]
        g0, g1 = gates[:, 0:1], gates[:, 1:2]
        lo0, hi0 = _unpack_pair(ybuf[slot, 0, rows, :])
        lo1, hi1 = _unpack_pair(ybuf[slot, 1, rows, :])
        x_lo = x_ref[rows, :d2] + gt[:, :d2] * (g0 * lo0 + g1 * lo1)
        x_hi = x_ref[rows, d2:] + gt[:, d2:] * (g0 * hi0 + g1 * hi1)
        ms = (jnp.sum(x_lo * x_lo, axis=-1, keepdims=True)
              + jnp.sum(x_hi * x_hi, axis=-1, keepdims=True)) * (1.0 / dm.D)
        rs = lax.rsqrt(ms + EPS)
        if last:
            y_scr[rows, :d2] = x_lo * rs * g[:, :d2]
            y_scr[rows, d2:] = x_hi * rs * g[:, d2:]
        else:
            o_ref[rows, :d2] = x_lo
            o_ref[rows, d2:] = x_hi
            sc = modn_ref[SC1, pl.ds(idx, 1), :]
            sh = modn_ref[SH1, pl.ds(idx, 1), :]
            for half, xh in ((slice(0, d2), x_lo), (slice(d2, dm.D), x_hi)):
                h = xh * rs * g[:, half] * (1.0 + sc[:, half]) + sh[:, half]
                hi, lo = _split_bf16(h)
                h_ref[rows, half] = hi
                lo_scr[rows, half] = lo
        return carry

    lax.fori_loop(0, ROW_TILE // ROW_CHUNK, chunk, 0, unroll=CHUNK_UNROLL)

    @pl.when(is_s)
    def _drain():
        wait_slot(1 - slot)

    if last:
        @pl.when(is_s)
        def _sample_rows():
            ysm_ref[...] = y_scr[...]

        @pl.when(jnp.logical_not(is_s))
        def _prompt_rows():
            yp_ref[...] = y_scr[...]
    else:
        w_hi, w_lo = _gate_weight(w_ref)
        p_ref[...] = _dot3(h_ref[...], lo_scr[...], w_hi, w_lo)


def _combine(dm, l, dest, ys, x1, gate, mod, g_next, w_in=None):
    d = dm.D
    last = w_in is None
    tile = lambda i: (i, 0)
    np_tiles = dm.Tp // ROW_TILE
    in_specs = [pl.BlockSpec((ROW_TILE * TOP_K,), lambda i: (i,), memory_space=pltpu.SMEM),
                pl.BlockSpec((ROW_TILE * TOP_K,), lambda i: (jnp.minimum(i + 1, dm.n_tiles - 1),),
                             memory_space=pltpu.SMEM),
                pl.BlockSpec(memory_space=pl.ANY),
                pl.BlockSpec((ROW_TILE, d), tile),
                pl.BlockSpec((ROW_TILE, LANES), tile),
                pl.BlockSpec((None, 6, dm.n_mod, d), lambda i: (l, 0, 0, 0))]
    scratch = [pltpu.VMEM((2, TOP_K, ROW_TILE, d // 2), U32)]
    if last:
        args = (dest, dest, ys, x1, gate, mod, g_next)
        in_specs.append(pl.BlockSpec((1, d), lambda i: (0, 0)))
        out_specs = [pl.BlockSpec((ROW_TILE, d), lambda i: (jnp.minimum(i, np_tiles - 1), 0)),
                     pl.BlockSpec((ROW_TILE, d), lambda i: (0, 0))]
        out_shape = [jax.ShapeDtypeStruct((dm.Tp, d), F32), jax.ShapeDtypeStruct((ROW_TILE, d), F32)]
        scratch.append(pltpu.VMEM((ROW_TILE, d), F32))
    else:
        args = (dest, dest, ys, x1, gate, mod, g_next, mod, w_in)
        in_specs += [pl.BlockSpec((None, 1, d), lambda i: (l + 1, 0, 0)),
                     pl.BlockSpec((None, 6, dm.n_mod, d), lambda i: (l + 1, 0, 0, 0)),
                     pl.BlockSpec((None, 2 * dm.H, d), lambda i: (l + 1, dm.off_z // (2 * dm.H), 0))]
        out_specs = [pl.BlockSpec((ROW_TILE, d), tile), pl.BlockSpec((ROW_TILE, d), tile),
                     pl.BlockSpec((ROW_TILE, LANES), tile)]
        out_shape = [jax.ShapeDtypeStruct((dm.Tc, d), F32), jax.ShapeDtypeStruct((dm.Tc, d), BF16),
                     jax.ShapeDtypeStruct((dm.Tc, LANES), F32)]
        scratch.append(pltpu.VMEM((ROW_TILE, d), BF16))
    scratch.append(pltpu.SemaphoreType.DMA((2,)))
    return pl.pallas_call(
        functools.partial(_combine_kernel, dm=dm, last=last),
        grid=(dm.n_tiles,),
        in_specs=in_specs,
        out_specs=out_specs,
        out_shape=out_shape,
        scratch_shapes=scratch,
        compiler_params=_cparams(("arbitrary",), ROW_TILE * d * 40 / MIB + 16),
        name="moe_combine_last" if last else "moe_combine",
    )(*args)


def _pad_lanes(x, n=LANES):
    return jnp.pad(x, [(0, 0)] * (x.ndim - 1) + [(0, n - x.shape[-1])])


def _moe_plan(dm, eid, rank, counts):
    counts = counts[0, :dm.NE]
    padded = (counts + MOE_BLOCK - 1) // MOE_BLOCK * MOE_BLOCK
    pad_end = jnp.cumsum(padded)
    pad_start = pad_end - padded
    dest = (pad_start[eid[:, :TOP_K]] + rank[:, :TOP_K]).reshape(-1).astype(I32)
    nb = dm.cap // MOE_BLOCK
    n_used = (pad_end[-1] // MOE_BLOCK).astype(I32)
    starts = jnp.minimum(jnp.arange(nb, dtype=I32), n_used - 1) * MOE_BLOCK
    blk_e = jnp.minimum(jnp.sum(pad_end[None, :] <= starts[:, None], axis=1), dm.NE - 1).astype(I32)
    spare = dm.cap + jnp.arange(dm.NE, dtype=I32) * MOE_BLOCK
    tails = jnp.where(padded > 0, pad_end - MOE_BLOCK, spare).astype(I32)
    return dest, blk_e, n_used.reshape(1), tails


def kernel(x_prompt, x_sample, c_prompt, c_sample, state_mlstm_C, state_mlstm_n, state_mlstm_m, w_ada, b_ada, norm1_g, norm2_g, w_in, b_igate, b_fgate, mh_norm_g, sgu_ln_g, sgu_ln_b, w_spatial, b_spatial, w_branch_a, w_branch_b, w_out, w_router_group, b_router_group, w_router_expert, b_router_expert, w_expert_gate, w_expert_up, w_expert_down, final_norm_g):
    bp, s, d = x_prompt.shape
    bs, ds_, _ = x_sample.shape
    depth = w_in.shape[0]
    h_n = b_igate.shape[1]
    dv, dk = state_mlstm_C.shape[-2:]
    qk, v = h_n * dk, h_n * dv
    gh = (w_in.shape[2] - (2 * qk + 2 * v + 2 * h_n + 2 * d)) // 2
    dm = Dims(Bp=bp, S=s, Bs=bs, DS=ds_, D=d, DEPTH=depth, H=h_n, DK=dk, DV=dv, GH=gh,
              G=w_spatial.shape[1], GC=w_spatial.shape[2], EG=w_router_group.shape[-1],
              NE=w_router_expert.shape[-1], DE=w_expert_gate.shape[-1])
    assert dm.DS == ROW_CHUNK and dm.Ts <= ROW_TILE and s % ROW_TILE == 0 and s % MLSTM_CHUNK == 0
    assert dm.EG + dm.NE <= LANES and 2 * h_n <= LANES and (2 * h_n) % 8 == 0 and dm.Bp + dm.Bs <= 32
    w_in = jnp.swapaxes(w_in, 1, 2)
    assert dm.off_v % v == 0 and dm.off_o % v == 0 and dm.off_z % (2 * gh) == 0 and d % 256 == 0

    c_all = jnp.concatenate([c_prompt, c_sample, jnp.zeros((32 - bp - bs, d), F32)], axis=0)
    mod = _adaln(c_all, w_ada, b_ada)[:, :bp + bs].reshape(depth, bp + bs, 6, d)
    mod = jnp.pad(mod, ((0, 0), (0, dm.n_mod - bp - bs), (0, 0), (0, 0))).transpose(0, 2, 1, 3)

    xs_rows = jnp.pad(x_sample.reshape(dm.Ts, d), ((0, ROW_TILE - dm.Ts), (0, 0)))
    x = (x_prompt.reshape(dm.Tp, d), xs_rows)
    zeros_c = jnp.zeros((1, bp, h_n, dv, dk), F32)
    zeros_n = jnp.zeros((1, bp, h_n, dk), F32)
    zeros_m = jnp.zeros((1, bp, 1, LANES), F32)
    m_in = _pad_lanes(state_mlstm_m)[:, :, None, :]
    g1 = norm1_g.reshape(depth, 1, d)
    out_c_p, out_n_p, out_m_p, out_c_s, out_n_s, out_m_s, out_v = [], [], [], [], [], [], []

    for l in range(depth):
        if l == 0:
            hb, gates, x = _norm_proj(dm, l, x, g1, mod, w_in, SC1, SH1)
        proj = _in_proj(dm, l, hb, w_in)

        bias = _pad_lanes(jnp.concatenate([b_igate[l], b_fgate[l]])[None, :])
        mhg = mh_norm_g[l][None, :]
        a_p, c_p, n_p, m_p = _mlstm(dm, proj, gates, bias, mhg, zeros_c, zeros_n, zeros_m, 0, sample=False)
        a_s, c_s, n_s, m_s = _mlstm(dm, proj, gates, bias, mhg, state_mlstm_C, state_mlstm_n, m_in, l,
                                    sample=True)
        lng, lnb = sgu_ln_g[l][None, :], sgu_ln_b[l][None, :]
        bst = _pad_lanes(b_spatial[l].T)
        (g_p,) = _sgu(dm, proj, lng, lnb, w_spatial[l], bst, sample=False)
        g_s, v_rows = _sgu(dm, proj, lng, lnb, w_spatial[l], bst, sample=True)

        merged = _merge(dm, a_p, a_s, g_p, g_s, proj, w_branch_a[l].astype(BF16), w_branch_b[l].astype(BF16))
        w_router = _pad_lanes(jnp.concatenate([w_router_group[l], w_router_expert[l]], axis=1))
        b_router = _pad_lanes(jnp.concatenate([b_router_group[l], b_router_expert[l]])[None, :])
        x1, hp, eid, gate = _outproj(dm, l, merged, x, w_out[l].astype(BF16), norm2_g.reshape(depth, 1, d),
                                     mod, w_router, b_router)

        rank, counts = _rank(dm, eid)
        dest, blk_e, n_used, tails = _moe_plan(dm, eid, rank, counts)
        xs = _dispatch(dm, tails, dest, hp)
        ys = _experts(dm, l, blk_e, n_used, xs, w_expert_gate, w_expert_up, w_expert_down)
        if l + 1 < depth:
            x, hb, gates = _combine(dm, l, dest, ys, x1, gate, mod, g1, w_in)
        else:
            y_p, y_s = _combine(dm, l, dest, ys, x1, gate, mod, final_norm_g[None, :])

        out_c_p.append(c_p)
        out_n_p.append(n_p)
        out_m_p.append(m_p[:, 0, :h_n])
        out_c_s.append(c_s)
        out_n_s.append(n_s)
        out_m_s.append(m_s[:, 0, :h_n])
        out_v.append(v_rows)

    return (y_p.reshape(bp, s, d), y_s[:dm.Ts].reshape(bs, ds_, d),
            jnp.stack(out_c_p), jnp.stack(out_n_p), jnp.stack(out_m_p),
            jnp.stack(out_c_s), jnp.stack(out_n_s), jnp.stack(out_m_s), jnp.stack(out_v))
```

```python
import functools
from typing import NamedTuple

import jax
import jax.numpy as jnp
from jax import lax
from jax.experimental import pallas as pl
from jax.experimental.pallas import tpu as pltpu

F32, BF16, U32, I32 = jnp.float32, jnp.bfloat16, jnp.uint32, jnp.int32
EPS = 1e-6
NEG = -1e30
LANES = 128
ROW_TILE = 512
ROW_CHUNK = 16
MLSTM_CHUNK = 128
MOE_BLOCK = 512
TOP_K = 2
DMA_UNROLL = 8
CHUNK_UNROLL = 2
MIB = 1024 * 1024
SH1, SC1, GT1, SH2, SC2, GT2 = range(6)


class Dims(NamedTuple):
    Bp: int
    S: int
    Bs: int
    DS: int
    D: int
    DEPTH: int
    H: int
    DK: int
    DV: int
    GH: int
    G: int
    GC: int
    EG: int
    NE: int
    DE: int

    @property
    def QK(self):
        return self.H * self.DK

    @property
    def V(self):
        return self.H * self.DV

    @property
    def Tp(self):
        return self.Bp * self.S

    @property
    def Ts(self):
        return self.Bs * self.DS

    @property
    def Tc(self):
        return self.Tp + ROW_TILE

    @property
    def n_tiles(self):
        return self.Tc // ROW_TILE

    @property
    def n_mod(self):
        return self.Bp + ROW_TILE // ROW_CHUNK

    @property
    def off_v(self):
        return 2 * self.QK

    @property
    def off_o(self):
        return 2 * self.QK + self.V

    @property
    def off_z(self):
        return 2 * self.QK + 2 * self.V

    @property
    def off_ga(self):
        return self.off_z + 2 * self.GH

    @property
    def off_gb(self):
        return self.off_ga + self.D

    @property
    def n_main(self):
        return self.off_gb + self.D

    @property
    def cap(self):
        tk = self.Tc * TOP_K
        return -(-(tk + self.NE * (MOE_BLOCK - 1)) // MOE_BLOCK) * MOE_BLOCK


def _cparams(semantics, vmem_mib):
    return pltpu.CompilerParams(dimension_semantics=semantics, vmem_limit_bytes=int(vmem_mib * MIB))


def _pick_tile(n, cap, unit=256):
    best = None
    for t in range(unit, min(n, cap) + 1, unit):
        if n % t == 0:
            best = t
    assert best is not None, (n, cap)
    return best


def _dot(a, b):
    return jnp.dot(a, b, preferred_element_type=F32)


def _split_bf16(x):
    hi = x.astype(BF16)
    lo = (x - hi.astype(F32)).astype(BF16)
    return hi, lo


def _dot3(a_hi, a_lo, w_hi, w_lo):
    return _dot(a_hi, w_hi) + _dot(a_hi, w_lo) + _dot(a_lo, w_hi)


def _gate_weight(w_ref):
    rows, d = w_ref.shape
    w = jnp.concatenate([w_ref[...], jnp.zeros((LANES - rows, d), F32)], axis=0).T
    return _split_bf16(w)


def _rms_mod(x, g, sc, sh):
    y = x * lax.rsqrt(jnp.mean(x * x, axis=-1, keepdims=True) + EPS) * g
    return y * (1.0 + sc) + sh


def _pack_pair(x):
    d2 = x.shape[-1] // 2
    bits = lax.bitcast_convert_type(x.astype(BF16).astype(F32), U32)
    return (bits[:, :d2] >> 16) | (bits[:, d2:] & jnp.uint32(0xFFFF0000))


def _unpack_pair(p):
    lo = lax.bitcast_convert_type(p << 16, F32)
    hi = lax.bitcast_convert_type(p & jnp.uint32(0xFFFF0000), F32)
    return lo, hi


def _adaln_kernel(c_ref, w_ref, b_ref, o_ref):
    c = c_ref[...]
    s = (c * jax.nn.sigmoid(c)).astype(BF16)
    o_ref[...] = _dot(s, w_ref[...].astype(BF16)) + b_ref[...]


def _adaln(c_all, w_ada, b_ada):
    depth, d, n6 = w_ada.shape
    r = c_all.shape[0]
    tn = _pick_tile(n6, 1024)
    return pl.pallas_call(
        _adaln_kernel,
        grid=(depth, n6 // tn),
        in_specs=[pl.BlockSpec((r, d), lambda l, j: (0, 0)),
                  pl.BlockSpec((None, d, tn), lambda l, j: (l, 0, j)),
                  pl.BlockSpec((None, 1, tn), lambda l, j: (l, 0, j))],
        out_specs=pl.BlockSpec((None, r, tn), lambda l, j: (l, 0, j)),
        out_shape=jax.ShapeDtypeStruct((depth, r, n6), F32),
        compiler_params=_cparams(("arbitrary", "arbitrary"), 2 * d * tn * 4 / MIB + 3 * d * tn * 2 / MIB + 8),
        name="adaln",
    )(c_all, w_ada, b_ada.reshape(depth, 1, n6))


def _norm_proj_kernel(*refs, first, dm, k_sc, k_sh):
    if first:
        xp_ref, xs_ref, g_ref, mod_ref, w_ref, h_ref, p_ref, xc_ref, lo_scr = refs
    else:
        x_ref, g_ref, mod_ref, w_ref, h_ref, p_ref, lo_scr = refs
    i = pl.program_id(0)
    is_s = i == dm.n_tiles - 1
    bp = jnp.minimum(i // (dm.S // ROW_TILE), dm.Bp - 1)
    g = g_ref[...]

    def chunk(j, carry):
        rows = pl.ds(pl.multiple_of(j * ROW_CHUNK, ROW_CHUNK), ROW_CHUNK)
        if first:
            x = jnp.where(is_s, xs_ref[rows, :], xp_ref[rows, :])
            xc_ref[rows, :] = x
        else:
            x = x_ref[rows, :]
        idx = jnp.where(is_s, dm.Bp + j, bp)
        h = _rms_mod(x, g, mod_ref[k_sc, pl.ds(idx, 1), :], mod_ref[k_sh, pl.ds(idx, 1), :])
        hi, lo = _split_bf16(h)
        h_ref[rows, :] = hi
        lo_scr[rows, :] = lo
        return carry

    lax.fori_loop(0, ROW_TILE // ROW_CHUNK, chunk, 0, unroll=CHUNK_UNROLL)
    w_hi, w_lo = _gate_weight(w_ref)
    p_ref[...] = _dot3(h_ref[...], lo_scr[...], w_hi, w_lo)


def _norm_proj(dm, l, x_in, g, mod, w_in, k_sc, k_sh):
    d, tc, nt = dm.D, dm.Tc, dm.n_tiles
    first = isinstance(x_in, tuple)
    np_tiles = dm.Tp // ROW_TILE
    tile = lambda i: (i, 0)
    if first:
        x_args = list(x_in)
        x_specs = [pl.BlockSpec((ROW_TILE, d), lambda i: (jnp.minimum(i, np_tiles - 1), 0)),
                   pl.BlockSpec((ROW_TILE, d), lambda i: (0, 0))]
    else:
        x_args = [x_in]
        x_specs = [pl.BlockSpec((ROW_TILE, d), tile)]
    out_shape = [jax.ShapeDtypeStruct((tc, d), BF16), jax.ShapeDtypeStruct((tc, LANES), F32)]
    out_specs = [pl.BlockSpec((ROW_TILE, d), tile), pl.BlockSpec((ROW_TILE, LANES), tile)]
    if first:
        out_shape.append(jax.ShapeDtypeStruct((tc, d), F32))
        out_specs.append(pl.BlockSpec((ROW_TILE, d), tile))
    return pl.pallas_call(
        functools.partial(_norm_proj_kernel, first=first, dm=dm, k_sc=k_sc, k_sh=k_sh),
        grid=(nt,),
        in_specs=x_specs + [pl.BlockSpec((None, 1, d), lambda i: (l, 0, 0)),
                            pl.BlockSpec((None, 6, dm.n_mod, d), lambda i: (l, 0, 0, 0)),
                            pl.BlockSpec((None, 2 * dm.H, d), lambda i: (l, dm.off_z // (2 * dm.H), 0))],
        out_specs=out_specs,
        out_shape=out_shape,
        scratch_shapes=[pltpu.VMEM((ROW_TILE, d), BF16)],
        compiler_params=_cparams(("arbitrary",), ROW_TILE * d * 40 / MIB + 16),
        name="norm_proj",
    )(*x_args, g, mod, w_in)


def _in_proj_kernel(a_ref, w_ref, wn_ref, o_ref, w_scr, *, n_aligned, shift):
    j, i = pl.program_id(0), pl.program_id(1)

    groups = w_ref.shape[0] // LANES

    @pl.when((i == 0) & (j < n_aligned))
    def _cast():
        for g in range(groups):
            w_scr[:, g * LANES:(g + 1) * LANES] = w_ref[g * LANES:(g + 1) * LANES, :].T.astype(BF16)

    @pl.when((i == 0) & (j >= n_aligned))
    def _cast_shifted():
        for g in range(groups):
            if g + 1 < groups:
                rows = w_ref[g * LANES + shift:(g + 1) * LANES + shift, :]
            else:
                rows = jnp.concatenate([w_ref[g * LANES + shift:, :], wn_ref[...]], axis=0)
            w_scr[:, g * LANES:(g + 1) * LANES] = rows.T.astype(BF16)

    o_ref[...] = _dot(a_ref[...], w_scr[...]).astype(o_ref.dtype)


def _in_proj(dm, l, a, w_in_t):
    m, k = a.shape
    off_if, n = dm.off_z, dm.n_main
    shift = 2 * dm.H
    tn = max(t for t in range(LANES, 1024 + 1, LANES) if off_if % t == 0 and n % t == 0)
    tm = _pick_tile(m, 1536)
    vmem = (2 * tm * k * 2 + 2 * k * (tn + shift) * 4 + k * tn * 2 + 2 * tm * tn * 2 + 2 * tm * tn * 4) / MIB + 8
    return pl.pallas_call(
        functools.partial(_in_proj_kernel, n_aligned=off_if // tn, shift=shift),
        grid=(n // tn, m // tm),
        in_specs=[pl.BlockSpec((tm, k), lambda j, i: (i, 0)),
                  pl.BlockSpec((None, tn, k), lambda j, i: (l, j, 0)),
                  pl.BlockSpec((None, shift, k), lambda j, i: (l, (j + 1) * (tn // shift), 0))],
        out_specs=pl.BlockSpec((tm, tn), lambda j, i: (i, j)),
        out_shape=jax.ShapeDtypeStruct((m, n), BF16),
        scratch_shapes=[pltpu.VMEM((k, tn), BF16)],
        compiler_params=_cparams(("arbitrary", "arbitrary"), vmem),
        name="in_proj",
    )(a, w_in_t, w_in_t)


def _pad_rows(x, rows):
    if x.shape[0] == rows:
        return x
    return jnp.concatenate([x, jnp.zeros((rows - x.shape[0], x.shape[1]), x.dtype)], axis=0)


def _mlstm_kernel(*refs, dm, lv, n_seq):
    a_ref = refs[10]
    seq_ok = pl.program_id(0) < n_seq

    @pl.when(seq_ok)
    def _run():
        _mlstm_step(*refs, dm=dm, lv=lv)

    @pl.when(jnp.logical_not(seq_ok))
    def _fill():
        a_ref[...] = jnp.zeros_like(a_ref)


def _mlstm_step(q_ref, k_ref, v_ref, o_ref, gt_ref, bias_ref, mhg_ref, c0_ref, n0_ref, m0_ref,
                a_ref, cout_ref, nout_ref, mout_ref, st_scr, m_scr, *, dm, lv):
    h_n, dk, dv = dm.H, dm.DK, dm.DV
    L = MLSTM_CHUNK
    c = pl.program_id(1)

    @pl.when(c == 0)
    def _init():
        for h in range(h_n):
            st_scr[h, :, :dv] = c0_ref[h].T
            st_scr[h, :, dv:] = jnp.broadcast_to(n0_ref[h:h + 1, :], (LANES, dk)).T
        m_scr[...] = m0_ref[...]

    row = lax.broadcasted_iota(I32, (L, L), 0)
    col = lax.broadcasted_iota(I32, (L, L), 1)
    causal = col <= row
    lane = lax.broadcasted_iota(I32, (L, LANES), 1)
    rowl = lax.broadcasted_iota(I32, (L, LANES), 0)

    xg = _pad_rows(gt_ref[...], L) + bias_ref[...]
    f_log = jnp.minimum(xg, 0.0) - jnp.log1p(jnp.exp(-jnp.abs(xg)))
    gl = jnp.where(lane < h_n, xg, f_log)
    gl = jnp.where(rowl < lv, gl, 0.0)
    tri = causal.astype(BF16)
    g_hi = gl.astype(BF16)
    r1 = gl - g_hi.astype(F32)
    g_mid = r1.astype(BF16)
    g_lo = (r1 - g_mid.astype(F32)).astype(BF16)
    cum = _dot(tri, g_hi) + _dot(tri, g_mid) + _dot(tri, g_lo)
    b_all = pltpu.roll(cum, LANES - h_n, axis=1)
    a_all = jnp.where(rowl < lv, gl - b_all, NEG)

    scale = dk ** -0.5
    kf = _pad_rows(k_ref[...], L).astype(F32) * scale
    kt_all = kf.T.astype(BF16)
    q_all = _pad_rows(q_ref[...], L)
    v_all = _pad_rows(v_ref[...], L)
    a_rows = a_all.T
    m_old = m_scr[...]
    m_new = m_old
    lane1 = lax.broadcasted_iota(I32, (1, LANES), 1)
    ones_blk = jnp.ones((L, LANES), BF16)
    v_blocks = dv // LANES

    for h in range(h_n):
        a_row = a_rows[h:h + 1, :]
        a_rep = jnp.broadcast_to(a_all[:, h:h + 1], (L, LANES))
        b_rep = jnp.broadcast_to(b_all[:, h:h + 1], (L, LANES))
        m_prev = m_old[:, h:h + 1]
        big_m = jnp.maximum(jnp.max(jnp.where(causal, a_row, NEG), axis=1, keepdims=True), m_prev)
        d_w = jnp.where(causal, jnp.exp(jnp.minimum(a_row - big_m, 0.0)), 0.0)
        inter = jnp.exp(m_prev - big_m)
        qh = q_all[:, h * dk:(h + 1) * dk]
        v_parts = [v_all[:, h * dv + j * LANES:h * dv + (j + 1) * LANES] for j in range(v_blocks)]
        kt = kt_all[h * dk:(h + 1) * dk, :]
        sd = (_dot(qh, kt) * d_w).astype(BF16)
        st = st_scr[h]
        x = _dot(sd, jnp.concatenate(v_parts + [ones_blk], axis=1)) + inter * _dot(qh, st.astype(BF16))
        den = x[:, dv:]
        inv = 1.0 / jnp.maximum(jnp.abs(den), jnp.exp(-(b_rep + big_m)))
        sq = jnp.sum(x[:, :dv] * x[:, :dv], axis=1, keepdims=True) * (1.0 / dv)
        sc = (inv * lax.rsqrt(inv * inv * sq + EPS))[:lv]
        for j in range(v_blocks):
            cols = slice(h * dv + j * LANES, h * dv + (j + 1) * LANES)
            gate_o = jax.nn.sigmoid(o_ref[:, cols].astype(F32))
            a_ref[:, cols] = (x[:lv, j * LANES:(j + 1) * LANES] * sc * (mhg_ref[:, cols] * gate_o)).astype(BF16)
        m_last = big_m[L - 1:L, :]
        decay = jnp.exp(m_prev - m_last)
        w_rep = jnp.exp(a_rep - m_last)
        vw = jnp.concatenate([(p.astype(F32) * w_rep).astype(BF16) for p in v_parts] + [w_rep.astype(BF16)], axis=1)
        st_scr[h] = decay * st + _dot(kt, vw)
        m_new = jnp.where(lane1 == h, b_all[L - 1:L, h:h + 1] + m_last, m_new)
    m_scr[...] = m_new

    @pl.when(c == pl.num_programs(1) - 1)
    def _fin():
        for h in range(h_n):
            cout_ref[h] = st_scr[h, :, :dv].T
            nout_ref[h:h + 1, :] = st_scr[h, :, dv:].T[0:1, :]
        mout_ref[...] = m_scr[...]


def _mlstm(dm, proj, gates, bias, mhg, c0, n0, m0, ls, *, sample):
    if sample:
        b_n, n_grid, nc, lv, row0, n_rows = dm.Bs, ROW_TILE // dm.DS, 1, dm.DS, dm.Tp // dm.DS, ROW_TILE
    else:
        b_n, n_grid, nc, lv, row0, n_rows = dm.Bp, dm.Bp, dm.S // MLSTM_CHUNK, MLSTM_CHUNK, 0, dm.Tp
    h_n, dk, dv, qk, v = dm.H, dm.DK, dm.DV, dm.QK, dm.V
    rb = lambda b, c: row0 + b * nc + c
    sq = lambda b: jnp.minimum(b, b_n - 1)
    in_specs = [
        pl.BlockSpec((lv, qk), lambda b, c: (rb(b, c), 0)),
        pl.BlockSpec((lv, qk), lambda b, c: (rb(b, c), 1)),
        pl.BlockSpec((lv, v), lambda b, c: (rb(b, c), dm.off_v // v)),
        pl.BlockSpec((lv, v), lambda b, c: (rb(b, c), dm.off_o // v)),
        pl.BlockSpec((lv, LANES), lambda b, c: (rb(b, c), 0)),
        pl.BlockSpec((1, LANES), lambda b, c: (0, 0)),
        pl.BlockSpec((1, v), lambda b, c: (0, 0)),
        pl.BlockSpec((None, None, h_n, dv, dk), lambda b, c: (ls, sq(b), 0, 0, 0)),
        pl.BlockSpec((None, None, h_n, dk), lambda b, c: (ls, sq(b), 0, 0)),
        pl.BlockSpec((None, None, 1, LANES), lambda b, c: (ls, sq(b), 0, 0)),
    ]
    return pl.pallas_call(
        functools.partial(_mlstm_kernel, dm=dm, lv=lv, n_seq=b_n),
        grid=(n_grid, nc),
        in_specs=in_specs,
        out_specs=[pl.BlockSpec((lv, v), lambda b, c: (b * nc + c, 0)),
                   pl.BlockSpec((None, h_n, dv, dk), lambda b, c: (sq(b), 0, 0, 0)),
                   pl.BlockSpec((None, h_n, dk), lambda b, c: (sq(b), 0, 0)),
                   pl.BlockSpec((None, 1, LANES), lambda b, c: (sq(b), 0, 0))],
        out_shape=[jax.ShapeDtypeStruct((n_rows, v), BF16),
                   jax.ShapeDtypeStruct((b_n, h_n, dv, dk), F32),
                   jax.ShapeDtypeStruct((b_n, h_n, dk), F32),
                   jax.ShapeDtypeStruct((b_n, 1, LANES), F32)],
        scratch_shapes=[pltpu.VMEM((h_n, dk, dv + LANES), F32), pltpu.VMEM((1, LANES), F32)],
        compiler_params=_cparams(("arbitrary", "arbitrary"), 40),
        name="mlstm_sample" if sample else "mlstm_prompt",
    )(proj, proj, proj, proj, gates, bias, mhg, c0, n0, m0)


def _sgu_kernel(*refs, dm, lv, n_seq, emit_v):
    g_ref = refs[5]
    seq_ok = pl.program_id(0) < n_seq

    @pl.when(seq_ok)
    def _run():
        _sgu_step(*refs, dm=dm, lv=lv, emit_v=emit_v)

    @pl.when(jnp.logical_not(seq_ok))
    def _fill():
        g_ref[...] = jnp.zeros_like(g_ref)


def _sgu_step(*refs, dm, lv, emit_v):
    z_ref, lng_ref, lnb_ref, ws_ref, bst_ref, g_ref = refs[:6]
    gh, gch = dm.GH, dm.GH // dm.G
    z = jax.nn.gelu(z_ref[...].astype(F32))
    u, v = z[:, :gh], z[:, gh:]
    xc = v - jnp.mean(v, axis=-1, keepdims=True)
    vn = xc * lax.rsqrt(jnp.mean(xc * xc, axis=-1, keepdims=True) + EPS) * lng_ref[...] + lnb_ref[...]
    if emit_v:
        refs[6][...] = vn
    row = lax.broadcasted_iota(I32, (lv, lv), 0)
    col = lax.broadcasted_iota(I32, (lv, lv), 1)
    vb = vn.astype(BF16)
    for g in range(dm.G):
        w = jnp.where(col <= row, ws_ref[g, :lv, :lv], 0.0).astype(BF16)
        mixed = _dot(w, vb[:, g * gch:(g + 1) * gch]) + bst_ref[:lv, g:g + 1]
        g_ref[:, g * gch:(g + 1) * gch] = (u[:, g * gch:(g + 1) * gch] * mixed).astype(BF16)


def _sgu(dm, proj, lng, lnb, ws, bst, *, sample):
    if sample:
        b_n, n_grid, nc, lv, row0, n_rows = dm.Bs, ROW_TILE // dm.DS, 1, dm.DS, dm.Tp // dm.DS, ROW_TILE
    else:
        b_n, n_grid, nc, lv, row0, n_rows = dm.Bp, dm.Bp, dm.S // dm.GC, dm.GC, 0, dm.Tp
    gh = dm.GH
    rb = lambda b, c: row0 + b * nc + c
    in_specs = [
        pl.BlockSpec((lv, 2 * gh), lambda b, c: (rb(b, c), dm.off_z // (2 * gh))),
        pl.BlockSpec((1, gh), lambda b, c: (0, 0)),
        pl.BlockSpec((1, gh), lambda b, c: (0, 0)),
        pl.BlockSpec((dm.G, dm.GC, dm.GC), lambda b, c: (0, 0, 0)),
        pl.BlockSpec((dm.GC, LANES), lambda b, c: (0, 0)),
    ]
    out_specs = [pl.BlockSpec((lv, gh), lambda b, c: (b * nc + c, 0))]
    out_shape = [jax.ShapeDtypeStruct((n_rows, gh), BF16)]
    if sample:
        out_specs.append(pl.BlockSpec((None, lv, gh), lambda b, c: (jnp.minimum(b, b_n - 1), 0, 0)))
        out_shape.append(jax.ShapeDtypeStruct((b_n, lv, gh), F32))
    return pl.pallas_call(
        functools.partial(_sgu_kernel, dm=dm, lv=lv, n_seq=b_n, emit_v=sample),
        grid=(n_grid, nc),
        in_specs=in_specs,
        out_specs=out_specs,
        out_shape=out_shape,
        compiler_params=_cparams(("arbitrary", "arbitrary"), 24),
        name="sgu_sample" if sample else "sgu_prompt",
    )(proj, lng, lnb, ws, bst)


def _merge_kernel(ap_ref, as_ref, gp_ref, gs_ref, ga_ref, gb_ref, wa_ref, wb_ref, o_ref):
    is_s = pl.program_id(1) == pl.num_programs(1) - 1
    pa = _dot(jnp.where(is_s, as_ref[...], ap_ref[...]), wa_ref[...])
    pb = _dot(jnp.where(is_s, gs_ref[...], gp_ref[...]), wb_ref[...])
    o_ref[...] = (jax.nn.sigmoid(ga_ref[...].astype(F32)) * pa
                  + jax.nn.sigmoid(gb_ref[...].astype(F32)) * pb).astype(o_ref.dtype)


def _merge(dm, a_p, a_s, g_p, g_s, proj, wa, wb):
    tc, d, v, gh = dm.Tc, dm.D, dm.V, dm.GH
    tm = ROW_TILE
    tn = _pick_tile(d, 1024)
    np_tiles = dm.Tp // tm
    prompt = lambda j, i: (jnp.minimum(i, np_tiles - 1), 0)
    const = lambda j, i: (0, 0)
    blocks = 2 * tm * (v + gh) * 2 + 3 * tm * tn * 2 + (v + gh) * tn * 2
    return pl.pallas_call(
        _merge_kernel,
        grid=(d // tn, tc // tm),
        in_specs=[pl.BlockSpec((tm, v), prompt),
                  pl.BlockSpec((tm, v), const),
                  pl.BlockSpec((tm, gh), prompt),
                  pl.BlockSpec((tm, gh), const),
                  pl.BlockSpec((tm, tn), lambda j, i: (i, dm.off_ga // tn + j)),
                  pl.BlockSpec((tm, tn), lambda j, i: (i, dm.off_gb // tn + j)),
                  pl.BlockSpec((v, tn), lambda j, i: (0, j)),
                  pl.BlockSpec((gh, tn), lambda j, i: (0, j))],
        out_specs=pl.BlockSpec((tm, tn), lambda j, i: (i, j)),
        out_shape=jax.ShapeDtypeStruct((tc, d), BF16),
        compiler_params=_cparams(("arbitrary", "arbitrary"), (2 * blocks + 6 * tm * tn * 4) / MIB + 8),
        name="merge",
    )(a_p, a_s, g_p, g_s, proj, proj, wa, wb)


def _route(logits, eg, ne):
    epg = ne // eg
    lane = lax.broadcasted_iota(I32, logits.shape, 1)
    gmask = lane < eg
    gmax = jnp.max(jnp.where(gmask, logits, NEG), axis=1, keepdims=True)
    gexp = jnp.where(gmask, jnp.exp(jnp.minimum(logits - gmax, 0.0)), 0.0)
    pg = gexp / jnp.sum(gexp, axis=1, keepdims=True)
    p_grp = jnp.max(pg, axis=1, keepdims=True)
    g_sel = jnp.min(jnp.where(gmask & (pg == p_grp), lane, LANES), axis=1, keepdims=True)
    lo = eg + g_sel * epg
    emask = (lane >= lo) & (lane < lo + epg)
    emax = jnp.max(jnp.where(emask, logits, NEG), axis=1, keepdims=True)
    eexp = jnp.where(emask, jnp.exp(jnp.minimum(logits - emax, 0.0)), 0.0)
    pe = eexp / jnp.sum(eexp, axis=1, keepdims=True)
    p1 = jnp.max(jnp.where(emask, pe, -1.0), axis=1, keepdims=True)
    i1 = jnp.min(jnp.where(emask & (pe == p1), lane, LANES), axis=1, keepdims=True)
    rest = emask & (lane != i1)
    p2 = jnp.max(jnp.where(rest, pe, -1.0), axis=1, keepdims=True)
    i2 = jnp.min(jnp.where(rest & (pe == p2), lane, LANES), axis=1, keepdims=True)
    psum = p1 + p2
    eid = jnp.where(lane == 0, i1 - eg, jnp.where(lane == 1, i2 - eg, 0))
    gate = jnp.where(lane == 0, p_grp * (p1 / psum), jnp.where(lane == 1, p_grp * (p2 / psum), 0.0))
    return eid, gate


def _outproj_kernel(m_ref, x_ref, w_ref, g_ref, mod_ref, wr_ref, rb_ref,
                    x1_ref, hp_ref, eid_ref, gate_ref, acc_scr, hi_scr, lo_scr, *, dm):
    i = pl.program_id(0)
    t = jnp.maximum(i - 1, 0)
    is_s = t == dm.n_tiles - 1
    bp = jnp.minimum(t // (dm.S // ROW_TILE), dm.Bp - 1)
    cur, prev = i % 2, (i + 1) % 2
    g = g_ref[...]

    @pl.when(i == 0)
    def _no_previous_tile():
        acc_scr[1] = jnp.zeros(acc_scr.shape[1:], F32)

    def chunk(j):
        rows = pl.ds(j * ROW_CHUNK, ROW_CHUNK)
        idx = jnp.where(is_s, dm.Bp + j, bp)
        x1 = x_ref[rows, :] + mod_ref[GT1, pl.ds(idx, 1), :] * acc_scr[prev, rows, :]
        x1_ref[rows, :] = x1
        h = _rms_mod(x1, g, mod_ref[SC2, pl.ds(idx, 1), :], mod_ref[SH2, pl.ds(idx, 1), :])
        hi, lo = _split_bf16(h)
        hi_scr[rows, :] = hi
        lo_scr[rows, :] = lo
        hp_ref[rows, :] = _pack_pair(h)

    n_chunks = ROW_TILE // ROW_CHUNK
    n_slices = min(dm.D // 256, n_chunks)
    for s in range(n_slices):
        cols = slice(s * dm.D // n_slices, (s + 1) * dm.D // n_slices)
        acc_scr[cur, :, cols] = _dot(m_ref[...], w_ref[:, cols])
        for j in range(s * n_chunks // n_slices, (s + 1) * n_chunks // n_slices):
            chunk(j)
    w_hi, w_lo = _split_bf16(wr_ref[...])
    logits = _dot3(hi_scr[...], lo_scr[...], w_hi, w_lo) + rb_ref[...]
    eid, gate = _route(logits, dm.EG, dm.NE)
    eid_ref[...] = eid
    gate_ref[...] = gate


def _outproj(dm, l, merged, x, w_out, g2, mod, w_router, b_router):
    d, tc = dm.D, dm.Tc
    tile = lambda i: (jnp.maximum(i - 1, 0), 0)
    const = lambda i: (0, 0)
    return pl.pallas_call(
        functools.partial(_outproj_kernel, dm=dm),
        grid=(dm.n_tiles + 1,),
        in_specs=[pl.BlockSpec((ROW_TILE, d), lambda i: (jnp.minimum(i, dm.n_tiles - 1), 0)),
                  pl.BlockSpec((ROW_TILE, d), tile),
                  pl.BlockSpec((d, d), const),
                  pl.BlockSpec((None, 1, d), lambda i: (l, 0, 0)),
                  pl.BlockSpec((None, 6, dm.n_mod, d), lambda i: (l, 0, 0, 0)),
                  pl.BlockSpec((d, LANES), const),
                  pl.BlockSpec((1, LANES), const)],
        out_specs=[pl.BlockSpec((ROW_TILE, d), tile),
                   pl.BlockSpec((ROW_TILE, d // 2), tile),
                   pl.BlockSpec((ROW_TILE, LANES), tile),
                   pl.BlockSpec((ROW_TILE, LANES), tile)],
        out_shape=[jax.ShapeDtypeStruct((tc, d), F32),
                   jax.ShapeDtypeStruct((tc, d // 2), U32),
                   jax.ShapeDtypeStruct((tc, LANES), I32),
                   jax.ShapeDtypeStruct((tc, LANES), F32)],
        scratch_shapes=[pltpu.VMEM((2, ROW_TILE, d), F32), pltpu.VMEM((ROW_TILE, d), BF16),
                        pltpu.VMEM((ROW_TILE, d), BF16)],
        compiler_params=_cparams(("arbitrary",), (4 * d * d + ROW_TILE * d * 44) / MIB + 12),
        name="out_proj",
    )(merged, x, w_out, g2, mod, w_router, b_router)


def _rank_kernel(eid_ref, rank_ref, cnt_ref, run_scr):
    i = pl.program_id(0)

    @pl.when(i == 0)
    def _init():
        run_scr[...] = jnp.zeros_like(run_scr)

    eid = eid_ref[...]
    lane = lax.broadcasted_iota(I32, eid.shape, 1)
    e0 = lane == eid[:, 0:1]
    e1 = lane == eid[:, 1:2]
    hot = (e0 | e1).astype(BF16)
    n = eid.shape[0]
    strict = (lax.broadcasted_iota(I32, (n, n), 1) < lax.broadcasted_iota(I32, (n, n), 0)).astype(BF16)
    before = _dot(strict, hot) + run_scr[...]
    r0 = jnp.sum(jnp.where(e0, before, 0.0), axis=1, keepdims=True)
    r1 = jnp.sum(jnp.where(e1, before, 0.0), axis=1, keepdims=True)
    rank_ref[...] = jnp.where(lane == 0, r0, jnp.where(lane == 1, r1, 0.0)).astype(I32)
    run_scr[...] = run_scr[...] + jnp.sum(hot.astype(F32), axis=0, keepdims=True)
    cnt_ref[...] = run_scr[...].astype(I32)


def _rank(dm, eid):
    tile = lambda i: (i, 0)
    return pl.pallas_call(
        _rank_kernel,
        grid=(dm.n_tiles,),
        in_specs=[pl.BlockSpec((ROW_TILE, LANES), tile)],
        out_specs=[pl.BlockSpec((ROW_TILE, LANES), tile), pl.BlockSpec((1, LANES), lambda i: (0, 0))],
        out_shape=[jax.ShapeDtypeStruct((dm.Tc, LANES), I32), jax.ShapeDtypeStruct((1, LANES), I32)],
        scratch_shapes=[pltpu.VMEM((1, LANES), F32)],
        compiler_params=_cparams(("arbitrary",), 16),
        name="moe_rank",
    )(eid)


def _dispatch_kernel(tail_ref, dest_ref, h_ref, xs_ref, zero_scr, sem):
    @pl.when(pl.program_id(0) == 0)
    def _clear_tails():
        zero_scr[...] = jnp.zeros_like(zero_scr)
        n_e = tail_ref.shape[0]

        def tail_copy(e):
            start = pl.multiple_of(tail_ref[e], MOE_BLOCK)
            return pltpu.make_async_copy(zero_scr, xs_ref.at[pl.ds(start, MOE_BLOCK)], sem)

        for e in range(n_e):
            tail_copy(e).start()
        for e in range(n_e):
            tail_copy(e).wait()

    for r in range(ROW_TILE):
        for k in range(TOP_K):
            pltpu.make_async_copy(h_ref.at[pl.ds(r, 1)], xs_ref.at[pl.ds(dest_ref[TOP_K * r + k], 1)],
                                  sem).start(priority=k)
    for k in range(TOP_K):
        pltpu.make_async_copy(h_ref, xs_ref.at[pl.ds(0, ROW_TILE)], sem).wait()


def _dispatch(dm, tails, dest, hp):
    d2 = dm.D // 2
    return pl.pallas_call(
        _dispatch_kernel,
        grid_spec=pltpu.PrefetchScalarGridSpec(
            num_scalar_prefetch=1,
            grid=(dm.n_tiles,),
            in_specs=[pl.BlockSpec((ROW_TILE * TOP_K,), lambda i, t: (i,), memory_space=pltpu.SMEM),
                      pl.BlockSpec((ROW_TILE, d2), lambda i, t: (i, 0))],
            out_specs=pl.BlockSpec(memory_space=pl.ANY),
            scratch_shapes=[pltpu.VMEM((MOE_BLOCK, d2), U32), pltpu.SemaphoreType.DMA]),
        out_shape=jax.ShapeDtypeStruct((dm.cap + dm.NE * MOE_BLOCK, d2), U32),
        compiler_params=_cparams(("arbitrary",), 16),
        name="moe_dispatch",
    )(tails, dest, hp)


def _expert_kernel(be_ref, nu_ref, xs_ref, wg_ref, wu_ref, wd_ref, ys_ref, wg_s, wu_s, wd_s):
    j = pl.program_id(0)
    changed = (j == 0) | (be_ref[j] != be_ref[jnp.maximum(j - 1, 0)])

    @pl.when(changed)
    def _cast():
        wg_s[...] = wg_ref[...].astype(BF16)
        wu_s[...] = wu_ref[...].astype(BF16)
        wd_s[...] = wd_ref[...].astype(BF16)

    @pl.when(j < nu_ref[0])
    def _compute():
        d2 = xs_ref.shape[1]
        lo, hi = _unpack_pair(xs_ref[...])
        lo, hi = lo.astype(BF16), hi.astype(BF16)
        hg = _dot(lo, wg_s[:d2, :]) + _dot(hi, wg_s[d2:, :])
        hu = _dot(lo, wu_s[:d2, :]) + _dot(hi, wu_s[d2:, :])
        act = (hg * jax.nn.sigmoid(hg) * hu).astype(BF16)
        ys_ref[...] = _pack_pair(_dot(act, wd_s[...]))


def _experts(dm, l, blk_e, n_used, xs, w_gate, w_up, w_down):
    d, de, d2 = dm.D, dm.DE, dm.D // 2
    nb = dm.cap // MOE_BLOCK
    wmap = lambda j, be, nu: (l, be[j], 0, 0)
    rows = lambda j, be, nu: (jnp.minimum(j, nu[0] - 1), 0)
    return pl.pallas_call(
        _expert_kernel,
        grid_spec=pltpu.PrefetchScalarGridSpec(
            num_scalar_prefetch=2,
            grid=(nb,),
            in_specs=[pl.BlockSpec((MOE_BLOCK, d2), rows),
                      pl.BlockSpec((None, None, d, de), wmap),
                      pl.BlockSpec((None, None, d, de), wmap),
                      pl.BlockSpec((None, None, de, d), wmap)],
            out_specs=pl.BlockSpec((MOE_BLOCK, d2), rows),
            scratch_shapes=[pltpu.VMEM((d, de), BF16), pltpu.VMEM((d, de), BF16), pltpu.VMEM((de, d), BF16)]),
        out_shape=jax.ShapeDtypeStruct((dm.cap, d2), U32),
        compiler_params=_cparams(("arbitrary",), (3 * d * de * (2 * 4 + 2) + MOE_BLOCK * d * 24) / MIB + 8),
        name="moe_experts",
    )(blk_e, n_used, xs, w_gate, w_up, w_down)


def _combine_kernel(*refs, dm, last):
    if last:
        dest_ref, destn_ref, ys_ref, x_ref, gate_ref, mod_ref, g_ref, yp_ref, ysm_ref, ybuf, y_scr, sem = refs
    else:
        (dest_ref, destn_ref, ys_ref, x_ref, gate_ref, mod_ref, g_ref, modn_ref, w_ref,
         o_ref, h_ref, p_ref, ybuf, lo_scr, sem) = refs
    i = pl.program_id(0)
    is_s = i == dm.n_tiles - 1
    bp = jnp.minimum(i // (dm.S // ROW_TILE), dm.Bp - 1)
    d2 = dm.D // 2
    g = g_ref[...]
    slot = i % 2

    def issue(idx_ref, to_slot, r0, n):
        for r in range(n):
            for k in range(TOP_K):
                pltpu.make_async_copy(ys_ref.at[pl.ds(idx_ref[TOP_K * (r0 + r) + k], 1)],
                                      ybuf.at[to_slot, k, pl.ds(r0 + r, 1)], sem.at[to_slot]).start(priority=k)

    def wait_slot(s):
        for k in range(TOP_K):
            pltpu.make_async_copy(ys_ref.at[pl.ds(0, ROW_TILE)], ybuf.at[s, k], sem.at[s]).wait()

    @pl.when(i == 0)
    def _first_tile():
        issue(dest_ref, 0, 0, ROW_TILE)

    wait_slot(slot)

    def chunk(j):
        rows = pl.ds(j * ROW_CHUNK, ROW_CHUNK)
        issue(destn_ref, 1 - slot, j * ROW_CHUNK, ROW_CHUNK)
        idx = jnp.where(is_s, dm.Bp + j, bp)
        gt = mod_ref[GT2, pl.ds(idx, 1), :]
        gates = gate_ref[rows, :]
        g0, g1 = gates[:, 0:1], gates[:, 1:2]
        lo0, hi0 = _unpack_pair(ybuf[slot, 0, rows, :])
        lo1, hi1 = _unpack_pair(ybuf[slot, 1, rows, :])
        x_lo = x_ref[rows, :d2] + gt[:, :d2] * (g0 * lo0 + g1 * lo1)
        x_hi = x_ref[rows, d2:] + gt[:, d2:] * (g0 * hi0 + g1 * hi1)
        ms = (jnp.sum(x_lo * x_lo, axis=-1, keepdims=True)
              + jnp.sum(x_hi * x_hi, axis=-1, keepdims=True)) * (1.0 / dm.D)
        rs = lax.rsqrt(ms + EPS)
        if last:
            y_scr[rows, :d2] = x_lo * rs * g[:, :d2]
            y_scr[rows, d2:] = x_hi * rs * g[:, d2:]
        else:
            o_ref[rows, :d2] = x_lo
            o_ref[rows, d2:] = x_hi
            sc = modn_ref[SC1, pl.ds(idx, 1), :]
            sh = modn_ref[SH1, pl.ds(idx, 1), :]
            for half, xh in ((slice(0, d2), x_lo), (slice(d2, dm.D), x_hi)):
                h = xh * rs * g[:, half] * (1.0 + sc[:, half]) + sh[:, half]
                hi, lo = _split_bf16(h)
                h_ref[rows, half] = hi
                lo_scr[rows, half] = lo

    for j in range(ROW_TILE // ROW_CHUNK):
        chunk(j)

    @pl.when(is_s)
    def _drain():
        wait_slot(1 - slot)

    if last:
        @pl.when(is_s)
        def _sample_rows():
            ysm_ref[...] = y_scr[...]

        @pl.when(jnp.logical_not(is_s))
        def _prompt_rows():
            yp_ref[...] = y_scr[...]
    else:
        w_hi, w_lo = _gate_weight(w_ref)
        p_ref[...] = _dot3(h_ref[...], lo_scr[...], w_hi, w_lo)


def _combine(dm, l, dest, ys, x1, gate, mod, g_next, w_in=None):
    d = dm.D
    last = w_in is None
    tile = lambda i: (i, 0)
    np_tiles = dm.Tp // ROW_TILE
    in_specs = [pl.BlockSpec((ROW_TILE * TOP_K,), lambda i: (i,), memory_space=pltpu.SMEM),
                pl.BlockSpec((ROW_TILE * TOP_K,), lambda i: (jnp.minimum(i + 1, dm.n_tiles - 1),),
                             memory_space=pltpu.SMEM),
                pl.BlockSpec(memory_space=pl.ANY),
                pl.BlockSpec((ROW_TILE, d), tile),
                pl.BlockSpec((ROW_TILE, LANES), tile),
                pl.BlockSpec((None, 6, dm.n_mod, d), lambda i: (l, 0, 0, 0))]
    scratch = [pltpu.VMEM((2, TOP_K, ROW_TILE, d // 2), U32)]
    if last:
        args = (dest, dest, ys, x1, gate, mod, g_next)
        in_specs.append(pl.BlockSpec((1, d), lambda i: (0, 0)))
        out_specs = [pl.BlockSpec((ROW_TILE, d), lambda i: (jnp.minimum(i, np_tiles - 1), 0)),
                     pl.BlockSpec((ROW_TILE, d), lambda i: (0, 0))]
        out_shape = [jax.ShapeDtypeStruct((dm.Tp, d), F32), jax.ShapeDtypeStruct((ROW_TILE, d), F32)]
        scratch.append(pltpu.VMEM((ROW_TILE, d), F32))
    else:
        args = (dest, dest, ys, x1, gate, mod, g_next, mod, w_in)
        in_specs += [pl.BlockSpec((None, 1, d), lambda i: (l + 1, 0, 0)),
                     pl.BlockSpec((None, 6, dm.n_mod, d), lambda i: (l + 1, 0, 0, 0)),
                     pl.BlockSpec((None, 2 * dm.H, d), lambda i: (l + 1, dm.off_z // (2 * dm.H), 0))]
        out_specs = [pl.BlockSpec((ROW_TILE, d), tile), pl.BlockSpec((ROW_TILE, d), tile),
                     pl.BlockSpec((ROW_TILE, LANES), tile)]
        out_shape = [jax.ShapeDtypeStruct((dm.Tc, d), F32), jax.ShapeDtypeStruct((dm.Tc, d), BF16),
                     jax.ShapeDtypeStruct((dm.Tc, LANES), F32)]
        scratch.append(pltpu.VMEM((ROW_TILE, d), BF16))
    scratch.append(pltpu.SemaphoreType.DMA((2,)))
    return pl.pallas_call(
        functools.partial(_combine_kernel, dm=dm, last=last),
        grid=(dm.n_tiles,),
        in_specs=in_specs,
        out_specs=out_specs,
        out_shape=out_shape,
        scratch_shapes=scratch,
        compiler_params=_cparams(("arbitrary",), ROW_TILE * d * 40 / MIB + 16),
        name="moe_combine_last" if last else "moe_combine",
    )(*args)


def _pad_lanes(x, n=LANES):
    return jnp.pad(x, [(0, 0)] * (x.ndim - 1) + [(0, n - x.shape[-1])])


def _moe_plan(dm, eid, rank, counts):
    counts = counts[0, :dm.NE]
    padded = (counts + MOE_BLOCK - 1) // MOE_BLOCK * MOE_BLOCK
    pad_end = jnp.cumsum(padded)
    pad_start = pad_end - padded
    dest = (pad_start[eid[:, :TOP_K]] + rank[:, :TOP_K]).reshape(-1).astype(I32)
    nb = dm.cap // MOE_BLOCK
    n_used = (pad_end[-1] // MOE_BLOCK).astype(I32)
    starts = jnp.minimum(jnp.arange(nb, dtype=I32), n_used - 1) * MOE_BLOCK
    blk_e = jnp.minimum(jnp.sum(pad_end[None, :] <= starts[:, None], axis=1), dm.NE - 1).astype(I32)
    spare = dm.cap + jnp.arange(dm.NE, dtype=I32) * MOE_BLOCK
    tails = jnp.where(padded > 0, pad_end - MOE_BLOCK, spare).astype(I32)
    return dest, blk_e, n_used.reshape(1), tails


def kernel(x_prompt, x_sample, c_prompt, c_sample, state_mlstm_C, state_mlstm_n, state_mlstm_m, w_ada, b_ada, norm1_g, norm2_g, w_in, b_igate, b_fgate, mh_norm_g, sgu_ln_g, sgu_ln_b, w_spatial, b_spatial, w_branch_a, w_branch_b, w_out, w_router_group, b_router_group, w_router_expert, b_router_expert, w_expert_gate, w_expert_up, w_expert_down, final_norm_g):
    bp, s, d = x_prompt.shape
    bs, ds_, _ = x_sample.shape
    depth = w_in.shape[0]
    h_n = b_igate.shape[1]
    dv, dk = state_mlstm_C.shape[-2:]
    qk, v = h_n * dk, h_n * dv
    gh = (w_in.shape[2] - (2 * qk + 2 * v + 2 * h_n + 2 * d)) // 2
    dm = Dims(Bp=bp, S=s, Bs=bs, DS=ds_, D=d, DEPTH=depth, H=h_n, DK=dk, DV=dv, GH=gh,
              G=w_spatial.shape[1], GC=w_spatial.shape[2], EG=w_router_group.shape[-1],
              NE=w_router_expert.shape[-1], DE=w_expert_gate.shape[-1])
    assert dm.DS == ROW_CHUNK and dm.Ts <= ROW_TILE and s % ROW_TILE == 0 and s % MLSTM_CHUNK == 0
    assert dm.EG + dm.NE <= LANES and 2 * h_n <= LANES and (2 * h_n) % 8 == 0 and dm.Bp + dm.Bs <= 32
    w_in = jnp.swapaxes(w_in, 1, 2)
    assert dm.off_v % v == 0 and dm.off_o % v == 0 and dm.off_z % (2 * gh) == 0 and d % 256 == 0

    c_all = jnp.concatenate([c_prompt, c_sample, jnp.zeros((32 - bp - bs, d), F32)], axis=0)
    mod = _adaln(c_all, w_ada, b_ada)[:, :bp + bs].reshape(depth, bp + bs, 6, d)
    mod = jnp.pad(mod, ((0, 0), (0, dm.n_mod - bp - bs), (0, 0), (0, 0))).transpose(0, 2, 1, 3)

    xs_rows = jnp.pad(x_sample.reshape(dm.Ts, d), ((0, ROW_TILE - dm.Ts), (0, 0)))
    x = (x_prompt.reshape(dm.Tp, d), xs_rows)
    zeros_c = jnp.zeros((1, bp, h_n, dv, dk), F32)
    zeros_n = jnp.zeros((1, bp, h_n, dk), F32)
    zeros_m = jnp.zeros((1, bp, 1, LANES), F32)
    m_in = _pad_lanes(state_mlstm_m)[:, :, None, :]
    g1 = norm1_g.reshape(depth, 1, d)
    out_c_p, out_n_p, out_m_p, out_c_s, out_n_s, out_m_s, out_v = [], [], [], [], [], [], []

    for l in range(depth):
        if l == 0:
            hb, gates, x = _norm_proj(dm, l, x, g1, mod, w_in, SC1, SH1)
        proj = _in_proj(dm, l, hb, w_in)

        bias = _pad_lanes(jnp.concatenate([b_igate[l], b_fgate[l]])[None, :])
        mhg = mh_norm_g[l][None, :]
        a_p, c_p, n_p, m_p = _mlstm(dm, proj, gates, bias, mhg, zeros_c, zeros_n, zeros_m, 0, sample=False)
        a_s, c_s, n_s, m_s = _mlstm(dm, proj, gates, bias, mhg, state_mlstm_C, state_mlstm_n, m_in, l,
                                    sample=True)
        lng, lnb = sgu_ln_g[l][None, :], sgu_ln_b[l][None, :]
        bst = _pad_lanes(b_spatial[l].T)
        (g_p,) = _sgu(dm, proj, lng, lnb, w_spatial[l], bst, sample=False)
        g_s, v_rows = _sgu(dm, proj, lng, lnb, w_spatial[l], bst, sample=True)

        merged = _merge(dm, a_p, a_s, g_p, g_s, proj, w_branch_a[l].astype(BF16), w_branch_b[l].astype(BF16))
        w_router = _pad_lanes(jnp.concatenate([w_router_group[l], w_router_expert[l]], axis=1))
        b_router = _pad_lanes(jnp.concatenate([b_router_group[l], b_router_expert[l]])[None, :])
        x1, hp, eid, gate = _outproj(dm, l, merged, x, w_out[l].astype(BF16), norm2_g.reshape(depth, 1, d),
                                     mod, w_router, b_router)

        rank, counts = _rank(dm, eid)
        dest, blk_e, n_used, tails = _moe_plan(dm, eid, rank, counts)
        xs = _dispatch(dm, tails, dest, hp)
        ys = _experts(dm, l, blk_e, n_used, xs, w_expert_gate, w_expert_up, w_expert_down)
        if l + 1 < depth:
            x, hb, gates = _combine(dm, l, dest, ys, x1, gate, mod, g1, w_in)
        else:
            y_p, y_s = _combine(dm, l, dest, ys, x1, gate, mod, final_norm_g[None, :])

        out_c_p.append(c_p)
        out_n_p.append(n_p)
        out_m_p.append(m_p[:, 0, :h_n])
        out_c_s.append(c_s)
        out_n_s.append(n_s)
        out_m_s.append(m_s[:, 0, :h_n])
        out_v.append(v_rows)

    return (y_p.reshape(bp, s, d), y_s[:dm.Ts].reshape(bs, ds_, d),
            jnp.stack(out_c_p), jnp.stack(out_n_p), jnp.stack(out_m_p),
            jnp.stack(out_c_s), jnp.stack(out_n_s), jnp.stack(out_m_s), jnp.stack(out_v))
```

```python
import functools
from typing import NamedTuple

import jax
import jax.numpy as jnp
from jax import lax
from jax.experimental import pallas as pl
from jax.experimental.pallas import tpu as pltpu

F32, BF16, U32, I32 = jnp.float32, jnp.bfloat16, jnp.uint32, jnp.int32
EPS = 1e-6
NEG = -1e30
LANES = 128
ROW_TILE = 512
ROW_CHUNK = 16
MLSTM_CHUNK = 256
MOE_BLOCK = 512
TOP_K = 2
DMA_UNROLL = 8
CHUNK_UNROLL = 2
MIB = 1024 * 1024
SH1, SC1, GT1, SH2, SC2, GT2 = range(6)


class Dims(NamedTuple):
    Bp: int
    S: int
    Bs: int
    DS: int
    D: int
    DEPTH: int
    H: int
    DK: int
    DV: int
    GH: int
    G: int
    GC: int
    EG: int
    NE: int
    DE: int

    @property
    def QK(self):
        return self.H * self.DK

    @property
    def V(self):
        return self.H * self.DV

    @property
    def Tp(self):
        return self.Bp * self.S

    @property
    def Ts(self):
        return self.Bs * self.DS

    @property
    def Tc(self):
        return self.Tp + ROW_TILE

    @property
    def n_tiles(self):
        return self.Tc // ROW_TILE

    @property
    def n_mod(self):
        return self.Bp + ROW_TILE // ROW_CHUNK

    @property
    def off_v(self):
        return 2 * self.QK

    @property
    def off_o(self):
        return 2 * self.QK + self.V

    @property
    def off_z(self):
        return 2 * self.QK + 2 * self.V

    @property
    def off_ga(self):
        return self.off_z + 2 * self.GH

    @property
    def off_gb(self):
        return self.off_ga + self.D

    @property
    def n_main(self):
        return self.off_gb + self.D

    @property
    def cap(self):
        tk = self.Tc * TOP_K
        return -(-(tk + self.NE * (MOE_BLOCK - 1)) // MOE_BLOCK) * MOE_BLOCK


def _cparams(semantics, vmem_mib):
    return pltpu.CompilerParams(dimension_semantics=semantics, vmem_limit_bytes=int(vmem_mib * MIB))


def _pick_tile(n, cap, unit=256):
    best = None
    for t in range(unit, min(n, cap) + 1, unit):
        if n % t == 0:
            best = t
    assert best is not None, (n, cap)
    return best


def _dot(a, b):
    return jnp.dot(a, b, preferred_element_type=F32)


def _split_bf16(x):
    hi = x.astype(BF16)
    lo = (x - hi.astype(F32)).astype(BF16)
    return hi, lo


def _dot3(a_hi, a_lo, w_hi, w_lo):
    return _dot(a_hi, w_hi) + _dot(a_hi, w_lo) + _dot(a_lo, w_hi)


def _gate_weight(w_ref):
    rows, d = w_ref.shape
    w = jnp.concatenate([w_ref[...], jnp.zeros((LANES - rows, d), F32)], axis=0).T
    return _split_bf16(w)


def _rms_mod(x, g, sc, sh):
    y = x * lax.rsqrt(jnp.mean(x * x, axis=-1, keepdims=True) + EPS) * g
    return y * (1.0 + sc) + sh


def _pack_pair(x):
    d2 = x.shape[-1] // 2
    bits = lax.bitcast_convert_type(x.astype(BF16).astype(F32), U32)
    return (bits[:, :d2] >> 16) | (bits[:, d2:] & jnp.uint32(0xFFFF0000))


def _unpack_pair(p):
    lo = lax.bitcast_convert_type(p << 16, F32)
    hi = lax.bitcast_convert_type(p & jnp.uint32(0xFFFF0000), F32)
    return lo, hi


def _adaln_kernel(c_ref, w_ref, b_ref, o_ref):
    c = c_ref[...]
    s = (c * jax.nn.sigmoid(c)).astype(BF16)
    o_ref[...] = _dot(s, w_ref[...].astype(BF16)) + b_ref[...]


def _adaln(c_all, w_ada, b_ada):
    depth, d, n6 = w_ada.shape
    r = c_all.shape[0]
    tn = _pick_tile(n6, 1024)
    return pl.pallas_call(
        _adaln_kernel,
        grid=(depth, n6 // tn),
        in_specs=[pl.BlockSpec((r, d), lambda l, j: (0, 0)),
                  pl.BlockSpec((None, d, tn), lambda l, j: (l, 0, j)),
                  pl.BlockSpec((None, 1, tn), lambda l, j: (l, 0, j))],
        out_specs=pl.BlockSpec((None, r, tn), lambda l, j: (l, 0, j)),
        out_shape=jax.ShapeDtypeStruct((depth, r, n6), F32),
        compiler_params=_cparams(("arbitrary", "arbitrary"), 2 * d * tn * 4 / MIB + 3 * d * tn * 2 / MIB + 8),
        name="adaln",
    )(c_all, w_ada, b_ada.reshape(depth, 1, n6))


def _norm_proj_kernel(*refs, first, dm, k_sc, k_sh):
    if first:
        xp_ref, xs_ref, g_ref, mod_ref, w_ref, h_ref, p_ref, xc_ref, lo_scr = refs
    else:
        x_ref, g_ref, mod_ref, w_ref, h_ref, p_ref, lo_scr = refs
    i = pl.program_id(0)
    is_s = i == dm.n_tiles - 1
    bp = jnp.minimum(i // (dm.S // ROW_TILE), dm.Bp - 1)
    g = g_ref[...]

    def chunk(j, carry):
        rows = pl.ds(pl.multiple_of(j * ROW_CHUNK, ROW_CHUNK), ROW_CHUNK)
        if first:
            x = jnp.where(is_s, xs_ref[rows, :], xp_ref[rows, :])
            xc_ref[rows, :] = x
        else:
            x = x_ref[rows, :]
        idx = jnp.where(is_s, dm.Bp + j, bp)
        h = _rms_mod(x, g, mod_ref[k_sc, pl.ds(idx, 1), :], mod_ref[k_sh, pl.ds(idx, 1), :])
        hi, lo = _split_bf16(h)
        h_ref[rows, :] = hi
        lo_scr[rows, :] = lo
        return carry

    lax.fori_loop(0, ROW_TILE // ROW_CHUNK, chunk, 0, unroll=CHUNK_UNROLL)
    w_hi, w_lo = _gate_weight(w_ref)
    p_ref[...] = _dot3(h_ref[...], lo_scr[...], w_hi, w_lo)


def _norm_proj(dm, l, x_in, g, mod, w_in, k_sc, k_sh):
    d, tc, nt = dm.D, dm.Tc, dm.n_tiles
    first = isinstance(x_in, tuple)
    np_tiles = dm.Tp // ROW_TILE
    tile = lambda i: (i, 0)
    if first:
        x_args = list(x_in)
        x_specs = [pl.BlockSpec((ROW_TILE, d), lambda i: (jnp.minimum(i, np_tiles - 1), 0)),
                   pl.BlockSpec((ROW_TILE, d), lambda i: (0, 0))]
    else:
        x_args = [x_in]
        x_specs = [pl.BlockSpec((ROW_TILE, d), tile)]
    out_shape = [jax.ShapeDtypeStruct((tc, d), BF16), jax.ShapeDtypeStruct((tc, LANES), F32)]
    out_specs = [pl.BlockSpec((ROW_TILE, d), tile), pl.BlockSpec((ROW_TILE, LANES), tile)]
    if first:
        out_shape.append(jax.ShapeDtypeStruct((tc, d), F32))
        out_specs.append(pl.BlockSpec((ROW_TILE, d), tile))
    return pl.pallas_call(
        functools.partial(_norm_proj_kernel, first=first, dm=dm, k_sc=k_sc, k_sh=k_sh),
        grid=(nt,),
        in_specs=x_specs + [pl.BlockSpec((None, 1, d), lambda i: (l, 0, 0)),
                            pl.BlockSpec((None, 6, dm.n_mod, d), lambda i: (l, 0, 0, 0)),
                            pl.BlockSpec((None, 2 * dm.H, d), lambda i: (l, dm.off_z // (2 * dm.H), 0))],
        out_specs=out_specs,
        out_shape=out_shape,
        scratch_shapes=[pltpu.VMEM((ROW_TILE, d), BF16)],
        compiler_params=_cparams(("arbitrary",), ROW_TILE * d * 40 / MIB + 16),
        name="norm_proj",
    )(*x_args, g, mod, w_in)


def _in_proj_kernel(a_ref, w_ref, wn_ref, o_ref, w_scr, *, n_aligned, shift, z_tiles):
    j, i = pl.program_id(0), pl.program_id(1)

    groups = w_ref.shape[0] // LANES

    @pl.when((i == 0) & (j < n_aligned))
    def _cast():
        for g in range(groups):
            w_scr[:, g * LANES:(g + 1) * LANES] = w_ref[g * LANES:(g + 1) * LANES, :].T.astype(BF16)

    @pl.when((i == 0) & (j >= n_aligned))
    def _cast_shifted():
        for g in range(groups):
            if g + 1 < groups:
                rows = w_ref[g * LANES + shift:(g + 1) * LANES + shift, :]
            else:
                rows = jnp.concatenate([w_ref[g * LANES + shift:, :], wn_ref[...]], axis=0)
            w_scr[:, g * LANES:(g + 1) * LANES] = rows.T.astype(BF16)

    is_z = (j >= z_tiles[0]) & (j < z_tiles[1])

    @pl.when(is_z)
    def _gelu_tile():
        o_ref[...] = jax.nn.gelu(_dot(a_ref[...], w_scr[...])).astype(o_ref.dtype)

    @pl.when(jnp.logical_not(is_z))
    def _plain_tile():
        o_ref[...] = _dot(a_ref[...], w_scr[...]).astype(o_ref.dtype)


def _in_proj(dm, l, a, w_in_t):
    m, k = a.shape
    off_if, n = dm.off_z, dm.n_main
    shift = 2 * dm.H
    tn = max(t for t in range(LANES, 1024 + 1, LANES) if off_if % t == 0 and dm.off_ga % t == 0 and n % t == 0)
    tm = _pick_tile(m, 1536)
    vmem = (2 * tm * k * 2 + 2 * k * (tn + shift) * 4 + k * tn * 2 + 2 * tm * tn * 2 + 2 * tm * tn * 4) / MIB + 8
    return pl.pallas_call(
        functools.partial(_in_proj_kernel, n_aligned=off_if // tn, shift=shift,
                          z_tiles=(dm.off_z // tn, dm.off_ga // tn)),
        grid=(n // tn, m // tm),
        in_specs=[pl.BlockSpec((tm, k), lambda j, i: (i, 0)),
                  pl.BlockSpec((None, tn, k), lambda j, i: (l, j, 0)),
                  pl.BlockSpec((None, shift, k), lambda j, i: (l, (j + 1) * (tn // shift), 0))],
        out_specs=pl.BlockSpec((tm, tn), lambda j, i: (i, j)),
        out_shape=jax.ShapeDtypeStruct((m, n), BF16),
        scratch_shapes=[pltpu.VMEM((k, tn), BF16)],
        compiler_params=_cparams(("arbitrary", "arbitrary"), vmem),
        name="in_proj",
    )(a, w_in_t, w_in_t)


def _pad_rows(x, rows):
    if x.shape[0] == rows:
        return x
    return jnp.concatenate([x, jnp.zeros((rows - x.shape[0], x.shape[1]), x.dtype)], axis=0)


def _mlstm_kernel(*refs, dm, lv, n_seq):
    a_ref = refs[10]
    seq_ok = pl.program_id(0) < n_seq

    @pl.when(seq_ok)
    def _run():
        _mlstm_step(*refs, dm=dm, lv=lv)

    @pl.when(jnp.logical_not(seq_ok))
    def _fill():
        a_ref[...] = jnp.zeros_like(a_ref)


def _mlstm_step(q_ref, k_ref, v_ref, o_ref, gt_ref, bias_ref, mhg_ref, c0_ref, n0_ref, m0_ref,
                a_ref, cout_ref, nout_ref, mout_ref, st_scr, m_scr, *, dm, lv):
    h_n, dk, dv = dm.H, dm.DK, dm.DV
    L = MLSTM_CHUNK
    c = pl.program_id(1)

    @pl.when(c == 0)
    def _init():
        for h in range(h_n):
            st_scr[h, :, :dv] = c0_ref[h].T
            st_scr[h, :, dv:] = jnp.broadcast_to(n0_ref[h:h + 1, :], (LANES, dk)).T
        m_scr[...] = m0_ref[...]

    row = lax.broadcasted_iota(I32, (L, L), 0)
    col = lax.broadcasted_iota(I32, (L, L), 1)
    causal = col <= row
    lane = lax.broadcasted_iota(I32, (L, LANES), 1)
    rowl = lax.broadcasted_iota(I32, (L, LANES), 0)

    xg = _pad_rows(gt_ref[...], L) + bias_ref[...]
    f_log = jnp.minimum(xg, 0.0) - jnp.log1p(jnp.exp(-jnp.abs(xg)))
    gl = jnp.where(lane < h_n, xg, f_log)
    gl = jnp.where(rowl < lv, gl, 0.0)
    tri = causal.astype(BF16)
    g_hi = gl.astype(BF16)
    r1 = gl - g_hi.astype(F32)
    g_mid = r1.astype(BF16)
    g_lo = (r1 - g_mid.astype(F32)).astype(BF16)
    cum = _dot(tri, g_hi) + _dot(tri, g_mid) + _dot(tri, g_lo)
    b_all = pltpu.roll(cum, LANES - h_n, axis=1)
    a_all = jnp.where(rowl < lv, gl - b_all, NEG)

    scale = dk ** -0.5
    kf = _pad_rows(k_ref[...], L).astype(F32) * scale
    kt_all = kf.T.astype(BF16)
    q_all = _pad_rows(q_ref[...], L)
    v_all = _pad_rows(v_ref[...], L)
    a_rows = a_all.T
    m_old = m_scr[...]
    m_new = m_old
    lane1 = lax.broadcasted_iota(I32, (1, LANES), 1)
    ones_blk = jnp.ones((L, LANES), BF16)
    v_blocks = dv // LANES

    for h in range(h_n):
        a_row = a_rows[h:h + 1, :]
        a_rep = jnp.broadcast_to(a_all[:, h:h + 1], (L, LANES))
        b_rep = jnp.broadcast_to(b_all[:, h:h + 1], (L, LANES))
        m_prev = m_old[:, h:h + 1]
        big_m = jnp.maximum(jnp.max(jnp.where(causal, a_row, NEG), axis=1, keepdims=True), m_prev)
        d_w = jnp.where(causal, jnp.exp(jnp.minimum(a_row - big_m, 0.0)), 0.0)
        inter = jnp.exp(m_prev - big_m)
        qh = q_all[:, h * dk:(h + 1) * dk]
        v_parts = [v_all[:, h * dv + j * LANES:h * dv + (j + 1) * LANES] for j in range(v_blocks)]
        kt = kt_all[h * dk:(h + 1) * dk, :]
        sd = (_dot(qh, kt) * d_w).astype(BF16)
        st = st_scr[h]
        x = _dot(sd, jnp.concatenate(v_parts + [ones_blk], axis=1)) + inter * _dot(qh, st.astype(BF16))
        den = x[:, dv:]
        inv = 1.0 / jnp.maximum(jnp.abs(den), jnp.exp(-(b_rep + big_m)))
        sq = jnp.sum(x[:, :dv] * x[:, :dv], axis=1, keepdims=True) * (1.0 / dv)
        sc = (inv * lax.rsqrt(inv * inv * sq + EPS))[:lv]
        for j in range(v_blocks):
            cols = slice(h * dv + j * LANES, h * dv + (j + 1) * LANES)
            gate_o = jax.nn.sigmoid(o_ref[:, cols].astype(F32))
            a_ref[:, cols] = (x[:lv, j * LANES:(j + 1) * LANES] * sc * (mhg_ref[:, cols] * gate_o)).astype(BF16)
        m_last = big_m[L - 1:L, :]
        decay = jnp.exp(m_prev - m_last)
        w_rep = jnp.exp(a_rep - m_last)
        vw = jnp.concatenate([(p.astype(F32) * w_rep).astype(BF16) for p in v_parts] + [w_rep.astype(BF16)], axis=1)
        st_scr[h] = decay * st + _dot(kt, vw)
        m_new = jnp.where(lane1 == h, b_all[L - 1:L, h:h + 1] + m_last, m_new)
    m_scr[...] = m_new

    @pl.when(c == pl.num_programs(1) - 1)
    def _fin():
        for h in range(h_n):
            cout_ref[h] = st_scr[h, :, :dv].T
            nout_ref[h:h + 1, :] = st_scr[h, :, dv:].T[0:1, :]
        mout_ref[...] = m_scr[...]


def _mlstm(dm, proj, gates, bias, mhg, c0, n0, m0, ls, *, sample):
    if sample:
        b_n, n_grid, nc, lv, row0, n_rows = dm.Bs, ROW_TILE // dm.DS, 1, dm.DS, dm.Tp // dm.DS, ROW_TILE
    else:
        b_n, n_grid, nc, lv, row0, n_rows = dm.Bp, dm.Bp, dm.S // MLSTM_CHUNK, MLSTM_CHUNK, 0, dm.Tp
    h_n, dk, dv, qk, v = dm.H, dm.DK, dm.DV, dm.QK, dm.V
    rb = lambda b, c: row0 + b * nc + c
    sq = lambda b: jnp.minimum(b, b_n - 1)
    in_specs = [
        pl.BlockSpec((lv, qk), lambda b, c: (rb(b, c), 0)),
        pl.BlockSpec((lv, qk), lambda b, c: (rb(b, c), 1)),
        pl.BlockSpec((lv, v), lambda b, c: (rb(b, c), dm.off_v // v)),
        pl.BlockSpec((lv, v), lambda b, c: (rb(b, c), dm.off_o // v)),
        pl.BlockSpec((lv, LANES), lambda b, c: (rb(b, c), 0)),
        pl.BlockSpec((1, LANES), lambda b, c: (0, 0)),
        pl.BlockSpec((1, v), lambda b, c: (0, 0)),
        pl.BlockSpec((None, None, h_n, dv, dk), lambda b, c: (ls, sq(b), 0, 0, 0)),
        pl.BlockSpec((None, None, h_n, dk), lambda b, c: (ls, sq(b), 0, 0)),
        pl.BlockSpec((None, None, 1, LANES), lambda b, c: (ls, sq(b), 0, 0)),
    ]
    return pl.pallas_call(
        functools.partial(_mlstm_kernel, dm=dm, lv=lv, n_seq=b_n),
        grid=(n_grid, nc),
        in_specs=in_specs,
        out_specs=[pl.BlockSpec((lv, v), lambda b, c: (b * nc + c, 0)),
                   pl.BlockSpec((None, h_n, dv, dk), lambda b, c: (sq(b), 0, 0, 0)),
                   pl.BlockSpec((None, h_n, dk), lambda b, c: (sq(b), 0, 0)),
                   pl.BlockSpec((None, 1, LANES), lambda b, c: (sq(b), 0, 0))],
        out_shape=[jax.ShapeDtypeStruct((n_rows, v), BF16),
                   jax.ShapeDtypeStruct((b_n, h_n, dv, dk), F32),
                   jax.ShapeDtypeStruct((b_n, h_n, dk), F32),
                   jax.ShapeDtypeStruct((b_n, 1, LANES), F32)],
        scratch_shapes=[pltpu.VMEM((h_n, dk, dv + LANES), F32), pltpu.VMEM((1, LANES), F32)],
        compiler_params=_cparams(("arbitrary", "arbitrary"), 40),
        name="mlstm_sample" if sample else "mlstm_prompt",
    )(proj, proj, proj, proj, gates, bias, mhg, c0, n0, m0)


def _sgu_kernel(*refs, dm, lv, n_seq, emit_v):
    g_ref = refs[5]
    seq_ok = pl.program_id(0) < n_seq

    @pl.when(seq_ok)
    def _run():
        _sgu_step(*refs, dm=dm, lv=lv, emit_v=emit_v)

    @pl.when(jnp.logical_not(seq_ok))
    def _fill():
        g_ref[...] = jnp.zeros_like(g_ref)


def _sgu_step(*refs, dm, lv, emit_v):
    z_ref, lng_ref, lnb_ref, ws_ref, bst_ref, g_ref = refs[:6]
    gh, gch = dm.GH, dm.GH // dm.G
    z = z_ref[...].astype(F32)
    u, v = z[:, :gh], z[:, gh:]
    xc = v - jnp.mean(v, axis=-1, keepdims=True)
    vn = xc * lax.rsqrt(jnp.mean(xc * xc, axis=-1, keepdims=True) + EPS) * lng_ref[...] + lnb_ref[...]
    if emit_v:
        refs[6][...] = vn
    row = lax.broadcasted_iota(I32, (lv, lv), 0)
    col = lax.broadcasted_iota(I32, (lv, lv), 1)
    vb = vn.astype(BF16)
    for g in range(dm.G):
        w = jnp.where(col <= row, ws_ref[g, :lv, :lv], 0.0).astype(BF16)
        mixed = _dot(w, vb[:, g * gch:(g + 1) * gch]) + bst_ref[:lv, g:g + 1]
        g_ref[:, g * gch:(g + 1) * gch] = (u[:, g * gch:(g + 1) * gch] * mixed).astype(BF16)


def _sgu(dm, proj, lng, lnb, ws, bst, *, sample):
    if sample:
        b_n, n_grid, nc, lv, row0, n_rows = dm.Bs, ROW_TILE // dm.DS, 1, dm.DS, dm.Tp // dm.DS, ROW_TILE
    else:
        b_n, n_grid, nc, lv, row0, n_rows = dm.Bp, dm.Bp, dm.S // dm.GC, dm.GC, 0, dm.Tp
    gh = dm.GH
    rb = lambda b, c: row0 + b * nc + c
    in_specs = [
        pl.BlockSpec((lv, 2 * gh), lambda b, c: (rb(b, c), dm.off_z // (2 * gh))),
        pl.BlockSpec((1, gh), lambda b, c: (0, 0)),
        pl.BlockSpec((1, gh), lambda b, c: (0, 0)),
        pl.BlockSpec((dm.G, dm.GC, dm.GC), lambda b, c: (0, 0, 0)),
        pl.BlockSpec((dm.GC, LANES), lambda b, c: (0, 0)),
    ]
    out_specs = [pl.BlockSpec((lv, gh), lambda b, c: (b * nc + c, 0))]
    out_shape = [jax.ShapeDtypeStruct((n_rows, gh), BF16)]
    if sample:
        out_specs.append(pl.BlockSpec((None, lv, gh), lambda b, c: (jnp.minimum(b, b_n - 1), 0, 0)))
        out_shape.append(jax.ShapeDtypeStruct((b_n, lv, gh), F32))
    return pl.pallas_call(
        functools.partial(_sgu_kernel, dm=dm, lv=lv, n_seq=b_n, emit_v=sample),
        grid=(n_grid, nc),
        in_specs=in_specs,
        out_specs=out_specs,
        out_shape=out_shape,
        compiler_params=_cparams(("arbitrary", "arbitrary"), 24),
        name="sgu_sample" if sample else "sgu_prompt",
    )(proj, lng, lnb, ws, bst)


def _merge_kernel(ap_ref, as_ref, gp_ref, gs_ref, ga_ref, gb_ref, wa_ref, wb_ref, o_ref):
    is_s = pl.program_id(1) == pl.num_programs(1) - 1
    pa = _dot(jnp.where(is_s, as_ref[...], ap_ref[...]), wa_ref[...])
    pb = _dot(jnp.where(is_s, gs_ref[...], gp_ref[...]), wb_ref[...])
    o_ref[...] = (jax.nn.sigmoid(ga_ref[...].astype(F32)) * pa
                  + jax.nn.sigmoid(gb_ref[...].astype(F32)) * pb).astype(o_ref.dtype)


def _merge(dm, a_p, a_s, g_p, g_s, proj, wa, wb):
    tc, d, v, gh = dm.Tc, dm.D, dm.V, dm.GH
    tm = ROW_TILE
    tn = _pick_tile(d, 1024)
    np_tiles = dm.Tp // tm
    prompt = lambda j, i: (jnp.minimum(i, np_tiles - 1), 0)
    const = lambda j, i: (0, 0)
    blocks = 2 * tm * (v + gh) * 2 + 3 * tm * tn * 2 + (v + gh) * tn * 2
    return pl.pallas_call(
        _merge_kernel,
        grid=(d // tn, tc // tm),
        in_specs=[pl.BlockSpec((tm, v), prompt),
                  pl.BlockSpec((tm, v), const),
                  pl.BlockSpec((tm, gh), prompt),
                  pl.BlockSpec((tm, gh), const),
                  pl.BlockSpec((tm, tn), lambda j, i: (i, dm.off_ga // tn + j)),
                  pl.BlockSpec((tm, tn), lambda j, i: (i, dm.off_gb // tn + j)),
                  pl.BlockSpec((v, tn), lambda j, i: (0, j)),
                  pl.BlockSpec((gh, tn), lambda j, i: (0, j))],
        out_specs=pl.BlockSpec((tm, tn), lambda j, i: (i, j)),
        out_shape=jax.ShapeDtypeStruct((tc, d), BF16),
        compiler_params=_cparams(("arbitrary", "arbitrary"), (2 * blocks + 6 * tm * tn * 4) / MIB + 8),
        name="merge",
    )(a_p, a_s, g_p, g_s, proj, proj, wa, wb)


def _route(logits, eg, ne):
    epg = ne // eg
    lane = lax.broadcasted_iota(I32, logits.shape, 1)
    gmask = lane < eg
    gmax = jnp.max(jnp.where(gmask, logits, NEG), axis=1, keepdims=True)
    gexp = jnp.where(gmask, jnp.exp(jnp.minimum(logits - gmax, 0.0)), 0.0)
    pg = gexp / jnp.sum(gexp, axis=1, keepdims=True)
    p_grp = jnp.max(pg, axis=1, keepdims=True)
    g_sel = jnp.min(jnp.where(gmask & (pg == p_grp), lane, LANES), axis=1, keepdims=True)
    lo = eg + g_sel * epg
    emask = (lane >= lo) & (lane < lo + epg)
    emax = jnp.max(jnp.where(emask, logits, NEG), axis=1, keepdims=True)
    eexp = jnp.where(emask, jnp.exp(jnp.minimum(logits - emax, 0.0)), 0.0)
    pe = eexp / jnp.sum(eexp, axis=1, keepdims=True)
    p1 = jnp.max(jnp.where(emask, pe, -1.0), axis=1, keepdims=True)
    i1 = jnp.min(jnp.where(emask & (pe == p1), lane, LANES), axis=1, keepdims=True)
    rest = emask & (lane != i1)
    p2 = jnp.max(jnp.where(rest, pe, -1.0), axis=1, keepdims=True)
    i2 = jnp.min(jnp.where(rest & (pe == p2), lane, LANES), axis=1, keepdims=True)
    psum = p1 + p2
    eid = jnp.where(lane == 0, i1 - eg, jnp.where(lane == 1, i2 - eg, 0))
    gate = jnp.where(lane == 0, p_grp * (p1 / psum), jnp.where(lane == 1, p_grp * (p2 / psum), 0.0))
    return eid, gate


def _outproj_kernel(m_ref, x_ref, w_ref, g_ref, mod_ref, wr_ref, rb_ref,
                    x1_ref, hp_ref, eid_ref, gate_ref, acc_scr, hi_scr, lo_scr, *, dm):
    i = pl.program_id(0)
    t = jnp.maximum(i - 1, 0)
    is_s = t == dm.n_tiles - 1
    bp = jnp.minimum(t // (dm.S // ROW_TILE), dm.Bp - 1)
    cur, prev = i % 2, (i + 1) % 2
    g = g_ref[...]

    @pl.when(i == 0)
    def _no_previous_tile():
        acc_scr[1] = jnp.zeros(acc_scr.shape[1:], F32)

    def chunk(j):
        rows = pl.ds(j * ROW_CHUNK, ROW_CHUNK)
        idx = jnp.where(is_s, dm.Bp + j, bp)
        x1 = x_ref[rows, :] + mod_ref[GT1, pl.ds(idx, 1), :] * acc_scr[prev, rows, :]
        x1_ref[rows, :] = x1
        h = _rms_mod(x1, g, mod_ref[SC2, pl.ds(idx, 1), :], mod_ref[SH2, pl.ds(idx, 1), :])
        hi, lo = _split_bf16(h)
        hi_scr[rows, :] = hi
        lo_scr[rows, :] = lo
        hp_ref[rows, :] = _pack_pair(h)

    n_chunks = ROW_TILE // ROW_CHUNK
    n_slices = min(dm.D // 256, n_chunks)
    for s in range(n_slices):
        cols = slice(s * dm.D // n_slices, (s + 1) * dm.D // n_slices)
        acc_scr[cur, :, cols] = _dot(m_ref[...], w_ref[:, cols])
        for j in range(s * n_chunks // n_slices, (s + 1) * n_chunks // n_slices):
            chunk(j)
    w_hi, w_lo = _split_bf16(wr_ref[...])
    logits = _dot3(hi_scr[...], lo_scr[...], w_hi, w_lo) + rb_ref[...]
    eid, gate = _route(logits, dm.EG, dm.NE)
    eid_ref[...] = eid
    gate_ref[...] = gate


def _outproj(dm, l, merged, x, w_out, g2, mod, w_router, b_router):
    d, tc = dm.D, dm.Tc
    tile = lambda i: (jnp.maximum(i - 1, 0), 0)
    const = lambda i: (0, 0)
    return pl.pallas_call(
        functools.partial(_outproj_kernel, dm=dm),
        grid=(dm.n_tiles + 1,),
        in_specs=[pl.BlockSpec((ROW_TILE, d), lambda i: (jnp.minimum(i, dm.n_tiles - 1), 0)),
                  pl.BlockSpec((ROW_TILE, d), tile),
                  pl.BlockSpec((d, d), const),
                  pl.BlockSpec((None, 1, d), lambda i: (l, 0, 0)),
                  pl.BlockSpec((None, 6, dm.n_mod, d), lambda i: (l, 0, 0, 0)),
                  pl.BlockSpec((d, LANES), const),
                  pl.BlockSpec((1, LANES), const)],
        out_specs=[pl.BlockSpec((ROW_TILE, d), tile),
                   pl.BlockSpec((ROW_TILE, d // 2), tile),
                   pl.BlockSpec((ROW_TILE, LANES), tile),
                   pl.BlockSpec((ROW_TILE, LANES), tile)],
        out_shape=[jax.ShapeDtypeStruct((tc, d), F32),
                   jax.ShapeDtypeStruct((tc, d // 2), U32),
                   jax.ShapeDtypeStruct((tc, LANES), I32),
                   jax.ShapeDtypeStruct((tc, LANES), F32)],
        scratch_shapes=[pltpu.VMEM((2, ROW_TILE, d), F32), pltpu.VMEM((ROW_TILE, d), BF16),
                        pltpu.VMEM((ROW_TILE, d), BF16)],
        compiler_params=_cparams(("arbitrary",), (4 * d * d + ROW_TILE * d * 44) / MIB + 12),
        name="out_proj",
    )(merged, x, w_out, g2, mod, w_router, b_router)


def _rank_kernel(eid_ref, rank_ref, cnt_ref, run_scr):
    i = pl.program_id(0)

    @pl.when(i == 0)
    def _init():
        run_scr[...] = jnp.zeros_like(run_scr)

    eid = eid_ref[...]
    lane = lax.broadcasted_iota(I32, eid.shape, 1)
    e0 = lane == eid[:, 0:1]
    e1 = lane == eid[:, 1:2]
    hot = (e0 | e1).astype(BF16)
    n = eid.shape[0]
    strict = (lax.broadcasted_iota(I32, (n, n), 1) < lax.broadcasted_iota(I32, (n, n), 0)).astype(BF16)
    before = _dot(strict, hot) + run_scr[...]
    r0 = jnp.sum(jnp.where(e0, before, 0.0), axis=1, keepdims=True)
    r1 = jnp.sum(jnp.where(e1, before, 0.0), axis=1, keepdims=True)
    rank_ref[...] = jnp.where(lane == 0, r0, jnp.where(lane == 1, r1, 0.0)).astype(I32)
    run_scr[...] = run_scr[...] + jnp.sum(hot.astype(F32), axis=0, keepdims=True)
    cnt_ref[...] = run_scr[...].astype(I32)


def _rank(dm, eid):
    tile = lambda i: (i, 0)
    return pl.pallas_call(
        _rank_kernel,
        grid=(dm.n_tiles,),
        in_specs=[pl.BlockSpec((ROW_TILE, LANES), tile)],
        out_specs=[pl.BlockSpec((ROW_TILE, LANES), tile), pl.BlockSpec((1, LANES), lambda i: (0, 0))],
        out_shape=[jax.ShapeDtypeStruct((dm.Tc, LANES), I32), jax.ShapeDtypeStruct((1, LANES), I32)],
        scratch_shapes=[pltpu.VMEM((1, LANES), F32)],
        compiler_params=_cparams(("arbitrary",), 16),
        name="moe_rank",
    )(eid)


def _dispatch_kernel(tail_ref, dest_ref, h_ref, xs_ref, zero_scr, sem):
    @pl.when(pl.program_id(0) == 0)
    def _clear_tails():
        zero_scr[...] = jnp.zeros_like(zero_scr)
        n_e = tail_ref.shape[0]

        def tail_copy(e):
            start = pl.multiple_of(tail_ref[e], MOE_BLOCK)
            return pltpu.make_async_copy(zero_scr, xs_ref.at[pl.ds(start, MOE_BLOCK)], sem)

        for e in range(n_e):
            tail_copy(e).start()
        for e in range(n_e):
            tail_copy(e).wait()

    for r in range(ROW_TILE):
        for k in range(TOP_K):
            pltpu.make_async_copy(h_ref.at[pl.ds(r, 1)], xs_ref.at[pl.ds(dest_ref[TOP_K * r + k], 1)],
                                  sem).start(priority=k)
    for k in range(TOP_K):
        pltpu.make_async_copy(h_ref, xs_ref.at[pl.ds(0, ROW_TILE)], sem).wait()


def _dispatch(dm, tails, dest, hp):
    d2 = dm.D // 2
    return pl.pallas_call(
        _dispatch_kernel,
        grid_spec=pltpu.PrefetchScalarGridSpec(
            num_scalar_prefetch=1,
            grid=(dm.n_tiles,),
            in_specs=[pl.BlockSpec((ROW_TILE * TOP_K,), lambda i, t: (i,), memory_space=pltpu.SMEM),
                      pl.BlockSpec((ROW_TILE, d2), lambda i, t: (i, 0))],
            out_specs=pl.BlockSpec(memory_space=pl.ANY),
            scratch_shapes=[pltpu.VMEM((MOE_BLOCK, d2), U32), pltpu.SemaphoreType.DMA]),
        out_shape=jax.ShapeDtypeStruct((dm.cap + dm.NE * MOE_BLOCK, d2), U32),
        compiler_params=_cparams(("arbitrary",), 16),
        name="moe_dispatch",
    )(tails, dest, hp)


def _expert_kernel(be_ref, nu_ref, xs_ref, wg_ref, wu_ref, wd_ref, ys_ref, wg_s, wu_s, wd_s):
    j = pl.program_id(0)
    changed = (j == 0) | (be_ref[j] != be_ref[jnp.maximum(j - 1, 0)])

    @pl.when(changed)
    def _cast():
        wg_s[...] = wg_ref[...].astype(BF16)
        wu_s[...] = wu_ref[...].astype(BF16)
        wd_s[...] = wd_ref[...].astype(BF16)

    @pl.when(j < nu_ref[0])
    def _compute():
        d2 = xs_ref.shape[1]
        lo, hi = _unpack_pair(xs_ref[...])
        lo, hi = lo.astype(BF16), hi.astype(BF16)
        hg = _dot(lo, wg_s[:d2, :]) + _dot(hi, wg_s[d2:, :])
        hu = _dot(lo, wu_s[:d2, :]) + _dot(hi, wu_s[d2:, :])
        act = (hg * jax.nn.sigmoid(hg) * hu).astype(BF16)
        ys_ref[...] = _pack_pair(_dot(act, wd_s[...]))


def _experts(dm, l, blk_e, n_used, xs, w_gate, w_up, w_down):
    d, de, d2 = dm.D, dm.DE, dm.D // 2
    nb = dm.cap // MOE_BLOCK
    wmap = lambda j, be, nu: (l, be[j], 0, 0)
    rows = lambda j, be, nu: (jnp.minimum(j, nu[0] - 1), 0)
    return pl.pallas_call(
        _expert_kernel,
        grid_spec=pltpu.PrefetchScalarGridSpec(
            num_scalar_prefetch=2,
            grid=(nb,),
            in_specs=[pl.BlockSpec((MOE_BLOCK, d2), rows),
                      pl.BlockSpec((None, None, d, de), wmap),
                      pl.BlockSpec((None, None, d, de), wmap),
                      pl.BlockSpec((None, None, de, d), wmap)],
            out_specs=pl.BlockSpec((MOE_BLOCK, d2), rows),
            scratch_shapes=[pltpu.VMEM((d, de), BF16), pltpu.VMEM((d, de), BF16), pltpu.VMEM((de, d), BF16)]),
        out_shape=jax.ShapeDtypeStruct((dm.cap, d2), U32),
        compiler_params=_cparams(("arbitrary",), (3 * d * de * (2 * 4 + 2) + MOE_BLOCK * d * 24) / MIB + 8),
        name="moe_experts",
    )(blk_e, n_used, xs, w_gate, w_up, w_down)


def _combine_kernel(*refs, dm, last):
    if last:
        dest_ref, destn_ref, ys_ref, x_ref, gate_ref, mod_ref, g_ref, yp_ref, ysm_ref, ybuf, y_scr, sem = refs
    else:
        (dest_ref, destn_ref, ys_ref, x_ref, gate_ref, mod_ref, g_ref, modn_ref, w_ref,
         o_ref, h_ref, p_ref, ybuf, lo_scr, sem) = refs
    i = pl.program_id(0)
    is_s = i == dm.n_tiles - 1
    bp = jnp.minimum(i // (dm.S // ROW_TILE), dm.Bp - 1)
    d2 = dm.D // 2
    g = g_ref[...]
    slot = i % 2

    def issue(idx_ref, to_slot, r0, n):
        for r in range(n):
            for k in range(TOP_K):
                pltpu.make_async_copy(ys_ref.at[pl.ds(idx_ref[TOP_K * (r0 + r) + k], 1)],
                                      ybuf.at[to_slot, k, pl.ds(r0 + r, 1)], sem.at[to_slot]).start(priority=k)

    def wait_slot(s):
        for k in range(TOP_K):
            pltpu.make_async_copy(ys_ref.at[pl.ds(0, ROW_TILE)], ybuf.at[s, k], sem.at[s]).wait()

    @pl.when(i == 0)
    def _first_tile():
        issue(dest_ref, 0, 0, ROW_TILE)

    wait_slot(slot)

    def chunk(j):
        rows = pl.ds(j * ROW_CHUNK, ROW_CHUNK)
        issue(destn_ref, 1 - slot, j * ROW_CHUNK, ROW_CHUNK)
        idx = jnp.where(is_s, dm.Bp + j, bp)
        gt = mod_ref[GT2, pl.ds(idx, 1), :]
        gates = gate_ref[rows, :]
        g0, g1 = gates[:, 0:1], gates[:, 1:2]
        lo0, hi0 = _unpack_pair(ybuf[slot, 0, rows, :])
        lo1, hi1 = _unpack_pair(ybuf[slot, 1, rows, :])
        x_lo = x_ref[rows, :d2] + gt[:, :d2] * (g0 * lo0 + g1 * lo1)
        x_hi = x_ref[rows, d2:] + gt[:, d2:] * (g0 * hi0 + g1 * hi1)
        ms = (jnp.sum(x_lo * x_lo, axis=-1, keepdims=True)
              + jnp.sum(x_hi * x_hi, axis=-1, keepdims=True)) * (1.0 / dm.D)
        rs = lax.rsqrt(ms + EPS)
        if last:
            y_scr[rows, :d2] = x_lo * rs * g[:, :d2]
            y_scr[rows, d2:] = x_hi * rs * g[:, d2:]
        else:
            o_ref[rows, :d2] = x_lo
            o_ref[rows, d2:] = x_hi
            sc = modn_ref[SC1, pl.ds(idx, 1), :]
            sh = modn_ref[SH1, pl.ds(idx, 1), :]
            for half, xh in ((slice(0, d2), x_lo), (slice(d2, dm.D), x_hi)):
                h = xh * rs * g[:, half] * (1.0 + sc[:, half]) + sh[:, half]
                hi, lo = _split_bf16(h)
                h_ref[rows, half] = hi
                lo_scr[rows, half] = lo

    for j in range(ROW_TILE // ROW_CHUNK):
        chunk(j)

    @pl.when(is_s)
    def _drain():
        wait_slot(1 - slot)

    if last:
        @pl.when(is_s)
        def _sample_rows():
            ysm_ref[...] = y_scr[...]

        @pl.when(jnp.logical_not(is_s))
        def _prompt_rows():
            yp_ref[...] = y_scr[...]
    else:
        w_hi, w_lo = _gate_weight(w_ref)
        p_ref[...] = _dot3(h_ref[...], lo_scr[...], w_hi, w_lo)


def _combine(dm, l, dest, ys, x1, gate, mod, g_next, w_in=None):
    d = dm.D
    last = w_in is None
    tile = lambda i: (i, 0)
    np_tiles = dm.Tp // ROW_TILE
    in_specs = [pl.BlockSpec((ROW_TILE * TOP_K,), lambda i: (i,), memory_space=pltpu.SMEM),
                pl.BlockSpec((ROW_TILE * TOP_K,), lambda i: (jnp.minimum(i + 1, dm.n_tiles - 1),),
                             memory_space=pltpu.SMEM),
                pl.BlockSpec(memory_space=pl.ANY),
                pl.BlockSpec((ROW_TILE, d), tile),
                pl.BlockSpec((ROW_TILE, LANES), tile),
                pl.BlockSpec((None, 6, dm.n_mod, d), lambda i: (l, 0, 0, 0))]
    scratch = [pltpu.VMEM((2, TOP_K, ROW_TILE, d // 2), U32)]
    if last:
        args = (dest, dest, ys, x1, gate, mod, g_next)
        in_specs.append(pl.BlockSpec((1, d), lambda i: (0, 0)))
        out_specs = [pl.BlockSpec((ROW_TILE, d), lambda i: (jnp.minimum(i, np_tiles - 1), 0)),
                     pl.BlockSpec((ROW_TILE, d), lambda i: (0, 0))]
        out_shape = [jax.ShapeDtypeStruct((dm.Tp, d), F32), jax.ShapeDtypeStruct((ROW_TILE, d), F32)]
        scratch.append(pltpu.VMEM((ROW_TILE, d), F32))
    else:
        args = (dest, dest, ys, x1, gate, mod, g_next, mod, w_in)
        in_specs += [pl.BlockSpec((None, 1, d), lambda i: (l + 1, 0, 0)),
                     pl.BlockSpec((None, 6, dm.n_mod, d), lambda i: (l + 1, 0, 0, 0)),
                     pl.BlockSpec((None, 2 * dm.H, d), lambda i: (l + 1, dm.off_z // (2 * dm.H), 0))]
        out_specs = [pl.BlockSpec((ROW_TILE, d), tile), pl.BlockSpec((ROW_TILE, d), tile),
                     pl.BlockSpec((ROW_TILE, LANES), tile)]
        out_shape = [jax.ShapeDtypeStruct((dm.Tc, d), F32), jax.ShapeDtypeStruct((dm.Tc, d), BF16),
                     jax.ShapeDtypeStruct((dm.Tc, LANES), F32)]
        scratch.append(pltpu.VMEM((ROW_TILE, d), BF16))
    scratch.append(pltpu.SemaphoreType.DMA((2,)))
    return pl.pallas_call(
        functools.partial(_combine_kernel, dm=dm, last=last),
        grid=(dm.n_tiles,),
        in_specs=in_specs,
        out_specs=out_specs,
        out_shape=out_shape,
        scratch_shapes=scratch,
        compiler_params=_cparams(("arbitrary",), ROW_TILE * d * 40 / MIB + 16),
        name="moe_combine_last" if last else "moe_combine",
    )(*args)


def _pad_lanes(x, n=LANES):
    return jnp.pad(x, [(0, 0)] * (x.ndim - 1) + [(0, n - x.shape[-1])])


def _moe_plan(dm, eid, rank, counts):
    counts = counts[0, :dm.NE]
    padded = (counts + MOE_BLOCK - 1) // MOE_BLOCK * MOE_BLOCK
    pad_end = jnp.cumsum(padded)
    pad_start = pad_end - padded
    dest = (pad_start[eid[:, :TOP_K]] + rank[:, :TOP_K]).reshape(-1).astype(I32)
    nb = dm.cap // MOE_BLOCK
    n_used = (pad_end[-1] // MOE_BLOCK).astype(I32)
    starts = jnp.minimum(jnp.arange(nb, dtype=I32), n_used - 1) * MOE_BLOCK
    blk_e = jnp.minimum(jnp.sum(pad_end[None, :] <= starts[:, None], axis=1), dm.NE - 1).astype(I32)
    spare = dm.cap + jnp.arange(dm.NE, dtype=I32) * MOE_BLOCK
    tails = jnp.where(padded > 0, pad_end - MOE_BLOCK, spare).astype(I32)
    return dest, blk_e, n_used.reshape(1), tails


def kernel(x_prompt, x_sample, c_prompt, c_sample, state_mlstm_C, state_mlstm_n, state_mlstm_m, w_ada, b_ada, norm1_g, norm2_g, w_in, b_igate, b_fgate, mh_norm_g, sgu_ln_g, sgu_ln_b, w_spatial, b_spatial, w_branch_a, w_branch_b, w_out, w_router_group, b_router_group, w_router_expert, b_router_expert, w_expert_gate, w_expert_up, w_expert_down, final_norm_g):
    bp, s, d = x_prompt.shape
    bs, ds_, _ = x_sample.shape
    depth = w_in.shape[0]
    h_n = b_igate.shape[1]
    dv, dk = state_mlstm_C.shape[-2:]
    qk, v = h_n * dk, h_n * dv
    gh = (w_in.shape[2] - (2 * qk + 2 * v + 2 * h_n + 2 * d)) // 2
    dm = Dims(Bp=bp, S=s, Bs=bs, DS=ds_, D=d, DEPTH=depth, H=h_n, DK=dk, DV=dv, GH=gh,
              G=w_spatial.shape[1], GC=w_spatial.shape[2], EG=w_router_group.shape[-1],
              NE=w_router_expert.shape[-1], DE=w_expert_gate.shape[-1])
    assert dm.DS == ROW_CHUNK and dm.Ts <= ROW_TILE and s % ROW_TILE == 0 and s % MLSTM_CHUNK == 0
    assert dm.EG + dm.NE <= LANES and 2 * h_n <= LANES and (2 * h_n) % 8 == 0 and dm.Bp + dm.Bs <= 32
    w_in = jnp.swapaxes(w_in, 1, 2)
    assert dm.off_v % v == 0 and dm.off_o % v == 0 and dm.off_z % (2 * gh) == 0 and d % 256 == 0

    c_all = jnp.concatenate([c_prompt, c_sample, jnp.zeros((32 - bp - bs, d), F32)], axis=0)
    mod = _adaln(c_all, w_ada, b_ada)[:, :bp + bs].reshape(depth, bp + bs, 6, d)
    mod = jnp.pad(mod, ((0, 0), (0, dm.n_mod - bp - bs), (0, 0), (0, 0))).transpose(0, 2, 1, 3)

    xs_rows = jnp.pad(x_sample.reshape(dm.Ts, d), ((0, ROW_TILE - dm.Ts), (0, 0)))
    x = (x_prompt.reshape(dm.Tp, d), xs_rows)
    zeros_c = jnp.zeros((1, bp, h_n, dv, dk), F32)
    zeros_n = jnp.zeros((1, bp, h_n, dk), F32)
    zeros_m = jnp.zeros((1, bp, 1, LANES), F32)
    m_in = _pad_lanes(state_mlstm_m)[:, :, None, :]
    g1 = norm1_g.reshape(depth, 1, d)
    out_c_p, out_n_p, out_m_p, out_c_s, out_n_s, out_m_s, out_v = [], [], [], [], [], [], []

    for l in range(depth):
        if l == 0:
            hb, gates, x = _norm_proj(dm, l, x, g1, mod, w_in, SC1, SH1)
        proj = _in_proj(dm, l, hb, w_in)

        bias = _pad_lanes(jnp.concatenate([b_igate[l], b_fgate[l]])[None, :])
        mhg = mh_norm_g[l][None, :]
        a_p, c_p, n_p, m_p = _mlstm(dm, proj, gates, bias, mhg, zeros_c, zeros_n, zeros_m, 0, sample=False)
        a_s, c_s, n_s, m_s = _mlstm(dm, proj, gates, bias, mhg, state_mlstm_C, state_mlstm_n, m_in, l,
                                    sample=True)
        lng, lnb = sgu_ln_g[l][None, :], sgu_ln_b[l][None, :]
        bst = _pad_lanes(b_spatial[l].T)
        (g_p,) = _sgu(dm, proj, lng, lnb, w_spatial[l], bst, sample=False)
        g_s, v_rows = _sgu(dm, proj, lng, lnb, w_spatial[l], bst, sample=True)

        merged = _merge(dm, a_p, a_s, g_p, g_s, proj, w_branch_a[l].astype(BF16), w_branch_b[l].astype(BF16))
        w_router = _pad_lanes(jnp.concatenate([w_router_group[l], w_router_expert[l]], axis=1))
        b_router = _pad_lanes(jnp.concatenate([b_router_group[l], b_router_expert[l]])[None, :])
        x1, hp, eid, gate = _outproj(dm, l, merged, x, w_out[l].astype(BF16), norm2_g.reshape(depth, 1, d),
                                     mod, w_router, b_router)

        rank, counts = _rank(dm, eid)
        dest, blk_e, n_used, tails = _moe_plan(dm, eid, rank, counts)
        xs = _dispatch(dm, tails, dest, hp)
        ys = _experts(dm, l, blk_e, n_used, xs, w_expert_gate, w_expert_up, w_expert_down)
        if l + 1 < depth:
            x, hb, gates = _combine(dm, l, dest, ys, x1, gate, mod, g1, w_in)
        else:
            y_p, y_s = _combine(dm, l, dest, ys, x1, gate, mod, final_norm_g[None, :])

        out_c_p.append(c_p)
        out_n_p.append(n_p)
        out_m_p.append(m_p[:, 0, :h_n])
        out_c_s.append(c_s)
        out_n_s.append(n_s)
        out_m_s.append(m_s[:, 0, :h_n])
        out_v.append(v_rows)

    return (y_p.reshape(bp, s, d), y_s[:dm.Ts].reshape(bs, ds_, d),
            jnp.stack(out_c_p), jnp.stack(out_n_p), jnp.stack(out_m_p),
            jnp.stack(out_c_s), jnp.stack(out_n_s), jnp.stack(out_m_s), jnp.stack(out_v))
```

```python
import functools
from typing import NamedTuple

import jax
import jax.numpy as jnp
from jax import lax
from jax.experimental import pallas as pl
from jax.experimental.pallas import tpu as pltpu

F32, BF16, U32, I32 = jnp.float32, jnp.bfloat16, jnp.uint32, jnp.int32
EPS = 1e-6
NEG = -1e30
LANES = 128
ROW_TILE = 512
ROW_CHUNK = 16
MLSTM_CHUNK = 256
MOE_BLOCK = 512
TOP_K = 2
CHUNK_UNROLL = 2
SGU_CHUNKS_PER_STEP = 4
MIB = 1024 * 1024
SH1, SC1, GT1, SH2, SC2, GT2 = range(6)


class Dims(NamedTuple):
    Bp: int
    S: int
    Bs: int
    DS: int
    D: int
    DEPTH: int
    H: int
    DK: int
    DV: int
    GH: int
    G: int
    GC: int
    EG: int
    NE: int
    DE: int

    @property
    def QK(self):
        return self.H * self.DK

    @property
    def V(self):
        return self.H * self.DV

    @property
    def Tp(self):
        return self.Bp * self.S

    @property
    def Ts(self):
        return self.Bs * self.DS

    @property
    def Tc(self):
        return self.Tp + ROW_TILE

    @property
    def n_tiles(self):
        return self.Tc // ROW_TILE

    @property
    def n_mod(self):
        return self.Bp + ROW_TILE // ROW_CHUNK

    @property
    def off_v(self):
        return 2 * self.QK

    @property
    def off_o(self):
        return 2 * self.QK + self.V

    @property
    def off_z(self):
        return 2 * self.QK + 2 * self.V

    @property
    def off_ga(self):
        return self.off_z + 2 * self.GH

    @property
    def off_gb(self):
        return self.off_ga + self.D

    @property
    def n_main(self):
        return self.off_gb + self.D

    @property
    def cap(self):
        tk = self.Tc * TOP_K
        return -(-(tk + self.NE * (MOE_BLOCK - 1)) // MOE_BLOCK) * MOE_BLOCK


def _cparams(semantics, vmem_mib):
    return pltpu.CompilerParams(dimension_semantics=semantics, vmem_limit_bytes=int(vmem_mib * MIB))


def _pick_tile(n, cap, unit=256):
    best = None
    for t in range(unit, min(n, cap) + 1, unit):
        if n % t == 0:
            best = t
    assert best is not None, (n, cap)
    return best


def _dot(a, b):
    return jnp.dot(a, b, preferred_element_type=F32)


def _split_bf16(x):
    hi = x.astype(BF16)
    lo = (x - hi.astype(F32)).astype(BF16)
    return hi, lo


def _dot3(a_hi, a_lo, w_hi, w_lo):
    return _dot(a_hi, w_hi) + _dot(a_hi, w_lo) + _dot(a_lo, w_hi)


def _gate_weight(w_ref):
    rows, d = w_ref.shape
    w = jnp.concatenate([w_ref[...], jnp.zeros((LANES - rows, d), F32)], axis=0).T
    return _split_bf16(w)


def _rms_mod(x, g, sc, sh):
    y = x * lax.rsqrt(jnp.mean(x * x, axis=-1, keepdims=True) + EPS) * g
    return y * (1.0 + sc) + sh


def _pack_pair(x):
    d2 = x.shape[-1] // 2
    bits = lax.bitcast_convert_type(x.astype(BF16).astype(F32), U32)
    return (bits[:, :d2] >> 16) | (bits[:, d2:] & jnp.uint32(0xFFFF0000))


def _unpack_pair(p):
    lo = lax.bitcast_convert_type(p << 16, F32)
    hi = lax.bitcast_convert_type(p & jnp.uint32(0xFFFF0000), F32)
    return lo, hi


def _adaln_kernel(c_ref, w_ref, b_ref, o_ref):
    c = c_ref[...]
    s = (c * jax.nn.sigmoid(c)).astype(BF16)
    o_ref[...] = _dot(s, w_ref[...].astype(BF16)) + b_ref[...]


def _adaln(c_all, w_ada, b_ada):
    depth, d, n6 = w_ada.shape
    r = c_all.shape[0]
    tn = _pick_tile(n6, 1024)
    return pl.pallas_call(
        _adaln_kernel,
        grid=(depth, n6 // tn),
        in_specs=[pl.BlockSpec((r, d), lambda l, j: (0, 0)),
                  pl.BlockSpec((None, d, tn), lambda l, j: (l, 0, j)),
                  pl.BlockSpec((None, 1, tn), lambda l, j: (l, 0, j))],
        out_specs=pl.BlockSpec((None, r, tn), lambda l, j: (l, 0, j)),
        out_shape=jax.ShapeDtypeStruct((depth, r, n6), F32),
        compiler_params=_cparams(("arbitrary", "arbitrary"), 2 * d * tn * 4 / MIB + 3 * d * tn * 2 / MIB + 8),
        name="adaln",
    )(c_all, w_ada, b_ada.reshape(depth, 1, n6))


def _norm_proj_kernel(*refs, first, dm, k_sc, k_sh):
    if first:
        xp_ref, xs_ref, g_ref, mod_ref, w_ref, h_ref, p_ref, xc_ref, lo_scr = refs
    else:
        x_ref, g_ref, mod_ref, w_ref, h_ref, p_ref, lo_scr = refs
    i = pl.program_id(0)
    is_s = i == dm.n_tiles - 1
    bp = jnp.minimum(i // (dm.S // ROW_TILE), dm.Bp - 1)
    g = g_ref[...]

    def chunk(j, carry):
        rows = pl.ds(pl.multiple_of(j * ROW_CHUNK, ROW_CHUNK), ROW_CHUNK)
        if first:
            x = jnp.where(is_s, xs_ref[rows, :], xp_ref[rows, :])
            xc_ref[rows, :] = x
        else:
            x = x_ref[rows, :]
        idx = jnp.where(is_s, dm.Bp + j, bp)
        h = _rms_mod(x, g, mod_ref[k_sc, pl.ds(idx, 1), :], mod_ref[k_sh, pl.ds(idx, 1), :])
        hi, lo = _split_bf16(h)
        h_ref[rows, :] = hi
        lo_scr[rows, :] = lo
        return carry

    lax.fori_loop(0, ROW_TILE // ROW_CHUNK, chunk, 0, unroll=CHUNK_UNROLL)
    w_hi, w_lo = _gate_weight(w_ref)
    p_ref[...] = _dot3(h_ref[...], lo_scr[...], w_hi, w_lo)


def _norm_proj(dm, l, x_in, g, mod, w_in, k_sc, k_sh):
    d, tc, nt = dm.D, dm.Tc, dm.n_tiles
    first = isinstance(x_in, tuple)
    np_tiles = dm.Tp // ROW_TILE
    tile = lambda i: (i, 0)
    if first:
        x_args = list(x_in)
        x_specs = [pl.BlockSpec((ROW_TILE, d), lambda i: (jnp.minimum(i, np_tiles - 1), 0)),
                   pl.BlockSpec((ROW_TILE, d), lambda i: (0, 0))]
    else:
        x_args = [x_in]
        x_specs = [pl.BlockSpec((ROW_TILE, d), tile)]
    out_shape = [jax.ShapeDtypeStruct((tc, d), BF16), jax.ShapeDtypeStruct((tc, LANES), F32)]
    out_specs = [pl.BlockSpec((ROW_TILE, d), tile), pl.BlockSpec((ROW_TILE, LANES), tile)]
    if first:
        out_shape.append(jax.ShapeDtypeStruct((tc, d), F32))
        out_specs.append(pl.BlockSpec((ROW_TILE, d), tile))
    return pl.pallas_call(
        functools.partial(_norm_proj_kernel, first=first, dm=dm, k_sc=k_sc, k_sh=k_sh),
        grid=(nt,),
        in_specs=x_specs + [pl.BlockSpec((None, 1, d), lambda i: (l, 0, 0)),
                            pl.BlockSpec((None, 6, dm.n_mod, d), lambda i: (l, 0, 0, 0)),
                            pl.BlockSpec((None, 2 * dm.H, d), lambda i: (l, dm.off_z // (2 * dm.H), 0))],
        out_specs=out_specs,
        out_shape=out_shape,
        scratch_shapes=[pltpu.VMEM((ROW_TILE, d), BF16)],
        compiler_params=_cparams(("arbitrary",), ROW_TILE * d * 40 / MIB + 16),
        name="norm_proj",
    )(*x_args, g, mod, w_in)


def _in_proj_kernel(a_ref, w_ref, wn_ref, o_ref, w_scr, *, n_aligned, shift):
    j, i = pl.program_id(0), pl.program_id(1)

    groups = w_ref.shape[0] // LANES

    @pl.when((i == 0) & (j < n_aligned))
    def _cast():
        for g in range(groups):
            w_scr[:, g * LANES:(g + 1) * LANES] = w_ref[g * LANES:(g + 1) * LANES, :].T.astype(BF16)

    @pl.when((i == 0) & (j >= n_aligned))
    def _cast_shifted():
        for g in range(groups):
            if g + 1 < groups:
                rows = w_ref[g * LANES + shift:(g + 1) * LANES + shift, :]
            else:
                rows = jnp.concatenate([w_ref[g * LANES + shift:, :], wn_ref[...]], axis=0)
            w_scr[:, g * LANES:(g + 1) * LANES] = rows.T.astype(BF16)

    o_ref[...] = _dot(a_ref[...], w_scr[...]).astype(o_ref.dtype)


def _in_proj(dm, l, a, w_in_t):
    m, k = a.shape
    off_if, n = dm.off_z, dm.n_main
    shift = 2 * dm.H
    tn = max(t for t in range(LANES, 1024 + 1, LANES) if off_if % t == 0 and n % t == 0)
    tm = _pick_tile(m, 1536)
    vmem = (2 * tm * k * 2 + 2 * k * (tn + shift) * 4 + k * tn * 2 + 2 * tm * tn * 2 + 2 * tm * tn * 4) / MIB + 8
    return pl.pallas_call(
        functools.partial(_in_proj_kernel, n_aligned=off_if // tn, shift=shift),
        grid=(n // tn, m // tm),
        in_specs=[pl.BlockSpec((tm, k), lambda j, i: (i, 0)),
                  pl.BlockSpec((None, tn, k), lambda j, i: (l, j, 0)),
                  pl.BlockSpec((None, shift, k), lambda j, i: (l, (j + 1) * (tn // shift), 0))],
        out_specs=pl.BlockSpec((tm, tn), lambda j, i: (i, j)),
        out_shape=jax.ShapeDtypeStruct((m, n), BF16),
        scratch_shapes=[pltpu.VMEM((k, tn), BF16)],
        compiler_params=_cparams(("arbitrary", "arbitrary"), vmem),
        name="in_proj",
    )(a, w_in_t, w_in_t)


def _pad_rows(x, rows):
    if x.shape[0] == rows:
        return x
    return jnp.concatenate([x, jnp.zeros((rows - x.shape[0], x.shape[1]), x.dtype)], axis=0)


def _mlstm_kernel(*refs, dm, lv, n_seq):
    a_ref = refs[10]
    seq_ok = pl.program_id(0) < n_seq

    @pl.when(seq_ok)
    def _run():
        _mlstm_step(*refs, dm=dm, lv=lv)

    @pl.when(jnp.logical_not(seq_ok))
    def _fill():
        a_ref[...] = jnp.zeros_like(a_ref)


def _mlstm_step(q_ref, k_ref, v_ref, o_ref, gt_ref, bias_ref, mhg_ref, c0_ref, n0_ref, m0_ref,
                a_ref, cout_ref, nout_ref, mout_ref, st_scr, m_scr, *, dm, lv):
    h_n, dk, dv = dm.H, dm.DK, dm.DV
    L = MLSTM_CHUNK
    c = pl.program_id(1)

    @pl.when(c == 0)
    def _init():
        for h in range(h_n):
            st_scr[h, :, :dv] = c0_ref[h].T
            st_scr[h, :, dv:] = jnp.broadcast_to(n0_ref[h:h + 1, :], (LANES, dk)).T
        m_scr[...] = m0_ref[...]

    row = lax.broadcasted_iota(I32, (L, L), 0)
    col = lax.broadcasted_iota(I32, (L, L), 1)
    causal = col <= row
    lane = lax.broadcasted_iota(I32, (L, LANES), 1)
    rowl = lax.broadcasted_iota(I32, (L, LANES), 0)

    xg = _pad_rows(gt_ref[...], L) + bias_ref[...]
    f_log = jnp.minimum(xg, 0.0) - jnp.log1p(jnp.exp(-jnp.abs(xg)))
    gl = jnp.where(lane < h_n, xg, f_log)
    gl = jnp.where(rowl < lv, gl, 0.0)
    tri = causal.astype(BF16)
    g_hi = gl.astype(BF16)
    r1 = gl - g_hi.astype(F32)
    g_mid = r1.astype(BF16)
    g_lo = (r1 - g_mid.astype(F32)).astype(BF16)
    cum = _dot(tri, g_hi) + _dot(tri, g_mid) + _dot(tri, g_lo)
    b_all = pltpu.roll(cum, LANES - h_n, axis=1)
    a_all = jnp.where(rowl < lv, gl - b_all, NEG)

    scale = dk ** -0.5
    kf = _pad_rows(k_ref[...], L).astype(F32) * scale
    kt_all = kf.T.astype(BF16)
    q_all = _pad_rows(q_ref[...], L)
    v_all = _pad_rows(v_ref[...], L)
    a_rows = a_all.T
    m_old = m_scr[...]
    m_new = m_old
    lane1 = lax.broadcasted_iota(I32, (1, LANES), 1)
    ones_blk = jnp.ones((L, LANES), BF16)
    v_blocks = dv // LANES

    for h in range(h_n):
        a_row = a_rows[h:h + 1, :]
        a_rep = jnp.broadcast_to(a_all[:, h:h + 1], (L, LANES))
        b_rep = jnp.broadcast_to(b_all[:, h:h + 1], (L, LANES))
        m_prev = m_old[:, h:h + 1]
        big_m = jnp.maximum(jnp.max(jnp.where(causal, a_row, NEG), axis=1, keepdims=True), m_prev)
        d_w = jnp.where(causal, jnp.exp(jnp.minimum(a_row - big_m, 0.0)), 0.0)
        inter = jnp.exp(m_prev - big_m)
        qh = q_all[:, h * dk:(h + 1) * dk]
        v_parts = [v_all[:, h * dv + j * LANES:h * dv + (j + 1) * LANES] for j in range(v_blocks)]
        kt = kt_all[h * dk:(h + 1) * dk, :]
        sd = (_dot(qh, kt) * d_w).astype(BF16)
        st = st_scr[h]
        x = _dot(sd, jnp.concatenate(v_parts + [ones_blk], axis=1)) + inter * _dot(qh, st.astype(BF16))
        den = x[:, dv:]
        inv = 1.0 / jnp.maximum(jnp.abs(den), jnp.exp(-(b_rep + big_m)))
        sq = jnp.sum(x[:, :dv] * x[:, :dv], axis=1, keepdims=True) * (1.0 / dv)
        sc = (inv * lax.rsqrt(inv * inv * sq + EPS))[:lv]
        for j in range(v_blocks):
            cols = slice(h * dv + j * LANES, h * dv + (j + 1) * LANES)
            gate_o = jax.nn.sigmoid(o_ref[:, cols].astype(F32))
            a_ref[:, cols] = (x[:lv, j * LANES:(j + 1) * LANES] * sc * (mhg_ref[:, cols] * gate_o)).astype(BF16)
        m_last = big_m[L - 1:L, :]
        decay = jnp.exp(m_prev - m_last)
        w_rep = jnp.exp(a_rep - m_last)
        vw = jnp.concatenate([(p.astype(F32) * w_rep).astype(BF16) for p in v_parts] + [w_rep.astype(BF16)], axis=1)
        st_scr[h] = decay * st + _dot(kt, vw)
        m_new = jnp.where(lane1 == h, b_all[L - 1:L, h:h + 1] + m_last, m_new)
    m_scr[...] = m_new

    @pl.when(c == pl.num_programs(1) - 1)
    def _fin():
        for h in range(h_n):
            cout_ref[h] = st_scr[h, :, :dv].T
            nout_ref[h:h + 1, :] = st_scr[h, :, dv:].T[0:1, :]
        mout_ref[...] = m_scr[...]


def _mlstm(dm, proj, gates, bias, mhg, c0, n0, m0, ls, *, sample):
    if sample:
        b_n, n_grid, nc, lv, row0, n_rows = dm.Bs, ROW_TILE // dm.DS, 1, dm.DS, dm.Tp // dm.DS, ROW_TILE
    else:
        b_n, n_grid, nc, lv, row0, n_rows = dm.Bp, dm.Bp, dm.S // MLSTM_CHUNK, MLSTM_CHUNK, 0, dm.Tp
    h_n, dk, dv, qk, v = dm.H, dm.DK, dm.DV, dm.QK, dm.V
    rb = lambda b, c: row0 + b * nc + c
    sq = lambda b: jnp.minimum(b, b_n - 1)
    in_specs = [
        pl.BlockSpec((lv, qk), lambda b, c: (rb(b, c), 0)),
        pl.BlockSpec((lv, qk), lambda b, c: (rb(b, c), 1)),
        pl.BlockSpec((lv, v), lambda b, c: (rb(b, c), dm.off_v // v)),
        pl.BlockSpec((lv, v), lambda b, c: (rb(b, c), dm.off_o // v)),
        pl.BlockSpec((lv, LANES), lambda b, c: (rb(b, c), 0)),
        pl.BlockSpec((1, LANES), lambda b, c: (0, 0)),
        pl.BlockSpec((1, v), lambda b, c: (0, 0)),
        pl.BlockSpec((None, None, h_n, dv, dk), lambda b, c: (ls, sq(b), 0, 0, 0)),
        pl.BlockSpec((None, None, h_n, dk), lambda b, c: (ls, sq(b), 0, 0)),
        pl.BlockSpec((None, None, 1, LANES), lambda b, c: (ls, sq(b), 0, 0)),
    ]
    return pl.pallas_call(
        functools.partial(_mlstm_kernel, dm=dm, lv=lv, n_seq=b_n),
        grid=(n_grid, nc),
        in_specs=in_specs,
        out_specs=[pl.BlockSpec((lv, v), lambda b, c: (b * nc + c, 0)),
                   pl.BlockSpec((None, h_n, dv, dk), lambda b, c: (sq(b), 0, 0, 0)),
                   pl.BlockSpec((None, h_n, dk), lambda b, c: (sq(b), 0, 0)),
                   pl.BlockSpec((None, 1, LANES), lambda b, c: (sq(b), 0, 0))],
        out_shape=[jax.ShapeDtypeStruct((n_rows, v), BF16),
                   jax.ShapeDtypeStruct((b_n, h_n, dv, dk), F32),
                   jax.ShapeDtypeStruct((b_n, h_n, dk), F32),
                   jax.ShapeDtypeStruct((b_n, 1, LANES), F32)],
        scratch_shapes=[pltpu.VMEM((h_n, dk, dv + LANES), F32), pltpu.VMEM((1, LANES), F32)],
        compiler_params=_cparams(("arbitrary", "arbitrary"), 40),
        name="mlstm_sample" if sample else "mlstm_prompt",
    )(proj, proj, proj, proj, gates, bias, mhg, c0, n0, m0)


def _sgu_kernel(*refs, dm, lv, n_seq, emit_v):
    g_ref = refs[5]
    seq_ok = pl.program_id(0) < n_seq

    @pl.when(seq_ok)
    def _run():
        _sgu_step(*refs, dm=dm, lv=lv, emit_v=emit_v)

    @pl.when(jnp.logical_not(seq_ok))
    def _fill():
        g_ref[...] = jnp.zeros_like(g_ref)


def _sgu_step(*refs, dm, lv, emit_v):
    z_ref, lng_ref, lnb_ref, ws_ref, bst_ref, g_ref = refs[:6]
    gh, gch = dm.GH, dm.GH // dm.G
    z = jax.nn.gelu(z_ref[...].astype(F32))
    u, v = z[:, :gh], z[:, gh:]
    xc = v - jnp.mean(v, axis=-1, keepdims=True)
    vn = xc * lax.rsqrt(jnp.mean(xc * xc, axis=-1, keepdims=True) + EPS) * lng_ref[...] + lnb_ref[...]
    if emit_v:
        refs[6][...] = vn
    row = lax.broadcasted_iota(I32, (lv, lv), 0)
    col = lax.broadcasted_iota(I32, (lv, lv), 1)
    vb = vn.astype(BF16)
    for g in range(dm.G):
        w = jnp.where(col <= row, ws_ref[g, :lv, :lv], 0.0).astype(BF16)
        cols = slice(g * gch, (g + 1) * gch)
        for c in range(z.shape[0] // lv):
            rows = slice(c * lv, (c + 1) * lv)
            mixed = _dot(w, vb[rows, cols]) + bst_ref[:lv, g:g + 1]
            g_ref[rows, cols] = (u[rows, cols] * mixed).astype(BF16)


def _sgu(dm, proj, lng, lnb, ws, bst, *, sample):
    if sample:
        b_n, n_grid, nc, lv, br, row0, n_rows = dm.Bs, ROW_TILE // dm.DS, 1, dm.DS, dm.DS, dm.Tp // dm.DS, ROW_TILE
    else:
        br = SGU_CHUNKS_PER_STEP * dm.GC
        b_n, n_grid, nc, lv, row0, n_rows = dm.Bp, dm.Bp, dm.S // br, dm.GC, 0, dm.Tp
    gh = dm.GH
    rb = lambda b, c: row0 + b * nc + c
    in_specs = [
        pl.BlockSpec((br, 2 * gh), lambda b, c: (rb(b, c), dm.off_z // (2 * gh))),
        pl.BlockSpec((1, gh), lambda b, c: (0, 0)),
        pl.BlockSpec((1, gh), lambda b, c: (0, 0)),
        pl.BlockSpec((dm.G, dm.GC, dm.GC), lambda b, c: (0, 0, 0)),
        pl.BlockSpec((dm.GC, LANES), lambda b, c: (0, 0)),
    ]
    out_specs = [pl.BlockSpec((br, gh), lambda b, c: (b * nc + c, 0))]
    out_shape = [jax.ShapeDtypeStruct((n_rows, gh), BF16)]
    if sample:
        out_specs.append(pl.BlockSpec((None, lv, gh), lambda b, c: (jnp.minimum(b, b_n - 1), 0, 0)))
        out_shape.append(jax.ShapeDtypeStruct((b_n, lv, gh), F32))
    return pl.pallas_call(
        functools.partial(_sgu_kernel, dm=dm, lv=lv, n_seq=b_n, emit_v=sample),
        grid=(n_grid, nc),
        in_specs=in_specs,
        out_specs=out_specs,
        out_shape=out_shape,
        compiler_params=_cparams(("arbitrary", "arbitrary"), 24),
        name="sgu_sample" if sample else "sgu_prompt",
    )(proj, lng, lnb, ws, bst)


def _merge_kernel(ap_ref, as_ref, gp_ref, gs_ref, ga_ref, gb_ref, wa_ref, wb_ref, o_ref):
    is_s = pl.program_id(1) == pl.num_programs(1) - 1
    pa = _dot(jnp.where(is_s, as_ref[...], ap_ref[...]), wa_ref[...])
    pb = _dot(jnp.where(is_s, gs_ref[...], gp_ref[...]), wb_ref[...])
    o_ref[...] = (jax.nn.sigmoid(ga_ref[...].astype(F32)) * pa
                  + jax.nn.sigmoid(gb_ref[...].astype(F32)) * pb).astype(o_ref.dtype)


def _merge(dm, a_p, a_s, g_p, g_s, proj, wa, wb):
    tc, d, v, gh = dm.Tc, dm.D, dm.V, dm.GH
    tm = ROW_TILE
    tn = _pick_tile(d, 1024)
    np_tiles = dm.Tp // tm
    prompt = lambda j, i: (jnp.minimum(i, np_tiles - 1), 0)
    const = lambda j, i: (0, 0)
    blocks = 2 * tm * (v + gh) * 2 + 3 * tm * tn * 2 + (v + gh) * tn * 2
    return pl.pallas_call(
        _merge_kernel,
        grid=(d // tn, tc // tm),
        in_specs=[pl.BlockSpec((tm, v), prompt),
                  pl.BlockSpec((tm, v), const),
                  pl.BlockSpec((tm, gh), prompt),
                  pl.BlockSpec((tm, gh), const),
                  pl.BlockSpec((tm, tn), lambda j, i: (i, dm.off_ga // tn + j)),
                  pl.BlockSpec((tm, tn), lambda j, i: (i, dm.off_gb // tn + j)),
                  pl.BlockSpec((v, tn), lambda j, i: (0, j)),
                  pl.BlockSpec((gh, tn), lambda j, i: (0, j))],
        out_specs=pl.BlockSpec((tm, tn), lambda j, i: (i, j)),
        out_shape=jax.ShapeDtypeStruct((tc, d), BF16),
        compiler_params=_cparams(("arbitrary", "arbitrary"), (2 * blocks + 6 * tm * tn * 4) / MIB + 8),
        name="merge",
    )(a_p, a_s, g_p, g_s, proj, proj, wa, wb)


def _route(logits, eg, ne):
    epg = ne // eg
    lane = lax.broadcasted_iota(I32, logits.shape, 1)
    gmask = lane < eg
    gmax = jnp.max(jnp.where(gmask, logits, NEG), axis=1, keepdims=True)
    gexp = jnp.where(gmask, jnp.exp(jnp.minimum(logits - gmax, 0.0)), 0.0)
    pg = gexp / jnp.sum(gexp, axis=1, keepdims=True)
    p_grp = jnp.max(pg, axis=1, keepdims=True)
    g_sel = jnp.min(jnp.where(gmask & (pg == p_grp), lane, LANES), axis=1, keepdims=True)
    lo = eg + g_sel * epg
    emask = (lane >= lo) & (lane < lo + epg)
    emax = jnp.max(jnp.where(emask, logits, NEG), axis=1, keepdims=True)
    eexp = jnp.where(emask, jnp.exp(jnp.minimum(logits - emax, 0.0)), 0.0)
    pe = eexp / jnp.sum(eexp, axis=1, keepdims=True)
    p1 = jnp.max(jnp.where(emask, pe, -1.0), axis=1, keepdims=True)
    i1 = jnp.min(jnp.where(emask & (pe == p1), lane, LANES), axis=1, keepdims=True)
    rest = emask & (lane != i1)
    p2 = jnp.max(jnp.where(rest, pe, -1.0), axis=1, keepdims=True)
    i2 = jnp.min(jnp.where(rest & (pe == p2), lane, LANES), axis=1, keepdims=True)
    psum = p1 + p2
    eid = jnp.where(lane == 0, i1 - eg, jnp.where(lane == 1, i2 - eg, 0))
    gate = jnp.where(lane == 0, p_grp * (p1 / psum), jnp.where(lane == 1, p_grp * (p2 / psum), 0.0))
    return eid, gate


def _outproj_kernel(m_ref, x_ref, w_ref, g_ref, mod_ref, wr_ref, rb_ref,
                    x1_ref, hp_ref, eid_ref, gate_ref, acc_scr, hi_scr, lo_scr, *, dm):
    i = pl.program_id(0)
    t = jnp.maximum(i - 1, 0)
    is_s = t == dm.n_tiles - 1
    bp = jnp.minimum(t // (dm.S // ROW_TILE), dm.Bp - 1)
    cur, prev = i % 2, (i + 1) % 2
    g = g_ref[...]

    @pl.when(i == 0)
    def _no_previous_tile():
        acc_scr[1] = jnp.zeros(acc_scr.shape[1:], F32)

    def chunk(j):
        rows = pl.ds(j * ROW_CHUNK, ROW_CHUNK)
        idx = jnp.where(is_s, dm.Bp + j, bp)
        x1 = x_ref[rows, :] + mod_ref[GT1, pl.ds(idx, 1), :] * acc_scr[prev, rows, :]
        x1_ref[rows, :] = x1
        h = _rms_mod(x1, g, mod_ref[SC2, pl.ds(idx, 1), :], mod_ref[SH2, pl.ds(idx, 1), :])
        hi, lo = _split_bf16(h)
        hi_scr[rows, :] = hi
        lo_scr[rows, :] = lo
        hp_ref[rows, :] = _pack_pair(h)

    n_chunks = ROW_TILE // ROW_CHUNK
    n_slices = min(dm.D // 256, n_chunks)
    for s in range(n_slices):
        cols = slice(s * dm.D // n_slices, (s + 1) * dm.D // n_slices)
        acc_scr[cur, :, cols] = _dot(m_ref[...], w_ref[:, cols])
        for j in range(s * n_chunks // n_slices, (s + 1) * n_chunks // n_slices):
            chunk(j)
    w_hi, w_lo = _split_bf16(wr_ref[...])
    logits = _dot3(hi_scr[...], lo_scr[...], w_hi, w_lo) + rb_ref[...]
    eid, gate = _route(logits, dm.EG, dm.NE)
    eid_ref[...] = eid
    gate_ref[...] = gate


def _outproj(dm, l, merged, x, w_out, g2, mod, w_router, b_router):
    d, tc = dm.D, dm.Tc
    tile = lambda i: (jnp.maximum(i - 1, 0), 0)
    const = lambda i: (0, 0)
    return pl.pallas_call(
        functools.partial(_outproj_kernel, dm=dm),
        grid=(dm.n_tiles + 1,),
        in_specs=[pl.BlockSpec((ROW_TILE, d), lambda i: (jnp.minimum(i, dm.n_tiles - 1), 0)),
                  pl.BlockSpec((ROW_TILE, d), tile),
                  pl.BlockSpec((d, d), const),
                  pl.BlockSpec((None, 1, d), lambda i: (l, 0, 0)),
                  pl.BlockSpec((None, 6, dm.n_mod, d), lambda i: (l, 0, 0, 0)),
                  pl.BlockSpec((d, LANES), const),
                  pl.BlockSpec((1, LANES), const)],
        out_specs=[pl.BlockSpec((ROW_TILE, d), tile),
                   pl.BlockSpec((ROW_TILE, d // 2), tile),
                   pl.BlockSpec((ROW_TILE, LANES), tile),
                   pl.BlockSpec((ROW_TILE, LANES), tile)],
        out_shape=[jax.ShapeDtypeStruct((tc, d), F32),
                   jax.ShapeDtypeStruct((tc, d // 2), U32),
                   jax.ShapeDtypeStruct((tc, LANES), I32),
                   jax.ShapeDtypeStruct((tc, LANES), F32)],
        scratch_shapes=[pltpu.VMEM((2, ROW_TILE, d), F32), pltpu.VMEM((ROW_TILE, d), BF16),
                        pltpu.VMEM((ROW_TILE, d), BF16)],
        compiler_params=_cparams(("arbitrary",), (4 * d * d + ROW_TILE * d * 44) / MIB + 12),
        name="out_proj",
    )(merged, x, w_out, g2, mod, w_router, b_router)


def _rank_kernel(eid_ref, rank_ref, cnt_ref, run_scr):
    i = pl.program_id(0)

    @pl.when(i == 0)
    def _init():
        run_scr[...] = jnp.zeros_like(run_scr)

    eid = eid_ref[...]
    lane = lax.broadcasted_iota(I32, eid.shape, 1)
    e0 = lane == eid[:, 0:1]
    e1 = lane == eid[:, 1:2]
    hot = (e0 | e1).astype(BF16)
    n = eid.shape[0]
    strict = (lax.broadcasted_iota(I32, (n, n), 1) < lax.broadcasted_iota(I32, (n, n), 0)).astype(BF16)
    before = _dot(strict, hot) + run_scr[...]
    r0 = jnp.sum(jnp.where(e0, before, 0.0), axis=1, keepdims=True)
    r1 = jnp.sum(jnp.where(e1, before, 0.0), axis=1, keepdims=True)
    rank_ref[...] = jnp.where(lane == 0, r0, jnp.where(lane == 1, r1, 0.0)).astype(I32)
    run_scr[...] = run_scr[...] + jnp.sum(hot.astype(F32), axis=0, keepdims=True)
    cnt_ref[...] = run_scr[...].astype(I32)


def _rank(dm, eid):
    tile = lambda i: (i, 0)
    return pl.pallas_call(
        _rank_kernel,
        grid=(dm.n_tiles,),
        in_specs=[pl.BlockSpec((ROW_TILE, LANES), tile)],
        out_specs=[pl.BlockSpec((ROW_TILE, LANES), tile), pl.BlockSpec((1, LANES), lambda i: (0, 0))],
        out_shape=[jax.ShapeDtypeStruct((dm.Tc, LANES), I32), jax.ShapeDtypeStruct((1, LANES), I32)],
        scratch_shapes=[pltpu.VMEM((1, LANES), F32)],
        compiler_params=_cparams(("arbitrary",), 16),
        name="moe_rank",
    )(eid)


def _dispatch_kernel(tail_ref, dest_ref, h_ref, xs_ref, zero_scr, sem):
    @pl.when(pl.program_id(0) == 0)
    def _clear_tails():
        zero_scr[...] = jnp.zeros_like(zero_scr)
        n_e = tail_ref.shape[0]

        def tail_copy(e):
            start = pl.multiple_of(tail_ref[e], MOE_BLOCK)
            return pltpu.make_async_copy(zero_scr, xs_ref.at[pl.ds(start, MOE_BLOCK)], sem)

        for e in range(n_e):
            tail_copy(e).start()
        for e in range(n_e):
            tail_copy(e).wait()

    for r in range(ROW_TILE):
        for k in range(TOP_K):
            pltpu.make_async_copy(h_ref.at[pl.ds(r, 1)], xs_ref.at[pl.ds(dest_ref[TOP_K * r + k], 1)],
                                  sem).start(priority=k)
    for k in range(TOP_K):
        pltpu.make_async_copy(h_ref, xs_ref.at[pl.ds(0, ROW_TILE)], sem).wait()


def _dispatch(dm, tails, dest, hp):
    d2 = dm.D // 2
    return pl.pallas_call(
        _dispatch_kernel,
        grid_spec=pltpu.PrefetchScalarGridSpec(
            num_scalar_prefetch=1,
            grid=(dm.n_tiles,),
            in_specs=[pl.BlockSpec((ROW_TILE * TOP_K,), lambda i, t: (i,), memory_space=pltpu.SMEM),
                      pl.BlockSpec((ROW_TILE, d2), lambda i, t: (i, 0))],
            out_specs=pl.BlockSpec(memory_space=pl.ANY),
            scratch_shapes=[pltpu.VMEM((MOE_BLOCK, d2), U32), pltpu.SemaphoreType.DMA]),
        out_shape=jax.ShapeDtypeStruct((dm.cap + dm.NE * MOE_BLOCK, d2), U32),
        compiler_params=_cparams(("arbitrary",), 16),
        name="moe_dispatch",
    )(tails, dest, hp)


def _expert_kernel(be_ref, nu_ref, nx_ref, sl_ref, xs_ref, wg_hbm, wu_hbm, wd_hbm, ys_ref,
                   wg_f, wu_f, wd_f, wg_s, wu_s, wd_s, sem, *, layer):
    j = pl.program_id(0)
    e, slot = be_ref[j], sl_ref[j]
    used = j < nu_ref[0]
    first = used & ((j == 0) | (e != be_ref[jnp.maximum(j - 1, 0)]))

    def weight_copies(expert, s):
        return [pltpu.make_async_copy(src.at[layer, expert], dst.at[s], sem.at[s])
                for src, dst in ((wg_hbm, wg_f), (wu_hbm, wu_f), (wd_hbm, wd_f))]

    @pl.when(j == 0)
    def _first_expert():
        for cp in weight_copies(e, slot):
            cp.start()

    @pl.when(first)
    def _switch_expert():
        for cp in weight_copies(e, slot):
            cp.wait()
        wg_s[...] = wg_f[slot].astype(BF16)
        wu_s[...] = wu_f[slot].astype(BF16)
        wd_s[...] = wd_f[slot].astype(BF16)

        @pl.when(nx_ref[j] >= 0)
        def _prefetch_next():
            for cp in weight_copies(nx_ref[j], 1 - slot):
                cp.start()

    @pl.when(used)
    def _compute():
        d2 = xs_ref.shape[1]
        lo, hi = _unpack_pair(xs_ref[...])
        lo, hi = lo.astype(BF16), hi.astype(BF16)
        hg = _dot(lo, wg_s[:d2, :]) + _dot(hi, wg_s[d2:, :])
        hu = _dot(lo, wu_s[:d2, :]) + _dot(hi, wu_s[d2:, :])
        act = (hg * jax.nn.sigmoid(hg) * hu).astype(BF16)
        ys_ref[...] = _pack_pair(_dot(act, wd_s[...]))


def _experts(dm, l, blk_e, n_used, blk_next, blk_slot, xs, w_gate, w_up, w_down):
    d, de, d2 = dm.D, dm.DE, dm.D // 2
    nb = dm.cap // MOE_BLOCK
    rows = lambda j, be, nu, nx, sl: (jnp.minimum(j, nu[0] - 1), 0)
    hbm = pl.BlockSpec(memory_space=pl.ANY)
    return pl.pallas_call(
        functools.partial(_expert_kernel, layer=l),
        grid_spec=pltpu.PrefetchScalarGridSpec(
            num_scalar_prefetch=4,
            grid=(nb,),
            in_specs=[pl.BlockSpec((MOE_BLOCK, d2), rows), hbm, hbm, hbm],
            out_specs=pl.BlockSpec((MOE_BLOCK, d2), rows),
            scratch_shapes=[pltpu.VMEM((2, d, de), F32), pltpu.VMEM((2, d, de), F32), pltpu.VMEM((2, de, d), F32),
                            pltpu.VMEM((d, de), BF16), pltpu.VMEM((d, de), BF16), pltpu.VMEM((de, d), BF16),
                            pltpu.SemaphoreType.DMA((2,))]),
        out_shape=jax.ShapeDtypeStruct((dm.cap, d2), U32),
        compiler_params=_cparams(("arbitrary",), (3 * d * de * (2 * 4 + 2) + MOE_BLOCK * d * 28) / MIB + 8),
        name="moe_experts",
    )(blk_e, n_used, blk_next, blk_slot, xs, w_gate, w_up, w_down)


def _combine_kernel(*refs, dm, last):
    if last:
        dest_ref, destn_ref, ys_ref, x_ref, gate_ref, mod_ref, g_ref, yp_ref, ysm_ref, ybuf, y_scr, sem = refs
    else:
        (dest_ref, destn_ref, ys_ref, x_ref, gate_ref, mod_ref, g_ref, modn_ref, w_ref,
         o_ref, h_ref, p_ref, ybuf, lo_scr, sem) = refs
    i = pl.program_id(0)
    is_s = i == dm.n_tiles - 1
    bp = jnp.minimum(i // (dm.S // ROW_TILE), dm.Bp - 1)
    d2 = dm.D // 2
    g = g_ref[...]
    slot = i % 2

    def issue(idx_ref, to_slot, r0, n):
        for r in range(n):
            for k in range(TOP_K):
                pltpu.make_async_copy(ys_ref.at[pl.ds(idx_ref[TOP_K * (r0 + r) + k], 1)],
                                      ybuf.at[to_slot, k, pl.ds(r0 + r, 1)], sem.at[to_slot]).start(priority=k)

    def wait_slot(s):
        for k in range(TOP_K):
            pltpu.make_async_copy(ys_ref.at[pl.ds(0, ROW_TILE)], ybuf.at[s, k], sem.at[s]).wait()

    @pl.when(i == 0)
    def _first_tile():
        issue(dest_ref, 0, 0, ROW_TILE)

    wait_slot(slot)

    def chunk(j):
        rows = pl.ds(j * ROW_CHUNK, ROW_CHUNK)
        issue(destn_ref, 1 - slot, j * ROW_CHUNK, ROW_CHUNK)
        idx = jnp.where(is_s, dm.Bp + j, bp)
        gt = mod_ref[GT2, pl.ds(idx, 1), :]
        gates = gate_ref[rows, :]
        g0, g1 = gates[:, 0:1], gates[:, 1:2]
        lo0, hi0 = _unpack_pair(ybuf[slot, 0, rows, :])
        lo1, hi1 = _unpack_pair(ybuf[slot, 1, rows, :])
        x_lo = x_ref[rows, :d2] + gt[:, :d2] * (g0 * lo0 + g1 * lo1)
        x_hi = x_ref[rows, d2:] + gt[:, d2:] * (g0 * hi0 + g1 * hi1)
        ms = (jnp.sum(x_lo * x_lo, axis=-1, keepdims=True)
              + jnp.sum(x_hi * x_hi, axis=-1, keepdims=True)) * (1.0 / dm.D)
        rs = lax.rsqrt(ms + EPS)
        if last:
            y_scr[rows, :d2] = x_lo * rs * g[:, :d2]
            y_scr[rows, d2:] = x_hi * rs * g[:, d2:]
        else:
            o_ref[rows, :d2] = x_lo
            o_ref[rows, d2:] = x_hi
            sc = modn_ref[SC1, pl.ds(idx, 1), :]
            sh = modn_ref[SH1, pl.ds(idx, 1), :]
            for half, xh in ((slice(0, d2), x_lo), (slice(d2, dm.D), x_hi)):
                h = xh * rs * g[:, half] * (1.0 + sc[:, half]) + sh[:, half]
                hi, lo = _split_bf16(h)
                h_ref[rows, half] = hi
                lo_scr[rows, half] = lo

    for j in range(ROW_TILE // ROW_CHUNK):
        chunk(j)

    @pl.when(is_s)
    def _drain():
        wait_slot(1 - slot)

    if last:
        @pl.when(is_s)
        def _sample_rows():
            ysm_ref[...] = y_scr[...]

        @pl.when(jnp.logical_not(is_s))
        def _prompt_rows():
            yp_ref[...] = y_scr[...]
    else:
        w_hi, w_lo = _gate_weight(w_ref)
        p_ref[...] = _dot3(h_ref[...], lo_scr[...], w_hi, w_lo)


def _combine(dm, l, dest, ys, x1, gate, mod, g_next, w_in=None):
    d = dm.D
    last = w_in is None
    tile = lambda i: (i, 0)
    np_tiles = dm.Tp // ROW_TILE
    in_specs = [pl.BlockSpec((ROW_TILE * TOP_K,), lambda i: (i,), memory_space=pltpu.SMEM),
                pl.BlockSpec((ROW_TILE * TOP_K,), lambda i: (jnp.minimum(i + 1, dm.n_tiles - 1),),
                             memory_space=pltpu.SMEM),
                pl.BlockSpec(memory_space=pl.ANY),
                pl.BlockSpec((ROW_TILE, d), tile),
                pl.BlockSpec((ROW_TILE, LANES), tile),
                pl.BlockSpec((None, 6, dm.n_mod, d), lambda i: (l, 0, 0, 0))]
    scratch = [pltpu.VMEM((2, TOP_K, ROW_TILE, d // 2), U32)]
    if last:
        args = (dest, dest, ys, x1, gate, mod, g_next)
        in_specs.append(pl.BlockSpec((1, d), lambda i: (0, 0)))
        out_specs = [pl.BlockSpec((ROW_TILE, d), lambda i: (jnp.minimum(i, np_tiles - 1), 0)),
                     pl.BlockSpec((ROW_TILE, d), lambda i: (0, 0))]
        out_shape = [jax.ShapeDtypeStruct((dm.Tp, d), F32), jax.ShapeDtypeStruct((ROW_TILE, d), F32)]
        scratch.append(pltpu.VMEM((ROW_TILE, d), F32))
    else:
        args = (dest, dest, ys, x1, gate, mod, g_next, mod, w_in)
        in_specs += [pl.BlockSpec((None, 1, d), lambda i: (l + 1, 0, 0)),
                     pl.BlockSpec((None, 6, dm.n_mod, d), lambda i: (l + 1, 0, 0, 0)),
                     pl.BlockSpec((None, 2 * dm.H, d), lambda i: (l + 1, dm.off_z // (2 * dm.H), 0))]
        out_specs = [pl.BlockSpec((ROW_TILE, d), tile), pl.BlockSpec((ROW_TILE, d), tile),
                     pl.BlockSpec((ROW_TILE, LANES), tile)]
        out_shape = [jax.ShapeDtypeStruct((dm.Tc, d), F32), jax.ShapeDtypeStruct((dm.Tc, d), BF16),
                     jax.ShapeDtypeStruct((dm.Tc, LANES), F32)]
        scratch.append(pltpu.VMEM((ROW_TILE, d), BF16))
    scratch.append(pltpu.SemaphoreType.DMA((2,)))
    return pl.pallas_call(
        functools.partial(_combine_kernel, dm=dm, last=last),
        grid=(dm.n_tiles,),
        in_specs=in_specs,
        out_specs=out_specs,
        out_shape=out_shape,
        scratch_shapes=scratch,
        compiler_params=_cparams(("arbitrary",), ROW_TILE * d * 40 / MIB + 16),
        name="moe_combine_last" if last else "moe_combine",
    )(*args)


def _pad_lanes(x, n=LANES):
    return jnp.pad(x, [(0, 0)] * (x.ndim - 1) + [(0, n - x.shape[-1])])


def _moe_plan(dm, eid, rank, counts):
    counts = counts[0, :dm.NE]
    padded = (counts + MOE_BLOCK - 1) // MOE_BLOCK * MOE_BLOCK
    pad_end = jnp.cumsum(padded)
    pad_start = pad_end - padded
    dest = (pad_start[eid[:, :TOP_K]] + rank[:, :TOP_K]).reshape(-1).astype(I32)
    nb = dm.cap // MOE_BLOCK
    n_used = (pad_end[-1] // MOE_BLOCK).astype(I32)
    starts = jnp.minimum(jnp.arange(nb, dtype=I32), n_used - 1) * MOE_BLOCK
    blk_e = jnp.minimum(jnp.sum(pad_end[None, :] <= starts[:, None], axis=1), dm.NE - 1).astype(I32)
    ids = jnp.arange(dm.NE, dtype=I32)
    spare = dm.cap + ids * MOE_BLOCK
    tails = jnp.where(padded > 0, pad_end - MOE_BLOCK, spare).astype(I32)
    live = jnp.where(counts > 0, ids, dm.NE)
    nxt = lax.cummin(jnp.concatenate([live[1:], jnp.full((1,), dm.NE, I32)]), reverse=True)
    blk_next = jnp.where(nxt[blk_e] < dm.NE, nxt[blk_e], -1).astype(I32)
    blk_slot = ((jnp.cumsum(counts > 0) - 1)[blk_e] % 2).astype(I32)
    return dest, blk_e, n_used.reshape(1), tails, blk_next, blk_slot


def kernel(x_prompt, x_sample, c_prompt, c_sample, state_mlstm_C, state_mlstm_n, state_mlstm_m, w_ada, b_ada, norm1_g, norm2_g, w_in, b_igate, b_fgate, mh_norm_g, sgu_ln_g, sgu_ln_b, w_spatial, b_spatial, w_branch_a, w_branch_b, w_out, w_router_group, b_router_group, w_router_expert, b_router_expert, w_expert_gate, w_expert_up, w_expert_down, final_norm_g):
    bp, s, d = x_prompt.shape
    bs, ds_, _ = x_sample.shape
    depth = w_in.shape[0]
    h_n = b_igate.shape[1]
    dv, dk = state_mlstm_C.shape[-2:]
    qk, v = h_n * dk, h_n * dv
    gh = (w_in.shape[2] - (2 * qk + 2 * v + 2 * h_n + 2 * d)) // 2
    dm = Dims(Bp=bp, S=s, Bs=bs, DS=ds_, D=d, DEPTH=depth, H=h_n, DK=dk, DV=dv, GH=gh,
              G=w_spatial.shape[1], GC=w_spatial.shape[2], EG=w_router_group.shape[-1],
              NE=w_router_expert.shape[-1], DE=w_expert_gate.shape[-1])
    assert dm.DS == ROW_CHUNK and dm.Ts <= ROW_TILE and s % ROW_TILE == 0 and s % MLSTM_CHUNK == 0
    assert dm.EG + dm.NE <= LANES and 2 * h_n <= LANES and (2 * h_n) % 8 == 0 and dm.Bp + dm.Bs <= 32
    w_in = jnp.swapaxes(w_in, 1, 2)
    assert dm.off_v % v == 0 and dm.off_o % v == 0 and dm.off_z % (2 * gh) == 0 and d % 256 == 0

    c_all = jnp.concatenate([c_prompt, c_sample, jnp.zeros((32 - bp - bs, d), F32)], axis=0)
    mod = _adaln(c_all, w_ada, b_ada)[:, :bp + bs].reshape(depth, bp + bs, 6, d)
    mod = jnp.pad(mod, ((0, 0), (0, dm.n_mod - bp - bs), (0, 0), (0, 0))).transpose(0, 2, 1, 3)

    xs_rows = jnp.pad(x_sample.reshape(dm.Ts, d), ((0, ROW_TILE - dm.Ts), (0, 0)))
    x = (x_prompt.reshape(dm.Tp, d), xs_rows)
    zeros_c = jnp.zeros((1, bp, h_n, dv, dk), F32)
    zeros_n = jnp.zeros((1, bp, h_n, dk), F32)
    zeros_m = jnp.zeros((1, bp, 1, LANES), F32)
    m_in = _pad_lanes(state_mlstm_m)[:, :, None, :]
    g1 = norm1_g.reshape(depth, 1, d)
    out_c_p, out_n_p, out_m_p, out_c_s, out_n_s, out_m_s, out_v = [], [], [], [], [], [], []

    for l in range(depth):
        if l == 0:
            hb, gates, x = _norm_proj(dm, l, x, g1, mod, w_in, SC1, SH1)
        proj = _in_proj(dm, l, hb, w_in)

        bias = _pad_lanes(jnp.concatenate([b_igate[l], b_fgate[l]])[None, :])
        mhg = mh_norm_g[l][None, :]
        a_p, c_p, n_p, m_p = _mlstm(dm, proj, gates, bias, mhg, zeros_c, zeros_n, zeros_m, 0, sample=False)
        a_s, c_s, n_s, m_s = _mlstm(dm, proj, gates, bias, mhg, state_mlstm_C, state_mlstm_n, m_in, l,
                                    sample=True)
        lng, lnb = sgu_ln_g[l][None, :], sgu_ln_b[l][None, :]
        bst = _pad_lanes(b_spatial[l].T)
        (g_p,) = _sgu(dm, proj, lng, lnb, w_spatial[l], bst, sample=False)
        g_s, v_rows = _sgu(dm, proj, lng, lnb, w_spatial[l], bst, sample=True)

        merged = _merge(dm, a_p, a_s, g_p, g_s, proj, w_branch_a[l].astype(BF16), w_branch_b[l].astype(BF16))
        w_router = _pad_lanes(jnp.concatenate([w_router_group[l], w_router_expert[l]], axis=1))
        b_router = _pad_lanes(jnp.concatenate([b_router_group[l], b_router_expert[l]])[None, :])
        x1, hp, eid, gate = _outproj(dm, l, merged, x, w_out[l].astype(BF16), norm2_g.reshape(depth, 1, d),
                                     mod, w_router, b_router)

        rank, counts = _rank(dm, eid)
        dest, blk_e, n_used, tails, blk_next, blk_slot = _moe_plan(dm, eid, rank, counts)
        xs = _dispatch(dm, tails, dest, hp)
        ys = _experts(dm, l, blk_e, n_used, blk_next, blk_slot, xs,
                      w_expert_gate, w_expert_up, w_expert_down)
        if l + 1 < depth:
            x, hb, gates = _combine(dm, l, dest, ys, x1, gate, mod, g1, w_in)
        else:
            y_p, y_s = _combine(dm, l, dest, ys, x1, gate, mod, final_norm_g[None, :])

        out_c_p.append(c_p)
        out_n_p.append(n_p)
        out_m_p.append(m_p[:, 0, :h_n])
        out_c_s.append(c_s)
        out_n_s.append(n_s)
        out_m_s.append(m_s[:, 0, :h_n])
        out_v.append(v_rows)

    return (y_p.reshape(bp, s, d), y_s[:dm.Ts].reshape(bs, ds_, d),
            jnp.stack(out_c_p), jnp.stack(out_n_p), jnp.stack(out_m_p),
            jnp.stack(out_c_s), jnp.stack(out_n_s), jnp.stack(out_m_s), jnp.stack(out_v))
```

```python
import functools
from typing import NamedTuple

import jax
import jax.numpy as jnp
from jax import lax
from jax.experimental import pallas as pl
from jax.experimental.pallas import tpu as pltpu

F32, BF16, U32, I32 = jnp.float32, jnp.bfloat16, jnp.uint32, jnp.int32
EPS = 1e-6
NEG = -1e30
LANES = 128
ROW_TILE = 512
ROW_CHUNK = 16
MLSTM_CHUNK = 256
MOE_BLOCK = 512
TOP_K = 2
CHUNK_UNROLL = 2
SGU_CHUNKS_PER_STEP = 4
MIB = 1024 * 1024
SH1, SC1, GT1, SH2, SC2, GT2 = range(6)


class Dims(NamedTuple):
    Bp: int
    S: int
    Bs: int
    DS: int
    D: int
    DEPTH: int
    H: int
    DK: int
    DV: int
    GH: int
    G: int
    GC: int
    EG: int
    NE: int
    DE: int

    @property
    def QK(self):
        return self.H * self.DK

    @property
    def V(self):
        return self.H * self.DV

    @property
    def Tp(self):
        return self.Bp * self.S

    @property
    def Ts(self):
        return self.Bs * self.DS

    @property
    def Tc(self):
        return self.Tp + ROW_TILE

    @property
    def n_tiles(self):
        return self.Tc // ROW_TILE

    @property
    def n_mod(self):
        return self.Bp + ROW_TILE // ROW_CHUNK

    @property
    def off_v(self):
        return 2 * self.QK

    @property
    def off_o(self):
        return 2 * self.QK + self.V

    @property
    def off_z(self):
        return 2 * self.QK + 2 * self.V

    @property
    def off_ga(self):
        return self.off_z + 2 * self.GH

    @property
    def off_gb(self):
        return self.off_ga + self.D

    @property
    def n_main(self):
        return self.off_gb + self.D

    @property
    def cap(self):
        tk = self.Tc * TOP_K
        return -(-(tk + self.NE * (MOE_BLOCK - 1)) // MOE_BLOCK) * MOE_BLOCK


def _cparams(semantics, vmem_mib):
    return pltpu.CompilerParams(dimension_semantics=semantics, vmem_limit_bytes=int(vmem_mib * MIB))


def _pick_tile(n, cap, unit=256):
    best = None
    for t in range(unit, min(n, cap) + 1, unit):
        if n % t == 0:
            best = t
    assert best is not None, (n, cap)
    return best


def _dot(a, b):
    return jnp.dot(a, b, preferred_element_type=F32)


def _split_bf16(x):
    hi = x.astype(BF16)
    lo = (x - hi.astype(F32)).astype(BF16)
    return hi, lo


def _dot3(a_hi, a_lo, w_hi, w_lo):
    return _dot(a_hi, w_hi) + _dot(a_hi, w_lo) + _dot(a_lo, w_hi)


def _gate_weight(w_ref):
    rows, d = w_ref.shape
    w = jnp.concatenate([w_ref[...], jnp.zeros((LANES - rows, d), F32)], axis=0).T
    return _split_bf16(w)


def _rms_mod(x, g, sc, sh):
    y = x * lax.rsqrt(jnp.mean(x * x, axis=-1, keepdims=True) + EPS) * g
    return y * (1.0 + sc) + sh


def _pack_pair(x):
    d2 = x.shape[-1] // 2
    bits = lax.bitcast_convert_type(x.astype(BF16).astype(F32), U32)
    return (bits[:, :d2] >> 16) | (bits[:, d2:] & jnp.uint32(0xFFFF0000))


def _unpack_pair(p):
    lo = lax.bitcast_convert_type(p << 16, F32)
    hi = lax.bitcast_convert_type(p & jnp.uint32(0xFFFF0000), F32)
    return lo, hi


def _adaln_kernel(c_ref, w_ref, b_ref, o_ref):
    c = c_ref[...]
    s = (c * jax.nn.sigmoid(c)).astype(BF16)
    o_ref[...] = _dot(s, w_ref[...].astype(BF16)) + b_ref[...]


def _adaln(c_all, w_ada, b_ada):
    depth, d, n6 = w_ada.shape
    r = c_all.shape[0]
    tn = _pick_tile(n6, 1024)
    return pl.pallas_call(
        _adaln_kernel,
        grid=(depth, n6 // tn),
        in_specs=[pl.BlockSpec((r, d), lambda l, j: (0, 0)),
                  pl.BlockSpec((None, d, tn), lambda l, j: (l, 0, j)),
                  pl.BlockSpec((None, 1, tn), lambda l, j: (l, 0, j))],
        out_specs=pl.BlockSpec((None, r, tn), lambda l, j: (l, 0, j)),
        out_shape=jax.ShapeDtypeStruct((depth, r, n6), F32),
        compiler_params=_cparams(("arbitrary", "arbitrary"), 2 * d * tn * 4 / MIB + 3 * d * tn * 2 / MIB + 8),
        name="adaln",
    )(c_all, w_ada, b_ada.reshape(depth, 1, n6))


def _norm_proj_kernel(*refs, first, dm, k_sc, k_sh):
    if first:
        xp_ref, xs_ref, g_ref, mod_ref, w_ref, h_ref, p_ref, xc_ref, lo_scr = refs
    else:
        x_ref, g_ref, mod_ref, w_ref, h_ref, p_ref, lo_scr = refs
    i = pl.program_id(0)
    is_s = i == dm.n_tiles - 1
    bp = jnp.minimum(i // (dm.S // ROW_TILE), dm.Bp - 1)
    g = g_ref[...]

    def chunk(j, carry):
        rows = pl.ds(pl.multiple_of(j * ROW_CHUNK, ROW_CHUNK), ROW_CHUNK)
        if first:
            x = jnp.where(is_s, xs_ref[rows, :], xp_ref[rows, :])
            xc_ref[rows, :] = x
        else:
            x = x_ref[rows, :]
        idx = jnp.where(is_s, dm.Bp + j, bp)
        h = _rms_mod(x, g, mod_ref[k_sc, pl.ds(idx, 1), :], mod_ref[k_sh, pl.ds(idx, 1), :])
        hi, lo = _split_bf16(h)
        h_ref[rows, :] = hi
        lo_scr[rows, :] = lo
        return carry

    lax.fori_loop(0, ROW_TILE // ROW_CHUNK, chunk, 0, unroll=CHUNK_UNROLL)
    w_hi, w_lo = _gate_weight(w_ref)
    p_ref[...] = _dot3(h_ref[...], lo_scr[...], w_hi, w_lo)


def _norm_proj(dm, l, x_in, g, mod, w_in, k_sc, k_sh):
    d, tc, nt = dm.D, dm.Tc, dm.n_tiles
    first = isinstance(x_in, tuple)
    np_tiles = dm.Tp // ROW_TILE
    tile = lambda i: (i, 0)
    if first:
        x_args = list(x_in)
        x_specs = [pl.BlockSpec((ROW_TILE, d), lambda i: (jnp.minimum(i, np_tiles - 1), 0)),
                   pl.BlockSpec((ROW_TILE, d), lambda i: (0, 0))]
    else:
        x_args = [x_in]
        x_specs = [pl.BlockSpec((ROW_TILE, d), tile)]
    out_shape = [jax.ShapeDtypeStruct((tc, d), BF16), jax.ShapeDtypeStruct((tc, LANES), F32)]
    out_specs = [pl.BlockSpec((ROW_TILE, d), tile), pl.BlockSpec((ROW_TILE, LANES), tile)]
    if first:
        out_shape.append(jax.ShapeDtypeStruct((tc, d), F32))
        out_specs.append(pl.BlockSpec((ROW_TILE, d), tile))
    return pl.pallas_call(
        functools.partial(_norm_proj_kernel, first=first, dm=dm, k_sc=k_sc, k_sh=k_sh),
        grid=(nt,),
        in_specs=x_specs + [pl.BlockSpec((None, 1, d), lambda i: (l, 0, 0)),
                            pl.BlockSpec((None, 6, dm.n_mod, d), lambda i: (l, 0, 0, 0)),
                            pl.BlockSpec((None, 2 * dm.H, d), lambda i: (l, dm.off_z // (2 * dm.H), 0))],
        out_specs=out_specs,
        out_shape=out_shape,
        scratch_shapes=[pltpu.VMEM((ROW_TILE, d), BF16)],
        compiler_params=_cparams(("arbitrary",), ROW_TILE * d * 40 / MIB + 16),
        name="norm_proj",
    )(*x_args, g, mod, w_in)


def _in_proj_kernel(a_ref, w_ref, wn_ref, o_ref, w_scr, *, n_aligned, shift):
    j, i = pl.program_id(0), pl.program_id(1)

    groups = w_ref.shape[0] // LANES

    @pl.when((i == 0) & (j < n_aligned))
    def _cast():
        for g in range(groups):
            w_scr[:, g * LANES:(g + 1) * LANES] = w_ref[g * LANES:(g + 1) * LANES, :].T.astype(BF16)

    @pl.when((i == 0) & (j >= n_aligned))
    def _cast_shifted():
        for g in range(groups):
            if g + 1 < groups:
                rows = w_ref[g * LANES + shift:(g + 1) * LANES + shift, :]
            else:
                rows = jnp.concatenate([w_ref[g * LANES + shift:, :], wn_ref[...]], axis=0)
            w_scr[:, g * LANES:(g + 1) * LANES] = rows.T.astype(BF16)

    o_ref[...] = _dot(a_ref[...], w_scr[...]).astype(o_ref.dtype)


def _in_proj(dm, l, a, w_in_t):
    m, k = a.shape
    off_if, n = dm.off_z, dm.n_main
    shift = 2 * dm.H
    tn = max(t for t in range(LANES, 1024 + 1, LANES) if off_if % t == 0 and n % t == 0)
    tm = _pick_tile(m, 1536)
    vmem = (2 * tm * k * 2 + 2 * k * (tn + shift) * 4 + k * tn * 2 + 2 * tm * tn * 2 + 2 * tm * tn * 4) / MIB + 8
    return pl.pallas_call(
        functools.partial(_in_proj_kernel, n_aligned=off_if // tn, shift=shift),
        grid=(n // tn, m // tm),
        in_specs=[pl.BlockSpec((tm, k), lambda j, i: (i, 0)),
                  pl.BlockSpec((None, tn, k), lambda j, i: (l, j, 0)),
                  pl.BlockSpec((None, shift, k), lambda j, i: (l, (j + 1) * (tn // shift), 0))],
        out_specs=pl.BlockSpec((tm, tn), lambda j, i: (i, j)),
        out_shape=jax.ShapeDtypeStruct((m, n), BF16),
        scratch_shapes=[pltpu.VMEM((k, tn), BF16)],
        compiler_params=_cparams(("arbitrary", "arbitrary"), vmem),
        name="in_proj",
    )(a, w_in_t, w_in_t)


def _pad_rows(x, rows):
    if x.shape[0] == rows:
        return x
    return jnp.concatenate([x, jnp.zeros((rows - x.shape[0], x.shape[1]), x.dtype)], axis=0)


def _mlstm_kernel(*refs, dm, lv, n_seq):
    a_ref = refs[10]
    seq_ok = pl.program_id(0) < n_seq

    @pl.when(seq_ok)
    def _run():
        _mlstm_step(*refs, dm=dm, lv=lv)

    @pl.when(jnp.logical_not(seq_ok))
    def _fill():
        a_ref[...] = jnp.zeros_like(a_ref)


def _mlstm_step(q_ref, k_ref, v_ref, o_ref, gt_ref, bias_ref, mhg_ref, c0_ref, n0_ref, m0_ref,
                a_ref, cout_ref, nout_ref, mout_ref, st_scr, m_scr, *, dm, lv):
    h_n, dk, dv = dm.H, dm.DK, dm.DV
    L = MLSTM_CHUNK
    c = pl.program_id(1)

    scale = dk ** -0.5

    @pl.when(c == 0)
    def _init():
        for h in range(h_n):
            st_scr[h, :, :dv] = c0_ref[h].T * (1.0 / scale)
            st_scr[h, :, dv:] = jnp.broadcast_to(n0_ref[h:h + 1, :] * (1.0 / scale), (LANES, dk)).T
        m_scr[...] = m0_ref[...]

    row = lax.broadcasted_iota(I32, (L, L), 0)
    col = lax.broadcasted_iota(I32, (L, L), 1)
    causal = col <= row
    lane = lax.broadcasted_iota(I32, (L, LANES), 1)
    rowl = lax.broadcasted_iota(I32, (L, LANES), 0)

    xg = _pad_rows(gt_ref[...], L) + bias_ref[...]
    f_log = jnp.minimum(xg, 0.0) - jnp.log1p(jnp.exp(-jnp.abs(xg)))
    gl = jnp.where(lane < h_n, xg, f_log)
    gl = jnp.where(rowl < lv, gl, 0.0)
    tri = causal.astype(BF16)
    g_hi = gl.astype(BF16)
    r1 = gl - g_hi.astype(F32)
    g_mid = r1.astype(BF16)
    g_lo = (r1 - g_mid.astype(F32)).astype(BF16)
    cum = _dot(tri, g_hi) + _dot(tri, g_mid) + _dot(tri, g_lo)
    b_all = pltpu.roll(cum, LANES - h_n, axis=1)
    a_all = jnp.where(rowl < lv, gl - b_all, NEG)

    kt_all = _pad_rows(k_ref[...], L).T
    q_all = _pad_rows(q_ref[...], L)
    v_all = _pad_rows(v_ref[...], L)
    a_rows = a_all.T
    m_old = m_scr[...]
    m_new = m_old
    lane1 = lax.broadcasted_iota(I32, (1, LANES), 1)
    ones_blk = jnp.ones((L, LANES), BF16)
    v_blocks = dv // LANES

    for h in range(h_n):
        a_row = a_rows[h:h + 1, :]
        a_rep = jnp.broadcast_to(a_all[:, h:h + 1], (L, LANES))
        b_rep = jnp.broadcast_to(b_all[:, h:h + 1], (L, LANES))
        m_prev = m_old[:, h:h + 1]
        big_m = jnp.maximum(jnp.max(jnp.where(causal, a_row, NEG), axis=1, keepdims=True), m_prev)
        d_w = jnp.where(causal, jnp.exp(jnp.minimum(a_row - big_m, 0.0)), 0.0)
        inter = jnp.exp(m_prev - big_m)
        qh = q_all[:, h * dk:(h + 1) * dk]
        v_parts = [v_all[:, h * dv + j * LANES:h * dv + (j + 1) * LANES] for j in range(v_blocks)]
        kt = kt_all[h * dk:(h + 1) * dk, :]
        sd = (_dot(qh, kt) * d_w).astype(BF16)
        st = st_scr[h]
        q_int = (inter * qh.astype(F32)).astype(BF16)
        x = _dot(jnp.concatenate([sd, q_int], axis=1),
                 jnp.concatenate([jnp.concatenate(v_parts + [ones_blk], axis=1), st.astype(BF16)], axis=0))
        den = x[:, dv:]
        inv = 1.0 / jnp.maximum(jnp.abs(den), jnp.exp(-(b_rep + big_m)) * (1.0 / scale))
        sq = jnp.sum(x[:, :dv] * x[:, :dv], axis=1, keepdims=True) * (1.0 / dv)
        sc = (inv * lax.rsqrt(inv * inv * sq + EPS))[:lv]
        for j in range(v_blocks):
            cols = slice(h * dv + j * LANES, h * dv + (j + 1) * LANES)
            gate_o = 0.5 * jnp.tanh(0.5 * o_ref[:, cols].astype(F32)) + 0.5
            a_ref[:, cols] = (x[:lv, j * LANES:(j + 1) * LANES] * sc * (mhg_ref[:, cols] * gate_o)).astype(BF16)
        m_last = big_m[L - 1:L, :]
        decay = jnp.exp(m_prev - m_last)
        w_rep = jnp.exp(a_rep - m_last)
        vw = jnp.concatenate([(p.astype(F32) * w_rep).astype(BF16) for p in v_parts] + [w_rep.astype(BF16)], axis=1)
        st_scr[h] = decay * st + _dot(kt, vw)
        m_new = jnp.where(lane1 == h, b_all[L - 1:L, h:h + 1] + m_last, m_new)
    m_scr[...] = m_new

    @pl.when(c == pl.num_programs(1) - 1)
    def _fin():
        for h in range(h_n):
            cout_ref[h] = st_scr[h, :, :dv].T * scale
            nout_ref[h:h + 1, :] = st_scr[h, :, dv:].T[0:1, :] * scale
        mout_ref[...] = m_scr[...]


def _mlstm(dm, proj, gates, bias, mhg, c0, n0, m0, ls, *, sample):
    if sample:
        b_n, n_grid, nc, lv, row0, n_rows = dm.Bs, ROW_TILE // dm.DS, 1, dm.DS, dm.Tp // dm.DS, ROW_TILE
    else:
        b_n, n_grid, nc, lv, row0, n_rows = dm.Bp, dm.Bp, dm.S // MLSTM_CHUNK, MLSTM_CHUNK, 0, dm.Tp
    h_n, dk, dv, qk, v = dm.H, dm.DK, dm.DV, dm.QK, dm.V
    rb = lambda b, c: row0 + b * nc + c
    sq = lambda b: jnp.minimum(b, b_n - 1)
    in_specs = [
        pl.BlockSpec((lv, qk), lambda b, c: (rb(b, c), 0)),
        pl.BlockSpec((lv, qk), lambda b, c: (rb(b, c), 1)),
        pl.BlockSpec((lv, v), lambda b, c: (rb(b, c), dm.off_v // v)),
        pl.BlockSpec((lv, v), lambda b, c: (rb(b, c), dm.off_o // v)),
        pl.BlockSpec((lv, LANES), lambda b, c: (rb(b, c), 0)),
        pl.BlockSpec((1, LANES), lambda b, c: (0, 0)),
        pl.BlockSpec((1, v), lambda b, c: (0, 0)),
        pl.BlockSpec((None, None, h_n, dv, dk), lambda b, c: (ls, sq(b), 0, 0, 0)),
        pl.BlockSpec((None, None, h_n, dk), lambda b, c: (ls, sq(b), 0, 0)),
        pl.BlockSpec((None, None, 1, LANES), lambda b, c: (ls, sq(b), 0, 0)),
    ]
    return pl.pallas_call(
        functools.partial(_mlstm_kernel, dm=dm, lv=lv, n_seq=b_n),
        grid=(n_grid, nc),
        in_specs=in_specs,
        out_specs=[pl.BlockSpec((lv, v), lambda b, c: (b * nc + c, 0)),
                   pl.BlockSpec((None, h_n, dv, dk), lambda b, c: (sq(b), 0, 0, 0)),
                   pl.BlockSpec((None, h_n, dk), lambda b, c: (sq(b), 0, 0)),
                   pl.BlockSpec((None, 1, LANES), lambda b, c: (sq(b), 0, 0))],
        out_shape=[jax.ShapeDtypeStruct((n_rows, v), BF16),
                   jax.ShapeDtypeStruct((b_n, h_n, dv, dk), F32),
                   jax.ShapeDtypeStruct((b_n, h_n, dk), F32),
                   jax.ShapeDtypeStruct((b_n, 1, LANES), F32)],
        scratch_shapes=[pltpu.VMEM((h_n, dk, dv + LANES), F32), pltpu.VMEM((1, LANES), F32)],
        compiler_params=_cparams(("arbitrary", "arbitrary"), 40),
        name="mlstm_sample" if sample else "mlstm_prompt",
    )(proj, proj, proj, proj, gates, bias, mhg, c0, n0, m0)


def _sgu_kernel(*refs, dm, lv, n_seq, emit_v):
    g_ref = refs[5]
    seq_ok = pl.program_id(0) < n_seq

    @pl.when(seq_ok)
    def _run():
        _sgu_step(*refs, dm=dm, lv=lv, emit_v=emit_v)

    @pl.when(jnp.logical_not(seq_ok))
    def _fill():
        g_ref[...] = jnp.zeros_like(g_ref)


def _sgu_step(*refs, dm, lv, emit_v):
    z_ref, lng_ref, lnb_ref, ws_ref, bst_ref, g_ref = refs[:6]
    gh, gch = dm.GH, dm.GH // dm.G
    z = jax.nn.gelu(z_ref[...].astype(F32))
    u, v = z[:, :gh], z[:, gh:]
    xc = v - jnp.mean(v, axis=-1, keepdims=True)
    vn = xc * lax.rsqrt(jnp.mean(xc * xc, axis=-1, keepdims=True) + EPS) * lng_ref[...] + lnb_ref[...]
    if emit_v:
        refs[6][...] = vn
    row = lax.broadcasted_iota(I32, (lv, lv), 0)
    col = lax.broadcasted_iota(I32, (lv, lv), 1)
    vb = vn.astype(BF16)
    for g in range(dm.G):
        w = jnp.where(col <= row, ws_ref[g, :lv, :lv], 0.0).astype(BF16)
        cols = slice(g * gch, (g + 1) * gch)
        for c in range(z.shape[0] // lv):
            rows = slice(c * lv, (c + 1) * lv)
            mixed = _dot(w, vb[rows, cols]) + bst_ref[:lv, g:g + 1]
            g_ref[rows, cols] = (u[rows, cols] * mixed).astype(BF16)


def _sgu(dm, proj, lng, lnb, ws, bst, *, sample):
    if sample:
        b_n, n_grid, nc, lv, br, row0, n_rows = dm.Bs, ROW_TILE // dm.DS, 1, dm.DS, dm.DS, dm.Tp // dm.DS, ROW_TILE
    else:
        br = SGU_CHUNKS_PER_STEP * dm.GC
        b_n, n_grid, nc, lv, row0, n_rows = dm.Bp, dm.Bp, dm.S // br, dm.GC, 0, dm.Tp
    gh = dm.GH
    rb = lambda b, c: row0 + b * nc + c
    in_specs = [
        pl.BlockSpec((br, 2 * gh), lambda b, c: (rb(b, c), dm.off_z // (2 * gh))),
        pl.BlockSpec((1, gh), lambda b, c: (0, 0)),
        pl.BlockSpec((1, gh), lambda b, c: (0, 0)),
        pl.BlockSpec((dm.G, dm.GC, dm.GC), lambda b, c: (0, 0, 0)),
        pl.BlockSpec((dm.GC, LANES), lambda b, c: (0, 0)),
    ]
    out_specs = [pl.BlockSpec((br, gh), lambda b, c: (b * nc + c, 0))]
    out_shape = [jax.ShapeDtypeStruct((n_rows, gh), BF16)]
    if sample:
        out_specs.append(pl.BlockSpec((None, lv, gh), lambda b, c: (jnp.minimum(b, b_n - 1), 0, 0)))
        out_shape.append(jax.ShapeDtypeStruct((b_n, lv, gh), F32))
    return pl.pallas_call(
        functools.partial(_sgu_kernel, dm=dm, lv=lv, n_seq=b_n, emit_v=sample),
        grid=(n_grid, nc),
        in_specs=in_specs,
        out_specs=out_specs,
        out_shape=out_shape,
        compiler_params=_cparams(("arbitrary", "arbitrary"), 24),
        name="sgu_sample" if sample else "sgu_prompt",
    )(proj, lng, lnb, ws, bst)


def _merge_kernel(ap_ref, as_ref, gp_ref, gs_ref, ga_ref, gb_ref, wa_ref, wb_ref, o_ref):
    is_s = pl.program_id(1) == pl.num_programs(1) - 1
    pa = _dot(jnp.where(is_s, as_ref[...], ap_ref[...]), wa_ref[...])
    pb = _dot(jnp.where(is_s, gs_ref[...], gp_ref[...]), wb_ref[...])
    sa = 0.5 * jnp.tanh(0.5 * ga_ref[...].astype(F32)) + 0.5
    sb = 0.5 * jnp.tanh(0.5 * gb_ref[...].astype(F32)) + 0.5
    o_ref[...] = (sa * pa + sb * pb).astype(o_ref.dtype)


def _merge(dm, a_p, a_s, g_p, g_s, proj, wa, wb):
    tc, d, v, gh = dm.Tc, dm.D, dm.V, dm.GH
    tm = ROW_TILE
    tn = _pick_tile(d, 1024)
    np_tiles = dm.Tp // tm
    prompt = lambda j, i: (jnp.minimum(i, np_tiles - 1), 0)
    const = lambda j, i: (0, 0)
    blocks = 2 * tm * (v + gh) * 2 + 3 * tm * tn * 2 + (v + gh) * tn * 2
    return pl.pallas_call(
        _merge_kernel,
        grid=(d // tn, tc // tm),
        in_specs=[pl.BlockSpec((tm, v), prompt),
                  pl.BlockSpec((tm, v), const),
                  pl.BlockSpec((tm, gh), prompt),
                  pl.BlockSpec((tm, gh), const),
                  pl.BlockSpec((tm, tn), lambda j, i: (i, dm.off_ga // tn + j)),
                  pl.BlockSpec((tm, tn), lambda j, i: (i, dm.off_gb // tn + j)),
                  pl.BlockSpec((v, tn), lambda j, i: (0, j)),
                  pl.BlockSpec((gh, tn), lambda j, i: (0, j))],
        out_specs=pl.BlockSpec((tm, tn), lambda j, i: (i, j)),
        out_shape=jax.ShapeDtypeStruct((tc, d), BF16),
        compiler_params=_cparams(("arbitrary", "arbitrary"), (2 * blocks + 6 * tm * tn * 4) / MIB + 8),
        name="merge",
    )(a_p, a_s, g_p, g_s, proj, proj, wa, wb)


def _route(logits, eg, ne):
    epg = ne // eg
    lane = lax.broadcasted_iota(I32, logits.shape, 1)
    gmask = lane < eg
    gmax = jnp.max(jnp.where(gmask, logits, NEG), axis=1, keepdims=True)
    gexp = jnp.where(gmask, jnp.exp(jnp.minimum(logits - gmax, 0.0)), 0.0)
    pg = gexp / jnp.sum(gexp, axis=1, keepdims=True)
    p_grp = jnp.max(pg, axis=1, keepdims=True)
    g_sel = jnp.min(jnp.where(gmask & (pg == p_grp), lane, LANES), axis=1, keepdims=True)
    lo = eg + g_sel * epg
    emask = (lane >= lo) & (lane < lo + epg)
    emax = jnp.max(jnp.where(emask, logits, NEG), axis=1, keepdims=True)
    eexp = jnp.where(emask, jnp.exp(jnp.minimum(logits - emax, 0.0)), 0.0)
    pe = eexp / jnp.sum(eexp, axis=1, keepdims=True)
    p1 = jnp.max(jnp.where(emask, pe, -1.0), axis=1, keepdims=True)
    i1 = jnp.min(jnp.where(emask & (pe == p1), lane, LANES), axis=1, keepdims=True)
    rest = emask & (lane != i1)
    p2 = jnp.max(jnp.where(rest, pe, -1.0), axis=1, keepdims=True)
    i2 = jnp.min(jnp.where(rest & (pe == p2), lane, LANES), axis=1, keepdims=True)
    psum = p1 + p2
    eid = jnp.where(lane == 0, i1 - eg, jnp.where(lane == 1, i2 - eg, 0))
    gate = jnp.where(lane == 0, p_grp * (p1 / psum), jnp.where(lane == 1, p_grp * (p2 / psum), 0.0))
    return eid, gate


def _outproj_kernel(m_ref, x_ref, w_ref, g_ref, mod_ref, wr_ref, rb_ref,
                    x1_ref, hp_ref, eid_ref, gate_ref, acc_scr, hi_scr, lo_scr, *, dm):
    i = pl.program_id(0)
    t = jnp.maximum(i - 1, 0)
    is_s = t == dm.n_tiles - 1
    bp = jnp.minimum(t // (dm.S // ROW_TILE), dm.Bp - 1)
    cur, prev = i % 2, (i + 1) % 2
    g = g_ref[...]

    @pl.when(i == 0)
    def _no_previous_tile():
        acc_scr[1] = jnp.zeros(acc_scr.shape[1:], F32)

    def chunk(j):
        rows = pl.ds(j * ROW_CHUNK, ROW_CHUNK)
        idx = jnp.where(is_s, dm.Bp + j, bp)
        x1 = x_ref[rows, :] + mod_ref[GT1, pl.ds(idx, 1), :] * acc_scr[prev, rows, :]
        x1_ref[rows, :] = x1
        h = _rms_mod(x1, g, mod_ref[SC2, pl.ds(idx, 1), :], mod_ref[SH2, pl.ds(idx, 1), :])
        hi, lo = _split_bf16(h)
        hi_scr[rows, :] = hi
        lo_scr[rows, :] = lo
        hp_ref[rows, :] = _pack_pair(h)

    n_chunks = ROW_TILE // ROW_CHUNK
    n_slices = min(dm.D // 256, n_chunks)
    for s in range(n_slices):
        cols = slice(s * dm.D // n_slices, (s + 1) * dm.D // n_slices)
        acc_scr[cur, :, cols] = _dot(m_ref[...], w_ref[:, cols])
        for j in range(s * n_chunks // n_slices, (s + 1) * n_chunks // n_slices):
            chunk(j)
    w_hi, w_lo = _split_bf16(wr_ref[...])
    logits = _dot3(hi_scr[...], lo_scr[...], w_hi, w_lo) + rb_ref[...]
    eid, gate = _route(logits, dm.EG, dm.NE)
    eid_ref[...] = eid
    gate_ref[...] = gate


def _outproj(dm, l, merged, x, w_out, g2, mod, w_router, b_router):
    d, tc = dm.D, dm.Tc
    tile = lambda i: (jnp.maximum(i - 1, 0), 0)
    const = lambda i: (0, 0)
    return pl.pallas_call(
        functools.partial(_outproj_kernel, dm=dm),
        grid=(dm.n_tiles + 1,),
        in_specs=[pl.BlockSpec((ROW_TILE, d), lambda i: (jnp.minimum(i, dm.n_tiles - 1), 0)),
                  pl.BlockSpec((ROW_TILE, d), tile),
                  pl.BlockSpec((d, d), const),
                  pl.BlockSpec((None, 1, d), lambda i: (l, 0, 0)),
                  pl.BlockSpec((None, 6, dm.n_mod, d), lambda i: (l, 0, 0, 0)),
                  pl.BlockSpec((d, LANES), const),
                  pl.BlockSpec((1, LANES), const)],
        out_specs=[pl.BlockSpec((ROW_TILE, d), tile),
                   pl.BlockSpec((ROW_TILE, d // 2), tile),
                   pl.BlockSpec((ROW_TILE, LANES), tile),
                   pl.BlockSpec((ROW_TILE, LANES), tile)],
        out_shape=[jax.ShapeDtypeStruct((tc, d), F32),
                   jax.ShapeDtypeStruct((tc, d // 2), U32),
                   jax.ShapeDtypeStruct((tc, LANES), I32),
                   jax.ShapeDtypeStruct((tc, LANES), F32)],
        scratch_shapes=[pltpu.VMEM((2, ROW_TILE, d), F32), pltpu.VMEM((ROW_TILE, d), BF16),
                        pltpu.VMEM((ROW_TILE, d), BF16)],
        compiler_params=_cparams(("arbitrary",), (4 * d * d + ROW_TILE * d * 44) / MIB + 12),
        name="out_proj",
    )(merged, x, w_out, g2, mod, w_router, b_router)


def _rank_kernel(eid_ref, rank_ref, cnt_ref, run_scr, start_scr):
    sweep, i = pl.program_id(0), pl.program_id(1)

    @pl.when(i == 0)
    def _init():
        @pl.when(sweep == 1)
        def _starts():
            padded = jnp.ceil(run_scr[...] * (1.0 / MOE_BLOCK)) * MOE_BLOCK
            earlier = (lax.broadcasted_iota(I32, (LANES, LANES), 0)
                       < lax.broadcasted_iota(I32, (LANES, LANES), 1)).astype(BF16)
            p_hi, p_lo = _split_bf16(jnp.broadcast_to(padded, (8, LANES)))
            start_scr[...] = (_dot(p_hi, earlier) + _dot(p_lo, earlier))[0:1, :]
            cnt_ref[...] = run_scr[...].astype(I32)

        @pl.when(sweep == 0)
        def _no_starts_yet():
            start_scr[...] = jnp.zeros_like(start_scr)

        run_scr[...] = jnp.zeros_like(run_scr)

    eid = eid_ref[...]
    lane = lax.broadcasted_iota(I32, eid.shape, 1)
    e0 = lane == eid[:, 0:1]
    e1 = lane == eid[:, 1:2]
    hot = (e0 | e1).astype(BF16)
    n = eid.shape[0]
    strict = (lax.broadcasted_iota(I32, (n, n), 1) < lax.broadcasted_iota(I32, (n, n), 0)).astype(BF16)
    before = _dot(strict, hot) + run_scr[...]
    slot = before + start_scr[...]
    s0 = jnp.sum(jnp.where(e0, slot, 0.0), axis=1, keepdims=True)
    s1 = jnp.sum(jnp.where(e1, slot, 0.0), axis=1, keepdims=True)
    rank_ref[...] = jnp.where(lane == 0, s0, jnp.where(lane == 1, s1, 0.0)).astype(I32)
    run_scr[...] = run_scr[...] + jnp.sum(hot.astype(F32), axis=0, keepdims=True)


def _rank(dm, eid):
    return pl.pallas_call(
        _rank_kernel,
        grid=(2, dm.n_tiles),
        in_specs=[pl.BlockSpec((ROW_TILE, LANES), lambda s, i: (i, 0))],
        out_specs=[pl.BlockSpec((ROW_TILE, LANES), lambda s, i: (s * i, 0)),
                   pl.BlockSpec((1, LANES), lambda s, i: (0, 0))],
        out_shape=[jax.ShapeDtypeStruct((dm.Tc, LANES), I32), jax.ShapeDtypeStruct((1, LANES), I32)],
        scratch_shapes=[pltpu.VMEM((1, LANES), F32), pltpu.VMEM((1, LANES), F32)],
        compiler_params=_cparams(("arbitrary", "arbitrary"), 16),
        name="moe_rank",
    )(eid)


def _dispatch_kernel(tail_ref, dest_ref, h_ref, xs_ref, zero_scr, sem):
    @pl.when(pl.program_id(0) == 0)
    def _clear_tails():
        zero_scr[...] = jnp.zeros_like(zero_scr)
        n_e = tail_ref.shape[0]

        def tail_copy(e):
            start = pl.multiple_of(tail_ref[e], MOE_BLOCK)
            return pltpu.make_async_copy(zero_scr, xs_ref.at[pl.ds(start, MOE_BLOCK)], sem)

        for e in range(n_e):
            tail_copy(e).start()
        for e in range(n_e):
            tail_copy(e).wait()

    for r in range(ROW_TILE):
        for k in range(TOP_K):
            pltpu.make_async_copy(h_ref.at[pl.ds(r, 1)], xs_ref.at[pl.ds(dest_ref[TOP_K * r + k], 1)],
                                  sem).start(priority=k)
    for k in range(TOP_K):
        pltpu.make_async_copy(h_ref, xs_ref.at[pl.ds(0, ROW_TILE)], sem).wait()


def _dispatch(dm, tails, dest, hp):
    d2 = dm.D // 2
    return pl.pallas_call(
        _dispatch_kernel,
        grid_spec=pltpu.PrefetchScalarGridSpec(
            num_scalar_prefetch=1,
            grid=(dm.n_tiles,),
            in_specs=[pl.BlockSpec((ROW_TILE * TOP_K,), lambda i, t: (i,), memory_space=pltpu.SMEM),
                      pl.BlockSpec((ROW_TILE, d2), lambda i, t: (i, 0))],
            out_specs=pl.BlockSpec(memory_space=pl.ANY),
            scratch_shapes=[pltpu.VMEM((MOE_BLOCK, d2), U32), pltpu.SemaphoreType.DMA]),
        out_shape=jax.ShapeDtypeStruct((dm.cap + dm.NE * MOE_BLOCK, d2), U32),
        compiler_params=_cparams(("arbitrary",), 16),
        name="moe_dispatch",
    )(tails, dest, hp)


def _expert_kernel(be_ref, nu_ref, nx_ref, sl_ref, xs_ref, wg_hbm, wu_hbm, wd_hbm, ys_ref,
                   wg_f, wu_f, wd_f, wg_s, wu_s, wd_s, sem, *, layer):
    j = pl.program_id(0)
    e, slot = be_ref[j], sl_ref[j]
    used = j < nu_ref[0]
    first = used & ((j == 0) | (e != be_ref[jnp.maximum(j - 1, 0)]))

    def weight_copies(expert, s):
        return [pltpu.make_async_copy(src.at[layer, expert], dst.at[s], sem.at[s])
                for src, dst in ((wg_hbm, wg_f), (wu_hbm, wu_f), (wd_hbm, wd_f))]

    @pl.when(j == 0)
    def _first_expert():
        for cp in weight_copies(e, slot):
            cp.start()

    @pl.when(first)
    def _switch_expert():
        for cp in weight_copies(e, slot):
            cp.wait()
        wg_s[...] = wg_f[slot].astype(BF16)
        wu_s[...] = wu_f[slot].astype(BF16)
        wd_s[...] = wd_f[slot].astype(BF16)

        @pl.when(nx_ref[j] >= 0)
        def _prefetch_next():
            for cp in weight_copies(nx_ref[j], 1 - slot):
                cp.start()

    @pl.when(used)
    def _compute():
        d2 = xs_ref.shape[1]
        lo, hi = _unpack_pair(xs_ref[...])
        lo, hi = lo.astype(BF16), hi.astype(BF16)
        hg = _dot(lo, wg_s[:d2, :]) + _dot(hi, wg_s[d2:, :])
        hu = _dot(lo, wu_s[:d2, :]) + _dot(hi, wu_s[d2:, :])
        act = (hg * jax.nn.sigmoid(hg) * hu).astype(BF16)
        ys_ref[...] = _pack_pair(_dot(act, wd_s[...]))


def _experts(dm, l, blk_e, n_used, blk_next, blk_slot, xs, w_gate, w_up, w_down):
    d, de, d2 = dm.D, dm.DE, dm.D // 2
    nb = dm.cap // MOE_BLOCK
    rows = lambda j, be, nu, nx, sl: (jnp.minimum(j, nu[0] - 1), 0)
    hbm = pl.BlockSpec(memory_space=pl.ANY)
    return pl.pallas_call(
        functools.partial(_expert_kernel, layer=l),
        grid_spec=pltpu.PrefetchScalarGridSpec(
            num_scalar_prefetch=4,
            grid=(nb,),
            in_specs=[pl.BlockSpec((MOE_BLOCK, d2), rows), hbm, hbm, hbm],
            out_specs=pl.BlockSpec((MOE_BLOCK, d2), rows),
            scratch_shapes=[pltpu.VMEM((2, d, de), F32), pltpu.VMEM((2, d, de), F32), pltpu.VMEM((2, de, d), F32),
                            pltpu.VMEM((d, de), BF16), pltpu.VMEM((d, de), BF16), pltpu.VMEM((de, d), BF16),
                            pltpu.SemaphoreType.DMA((2,))]),
        out_shape=jax.ShapeDtypeStruct((dm.cap, d2), U32),
        compiler_params=_cparams(("arbitrary",), (3 * d * de * (2 * 4 + 2) + MOE_BLOCK * d * 28) / MIB + 8),
        name="moe_experts",
    )(blk_e, n_used, blk_next, blk_slot, xs, w_gate, w_up, w_down)


def _combine_kernel(*refs, dm, last):
    if last:
        dest_ref, destn_ref, ys_ref, x_ref, gate_ref, mod_ref, g_ref, yp_ref, ysm_ref, ybuf, y_scr, sem = refs
    else:
        (dest_ref, destn_ref, ys_ref, x_ref, gate_ref, mod_ref, g_ref, modn_ref, w_ref,
         o_ref, h_ref, p_ref, ybuf, lo_scr, sem) = refs
    i = pl.program_id(0)
    is_s = i == dm.n_tiles - 1
    bp = jnp.minimum(i // (dm.S // ROW_TILE), dm.Bp - 1)
    d2 = dm.D // 2
    g = g_ref[...]
    slot = i % 2

    def issue(idx_ref, to_slot, r0, n):
        for r in range(n):
            for k in range(TOP_K):
                pltpu.make_async_copy(ys_ref.at[pl.ds(idx_ref[TOP_K * (r0 + r) + k], 1)],
                                      ybuf.at[to_slot, k, pl.ds(r0 + r, 1)], sem.at[to_slot]).start(priority=k)

    def wait_slot(s):
        for k in range(TOP_K):
            pltpu.make_async_copy(ys_ref.at[pl.ds(0, ROW_TILE)], ybuf.at[s, k], sem.at[s]).wait()

    @pl.when(i == 0)
    def _first_tile():
        issue(dest_ref, 0, 0, ROW_TILE)

    wait_slot(slot)

    def chunk(j):
        rows = pl.ds(j * ROW_CHUNK, ROW_CHUNK)
        issue(destn_ref, 1 - slot, j * ROW_CHUNK, ROW_CHUNK)
        idx = jnp.where(is_s, dm.Bp + j, bp)
        gt = mod_ref[GT2, pl.ds(idx, 1), :]
        gates = gate_ref[rows, :]
        g0, g1 = gates[:, 0:1], gates[:, 1:2]
        lo0, hi0 = _unpack_pair(ybuf[slot, 0, rows, :])
        lo1, hi1 = _unpack_pair(ybuf[slot, 1, rows, :])
        x_lo = x_ref[rows, :d2] + gt[:, :d2] * (g0 * lo0 + g1 * lo1)
        x_hi = x_ref[rows, d2:] + gt[:, d2:] * (g0 * hi0 + g1 * hi1)
        ms = (jnp.sum(x_lo * x_lo, axis=-1, keepdims=True)
              + jnp.sum(x_hi * x_hi, axis=-1, keepdims=True)) * (1.0 / dm.D)
        rs = lax.rsqrt(ms + EPS)
        if last:
            y_scr[rows, :d2] = x_lo * rs * g[:, :d2]
            y_scr[rows, d2:] = x_hi * rs * g[:, d2:]
        else:
            o_ref[rows, :d2] = x_lo
            o_ref[rows, d2:] = x_hi
            sc = modn_ref[SC1, pl.ds(idx, 1), :]
            sh = modn_ref[SH1, pl.ds(idx, 1), :]
            for half, xh in ((slice(0, d2), x_lo), (slice(d2, dm.D), x_hi)):
                h = xh * rs * g[:, half] * (1.0 + sc[:, half]) + sh[:, half]
                hi, lo = _split_bf16(h)
                h_ref[rows, half] = hi
                lo_scr[rows, half] = lo

    for j in range(ROW_TILE // ROW_CHUNK):
        chunk(j)

    @pl.when(is_s)
    def _drain():
        wait_slot(1 - slot)

    if last:
        @pl.when(is_s)
        def _sample_rows():
            ysm_ref[...] = y_scr[...]

        @pl.when(jnp.logical_not(is_s))
        def _prompt_rows():
            yp_ref[...] = y_scr[...]
    else:
        w_hi, w_lo = _gate_weight(w_ref)
        p_ref[...] = _dot3(h_ref[...], lo_scr[...], w_hi, w_lo)


def _combine(dm, l, dest, ys, x1, gate, mod, g_next, w_in=None):
    d = dm.D
    last = w_in is None
    tile = lambda i: (i, 0)
    np_tiles = dm.Tp // ROW_TILE
    in_specs = [pl.BlockSpec((ROW_TILE * TOP_K,), lambda i: (i,), memory_space=pltpu.SMEM),
                pl.BlockSpec((ROW_TILE * TOP_K,), lambda i: (jnp.minimum(i + 1, dm.n_tiles - 1),),
                             memory_space=pltpu.SMEM),
                pl.BlockSpec(memory_space=pl.ANY),
                pl.BlockSpec((ROW_TILE, d), tile),
                pl.BlockSpec((ROW_TILE, LANES), tile),
                pl.BlockSpec((None, 6, dm.n_mod, d), lambda i: (l, 0, 0, 0))]
    scratch = [pltpu.VMEM((2, TOP_K, ROW_TILE, d // 2), U32)]
    if last:
        args = (dest, dest, ys, x1, gate, mod, g_next)
        in_specs.append(pl.BlockSpec((1, d), lambda i: (0, 0)))
        out_specs = [pl.BlockSpec((ROW_TILE, d), lambda i: (jnp.minimum(i, np_tiles - 1), 0)),
                     pl.BlockSpec((ROW_TILE, d), lambda i: (0, 0))]
        out_shape = [jax.ShapeDtypeStruct((dm.Tp, d), F32), jax.ShapeDtypeStruct((ROW_TILE, d), F32)]
        scratch.append(pltpu.VMEM((ROW_TILE, d), F32))
    else:
        args = (dest, dest, ys, x1, gate, mod, g_next, mod, w_in)
        in_specs += [pl.BlockSpec((None, 1, d), lambda i: (l + 1, 0, 0)),
                     pl.BlockSpec((None, 6, dm.n_mod, d), lambda i: (l + 1, 0, 0, 0)),
                     pl.BlockSpec((None, 2 * dm.H, d), lambda i: (l + 1, dm.off_z // (2 * dm.H), 0))]
        out_specs = [pl.BlockSpec((ROW_TILE, d), tile), pl.BlockSpec((ROW_TILE, d), tile),
                     pl.BlockSpec((ROW_TILE, LANES), tile)]
        out_shape = [jax.ShapeDtypeStruct((dm.Tc, d), F32), jax.ShapeDtypeStruct((dm.Tc, d), BF16),
                     jax.ShapeDtypeStruct((dm.Tc, LANES), F32)]
        scratch.append(pltpu.VMEM((ROW_TILE, d), BF16))
    scratch.append(pltpu.SemaphoreType.DMA((2,)))
    return pl.pallas_call(
        functools.partial(_combine_kernel, dm=dm, last=last),
        grid=(dm.n_tiles,),
        in_specs=in_specs,
        out_specs=out_specs,
        out_shape=out_shape,
        scratch_shapes=scratch,
        compiler_params=_cparams(("arbitrary",), ROW_TILE * d * 40 / MIB + 16),
        name="moe_combine_last" if last else "moe_combine",
    )(*args)


def _pad_lanes(x, n=LANES):
    return jnp.pad(x, [(0, 0)] * (x.ndim - 1) + [(0, n - x.shape[-1])])


def _moe_plan(dm, slots, counts):
    counts = counts[0, :dm.NE]
    padded = (counts + MOE_BLOCK - 1) // MOE_BLOCK * MOE_BLOCK
    pad_end = jnp.cumsum(padded)
    dest = slots[:, :TOP_K].reshape(-1)
    nb = dm.cap // MOE_BLOCK
    n_used = (pad_end[-1] // MOE_BLOCK).astype(I32)
    starts = jnp.minimum(jnp.arange(nb, dtype=I32), n_used - 1) * MOE_BLOCK
    blk_e = jnp.minimum(jnp.sum(pad_end[None, :] <= starts[:, None], axis=1), dm.NE - 1).astype(I32)
    ids = jnp.arange(dm.NE, dtype=I32)
    spare = dm.cap + ids * MOE_BLOCK
    tails = jnp.where(padded > 0, pad_end - MOE_BLOCK, spare).astype(I32)
    live = jnp.where(counts > 0, ids, dm.NE)
    nxt = lax.cummin(jnp.concatenate([live[1:], jnp.full((1,), dm.NE, I32)]), reverse=True)
    blk_next = jnp.where(nxt[blk_e] < dm.NE, nxt[blk_e], -1).astype(I32)
    blk_slot = ((jnp.cumsum(counts > 0) - 1)[blk_e] % 2).astype(I32)
    return dest, blk_e, n_used.reshape(1), tails, blk_next, blk_slot


def kernel(x_prompt, x_sample, c_prompt, c_sample, state_mlstm_C, state_mlstm_n, state_mlstm_m, w_ada, b_ada, norm1_g, norm2_g, w_in, b_igate, b_fgate, mh_norm_g, sgu_ln_g, sgu_ln_b, w_spatial, b_spatial, w_branch_a, w_branch_b, w_out, w_router_group, b_router_group, w_router_expert, b_router_expert, w_expert_gate, w_expert_up, w_expert_down, final_norm_g):
    bp, s, d = x_prompt.shape
    bs, ds_, _ = x_sample.shape
    depth = w_in.shape[0]
    h_n = b_igate.shape[1]
    dv, dk = state_mlstm_C.shape[-2:]
    qk, v = h_n * dk, h_n * dv
    gh = (w_in.shape[2] - (2 * qk + 2 * v + 2 * h_n + 2 * d)) // 2
    dm = Dims(Bp=bp, S=s, Bs=bs, DS=ds_, D=d, DEPTH=depth, H=h_n, DK=dk, DV=dv, GH=gh,
              G=w_spatial.shape[1], GC=w_spatial.shape[2], EG=w_router_group.shape[-1],
              NE=w_router_expert.shape[-1], DE=w_expert_gate.shape[-1])
    assert dm.DS == ROW_CHUNK and dm.Ts <= ROW_TILE and s % ROW_TILE == 0 and s % MLSTM_CHUNK == 0
    assert dm.EG + dm.NE <= LANES and 2 * h_n <= LANES and (2 * h_n) % 8 == 0 and dm.Bp + dm.Bs <= 32
    w_in = jnp.swapaxes(w_in, 1, 2)
    assert dm.off_v % v == 0 and dm.off_o % v == 0 and dm.off_z % (2 * gh) == 0 and d % 256 == 0

    c_all = jnp.concatenate([c_prompt, c_sample, jnp.zeros((32 - bp - bs, d), F32)], axis=0)
    mod = _adaln(c_all, w_ada, b_ada)[:, :bp + bs].reshape(depth, bp + bs, 6, d)
    mod = jnp.pad(mod, ((0, 0), (0, dm.n_mod - bp - bs), (0, 0), (0, 0))).transpose(0, 2, 1, 3)

    xs_rows = jnp.pad(x_sample.reshape(dm.Ts, d), ((0, ROW_TILE - dm.Ts), (0, 0)))
    x = (x_prompt.reshape(dm.Tp, d), xs_rows)
    zeros_c = jnp.zeros((1, bp, h_n, dv, dk), F32)
    zeros_n = jnp.zeros((1, bp, h_n, dk), F32)
    zeros_m = jnp.zeros((1, bp, 1, LANES), F32)
    m_in = _pad_lanes(state_mlstm_m)[:, :, None, :]
    g1 = norm1_g.reshape(depth, 1, d)
    out_c_p, out_n_p, out_m_p, out_c_s, out_n_s, out_m_s, out_v = [], [], [], [], [], [], []

    for l in range(depth):
        if l == 0:
            hb, gates, x = _norm_proj(dm, l, x, g1, mod, w_in, SC1, SH1)
        proj = _in_proj(dm, l, hb, w_in)

        bias = _pad_lanes(jnp.concatenate([b_igate[l], b_fgate[l]])[None, :])
        mhg = mh_norm_g[l][None, :]
        a_p, c_p, n_p, m_p = _mlstm(dm, proj, gates, bias, mhg, zeros_c, zeros_n, zeros_m, 0, sample=False)
        a_s, c_s, n_s, m_s = _mlstm(dm, proj, gates, bias, mhg, state_mlstm_C, state_mlstm_n, m_in, l,
                                    sample=True)
        lng, lnb = sgu_ln_g[l][None, :], sgu_ln_b[l][None, :]
        bst = _pad_lanes(b_spatial[l].T)
        (g_p,) = _sgu(dm, proj, lng, lnb, w_spatial[l], bst, sample=False)
        g_s, v_rows = _sgu(dm, proj, lng, lnb, w_spatial[l], bst, sample=True)

        merged = _merge(dm, a_p, a_s, g_p, g_s, proj, w_branch_a[l].astype(BF16), w_branch_b[l].astype(BF16))
        w_router = _pad_lanes(jnp.concatenate([w_router_group[l], w_router_expert[l]], axis=1))
        b_router = _pad_lanes(jnp.concatenate([b_router_group[l], b_router_expert[l]])[None, :])
        x1, hp, eid, gate = _outproj(dm, l, merged, x, w_out[l].astype(BF16), norm2_g.reshape(depth, 1, d),
                                     mod, w_router, b_router)

        slots, counts = _rank(dm, eid)
        dest, blk_e, n_used, tails, blk_next, blk_slot = _moe_plan(dm, slots, counts)
        xs = _dispatch(dm, tails, dest, hp)
        ys = _experts(dm, l, blk_e, n_used, blk_next, blk_slot, xs,
                      w_expert_gate, w_expert_up, w_expert_down)
        if l + 1 < depth:
            x, hb, gates = _combine(dm, l, dest, ys, x1, gate, mod, g1, w_in)
        else:
            y_p, y_s = _combine(dm, l, dest, ys, x1, gate, mod, final_norm_g[None, :])

        out_c_p.append(c_p)
        out_n_p.append(n_p)
        out_m_p.append(m_p[:, 0, :h_n])
        out_c_s.append(c_s)
        out_n_s.append(n_s)
        out_m_s.append(m_s[:, 0, :h_n])
        out_v.append(v_rows)

    return (y_p.reshape(bp, s, d), y_s[:dm.Ts].reshape(bs, ds_, d),
            jnp.stack(out_c_p), jnp.stack(out_n_p), jnp.stack(out_m_p),
            jnp.stack(out_c_s), jnp.stack(out_n_s), jnp.stack(out_m_s), jnp.stack(out_v))
```

```python
import functools
from typing import NamedTuple

import jax
import jax.numpy as jnp
from jax import lax
from jax.experimental import pallas as pl
from jax.experimental.pallas import tpu as pltpu

F32, BF16, U32, I32 = jnp.float32, jnp.bfloat16, jnp.uint32, jnp.int32
EPS = 1e-6
NEG = -1e30
LANES = 128
ROW_TILE = 512
ROW_CHUNK = 16
MLSTM_CHUNK = 256
MOE_BLOCK = 512
TOP_K = 2
CHUNK_UNROLL = 2
SGU_CHUNKS_PER_STEP = 4
MIB = 1024 * 1024
SH1, SC1, GT1, SH2, SC2, GT2 = range(6)


class Dims(NamedTuple):
    Bp: int
    S: int
    Bs: int
    DS: int
    D: int
    DEPTH: int
    H: int
    DK: int
    DV: int
    GH: int
    G: int
    GC: int
    EG: int
    NE: int
    DE: int

    @property
    def QK(self):
        return self.H * self.DK

    @property
    def V(self):
        return self.H * self.DV

    @property
    def Tp(self):
        return self.Bp * self.S

    @property
    def Ts(self):
        return self.Bs * self.DS

    @property
    def Tc(self):
        return self.Tp + ROW_TILE

    @property
    def n_tiles(self):
        return self.Tc // ROW_TILE

    @property
    def n_mod(self):
        return self.Bp + ROW_TILE // ROW_CHUNK

    @property
    def off_v(self):
        return 2 * self.QK

    @property
    def off_o(self):
        return 2 * self.QK + self.V

    @property
    def off_z(self):
        return 2 * self.QK + 2 * self.V

    @property
    def off_ga(self):
        return self.off_z + 2 * self.GH

    @property
    def off_gb(self):
        return self.off_ga + self.D

    @property
    def n_main(self):
        return self.off_gb + self.D

    @property
    def cap(self):
        tk = self.Tc * TOP_K
        return -(-(tk + self.NE * (MOE_BLOCK - 1)) // MOE_BLOCK) * MOE_BLOCK


def _cparams(semantics, vmem_mib):
    return pltpu.CompilerParams(dimension_semantics=semantics, vmem_limit_bytes=int(vmem_mib * MIB))


def _pick_tile(n, cap, unit=256):
    best = None
    for t in range(unit, min(n, cap) + 1, unit):
        if n % t == 0:
            best = t
    assert best is not None, (n, cap)
    return best


def _dot(a, b):
    return jnp.dot(a, b, preferred_element_type=F32)


def _split_bf16(x):
    hi = x.astype(BF16)
    lo = (x - hi.astype(F32)).astype(BF16)
    return hi, lo


def _dot3(a_hi, a_lo, w_hi, w_lo):
    return _dot(a_hi, w_hi) + _dot(a_hi, w_lo) + _dot(a_lo, w_hi)


def _gate_weight(w_ref):
    rows, d = w_ref.shape
    w = jnp.concatenate([w_ref[...], jnp.zeros((LANES - rows, d), F32)], axis=0).T
    return _split_bf16(w)


def _rms_mod(x, g, sc, sh):
    y = x * lax.rsqrt(jnp.mean(x * x, axis=-1, keepdims=True) + EPS) * g
    return y * (1.0 + sc) + sh


def _pack_pair(x):
    d2 = x.shape[-1] // 2
    bits = lax.bitcast_convert_type(x.astype(BF16).astype(F32), U32)
    return (bits[:, :d2] >> 16) | (bits[:, d2:] & jnp.uint32(0xFFFF0000))


def _unpack_pair(p):
    lo = lax.bitcast_convert_type(p << 16, F32)
    hi = lax.bitcast_convert_type(p & jnp.uint32(0xFFFF0000), F32)
    return lo, hi


def _adaln_kernel(c_ref, w_ref, b_ref, o_ref):
    c = c_ref[...]
    s = (c * jax.nn.sigmoid(c)).astype(BF16)
    o_ref[...] = _dot(s, w_ref[...].astype(BF16)) + b_ref[...]


def _adaln(c_all, w_ada, b_ada):
    depth, d, n6 = w_ada.shape
    r = c_all.shape[0]
    tn = _pick_tile(n6, 1024)
    return pl.pallas_call(
        _adaln_kernel,
        grid=(depth, n6 // tn),
        in_specs=[pl.BlockSpec((r, d), lambda l, j: (0, 0)),
                  pl.BlockSpec((None, d, tn), lambda l, j: (l, 0, j)),
                  pl.BlockSpec((None, 1, tn), lambda l, j: (l, 0, j))],
        out_specs=pl.BlockSpec((None, r, tn), lambda l, j: (l, 0, j)),
        out_shape=jax.ShapeDtypeStruct((depth, r, n6), F32),
        compiler_params=_cparams(("arbitrary", "arbitrary"), 2 * d * tn * 4 / MIB + 3 * d * tn * 2 / MIB + 8),
        name="adaln",
    )(c_all, w_ada, b_ada.reshape(depth, 1, n6))


def _norm_proj_kernel(*refs, first, dm, k_sc, k_sh):
    if first:
        xp_ref, xs_ref, g_ref, mod_ref, w_ref, h_ref, p_ref, lo_scr = refs
    else:
        x_ref, g_ref, mod_ref, w_ref, h_ref, p_ref, lo_scr = refs
    i = pl.program_id(0)
    is_s = i == dm.n_tiles - 1
    bp = jnp.minimum(i // (dm.S // ROW_TILE), dm.Bp - 1)
    g = g_ref[...]

    def chunk(j, carry):
        rows = pl.ds(pl.multiple_of(j * ROW_CHUNK, ROW_CHUNK), ROW_CHUNK)
        if first:
            x = jnp.where(is_s, xs_ref[rows, :], xp_ref[rows, :])
        else:
            x = x_ref[rows, :]
        idx = jnp.where(is_s, dm.Bp + j, bp)
        h = _rms_mod(x, g, mod_ref[k_sc, pl.ds(idx, 1), :], mod_ref[k_sh, pl.ds(idx, 1), :])
        hi, lo = _split_bf16(h)
        h_ref[rows, :] = hi
        lo_scr[rows, :] = lo
        return carry

    lax.fori_loop(0, ROW_TILE // ROW_CHUNK, chunk, 0, unroll=CHUNK_UNROLL)
    w_hi, w_lo = _gate_weight(w_ref)
    p_ref[...] = _dot3(h_ref[...], lo_scr[...], w_hi, w_lo)


def _norm_proj(dm, l, x_in, g, mod, w_in, k_sc, k_sh):
    d, tc, nt = dm.D, dm.Tc, dm.n_tiles
    first = isinstance(x_in, tuple)
    np_tiles = dm.Tp // ROW_TILE
    tile = lambda i: (i, 0)
    if first:
        x_args = list(x_in)
        x_specs = [pl.BlockSpec((ROW_TILE, d), lambda i: (jnp.minimum(i, np_tiles - 1), 0)),
                   pl.BlockSpec((ROW_TILE, d), lambda i: (0, 0))]
    else:
        x_args = [x_in]
        x_specs = [pl.BlockSpec((ROW_TILE, d), tile)]
    out_shape = [jax.ShapeDtypeStruct((tc, d), BF16), jax.ShapeDtypeStruct((tc, LANES), F32)]
    out_specs = [pl.BlockSpec((ROW_TILE, d), tile), pl.BlockSpec((ROW_TILE, LANES), tile)]
    return pl.pallas_call(
        functools.partial(_norm_proj_kernel, first=first, dm=dm, k_sc=k_sc, k_sh=k_sh),
        grid=(nt,),
        in_specs=x_specs + [pl.BlockSpec((None, 1, d), lambda i: (l, 0, 0)),
                            pl.BlockSpec((None, 6, dm.n_mod, d), lambda i: (l, 0, 0, 0)),
                            pl.BlockSpec((None, 2 * dm.H, d), lambda i: (l, dm.off_z // (2 * dm.H), 0))],
        out_specs=out_specs,
        out_shape=out_shape,
        scratch_shapes=[pltpu.VMEM((ROW_TILE, d), BF16)],
        compiler_params=_cparams(("arbitrary",), ROW_TILE * d * 40 / MIB + 16),
        name="norm_proj",
    )(*x_args, g, mod, w_in)


def _in_proj_kernel(a_ref, w_ref, wn_ref, o_ref, w_scr, *, n_aligned, shift):
    j, i = pl.program_id(0), pl.program_id(1)

    groups = w_ref.shape[0] // LANES

    @pl.when((i == 0) & (j < n_aligned))
    def _cast():
        for g in range(groups):
            w_scr[:, g * LANES:(g + 1) * LANES] = w_ref[g * LANES:(g + 1) * LANES, :].T.astype(BF16)

    @pl.when((i == 0) & (j >= n_aligned))
    def _cast_shifted():
        for g in range(groups):
            if g + 1 < groups:
                rows = w_ref[g * LANES + shift:(g + 1) * LANES + shift, :]
            else:
                rows = jnp.concatenate([w_ref[g * LANES + shift:, :], wn_ref[...]], axis=0)
            w_scr[:, g * LANES:(g + 1) * LANES] = rows.T.astype(BF16)

    o_ref[...] = _dot(a_ref[...], w_scr[...]).astype(o_ref.dtype)


def _in_proj(dm, l, a, w_in_t):
    m, k = a.shape
    off_if, n = dm.off_z, dm.n_main
    shift = 2 * dm.H
    tn = max(t for t in range(LANES, 1024 + 1, LANES) if off_if % t == 0 and n % t == 0)
    tm = _pick_tile(m, 1536)
    vmem = (2 * tm * k * 2 + 2 * k * (tn + shift) * 4 + k * tn * 2 + 2 * tm * tn * 2 + 2 * tm * tn * 4) / MIB + 8
    return pl.pallas_call(
        functools.partial(_in_proj_kernel, n_aligned=off_if // tn, shift=shift),
        grid=(n // tn, m // tm),
        in_specs=[pl.BlockSpec((tm, k), lambda j, i: (i, 0)),
                  pl.BlockSpec((None, tn, k), lambda j, i: (l, j, 0)),
                  pl.BlockSpec((None, shift, k), lambda j, i: (l, (j + 1) * (tn // shift), 0))],
        out_specs=pl.BlockSpec((tm, tn), lambda j, i: (i, j)),
        out_shape=jax.ShapeDtypeStruct((m, n), BF16),
        scratch_shapes=[pltpu.VMEM((k, tn), BF16)],
        compiler_params=_cparams(("arbitrary", "arbitrary"), vmem),
        name="in_proj",
    )(a, w_in_t, w_in_t)


def _pad_rows(x, rows):
    if x.shape[0] == rows:
        return x
    return jnp.concatenate([x, jnp.zeros((rows - x.shape[0], x.shape[1]), x.dtype)], axis=0)


def _mlstm_kernel(*refs, dm, lv, n_seq):
    a_ref = refs[10]
    seq_ok = pl.program_id(0) < n_seq

    @pl.when(seq_ok)
    def _run():
        _mlstm_step(*refs, dm=dm, lv=lv)

    @pl.when(jnp.logical_not(seq_ok))
    def _fill():
        a_ref[...] = jnp.zeros_like(a_ref)


def _mlstm_step(q_ref, k_ref, v_ref, o_ref, gt_ref, bias_ref, mhg_ref, c0_ref, n0_ref, m0_ref,
                a_ref, cout_ref, nout_ref, mout_ref, st_scr, m_scr, *, dm, lv):
    h_n, dk, dv = dm.H, dm.DK, dm.DV
    L = max(lv, LANES)
    c = pl.program_id(1)

    scale = dk ** -0.5

    @pl.when(c == 0)
    def _init():
        for h in range(h_n):
            st_scr[h, :, :dv] = c0_ref[h].T * (1.0 / scale)
            st_scr[h, :, dv:] = jnp.broadcast_to(n0_ref[h:h + 1, :] * (1.0 / scale), (LANES, dk)).T
        m_scr[...] = m0_ref[...]

    row = lax.broadcasted_iota(I32, (L, L), 0)
    col = lax.broadcasted_iota(I32, (L, L), 1)
    causal = col <= row
    lane = lax.broadcasted_iota(I32, (L, LANES), 1)
    rowl = lax.broadcasted_iota(I32, (L, LANES), 0)

    xg = _pad_rows(gt_ref[...], L) + bias_ref[...]
    f_log = jnp.minimum(xg, 0.0) - jnp.log1p(jnp.exp(-jnp.abs(xg)))
    gl = jnp.where(lane < h_n, xg, f_log)
    gl = jnp.where(rowl < lv, gl, 0.0)
    tri = causal.astype(BF16)
    g_hi = gl.astype(BF16)
    r1 = gl - g_hi.astype(F32)
    g_mid = r1.astype(BF16)
    g_lo = (r1 - g_mid.astype(F32)).astype(BF16)
    cum = _dot(tri, g_hi) + _dot(tri, g_mid) + _dot(tri, g_lo)
    b_all = pltpu.roll(cum, LANES - h_n, axis=1)
    a_all = jnp.where(rowl < lv, gl - b_all, NEG)

    kt_all = _pad_rows(k_ref[...], L).T
    q_all = _pad_rows(q_ref[...], L)
    v_all = _pad_rows(v_ref[...], L)
    a_rows = a_all.T
    m_old = m_scr[...]
    m_new = m_old
    lane1 = lax.broadcasted_iota(I32, (1, LANES), 1)
    ones_blk = jnp.ones((L, LANES), BF16)
    v_blocks = dv // LANES

    for h in range(h_n):
        a_row = a_rows[h:h + 1, :]
        a_rep = jnp.broadcast_to(a_all[:, h:h + 1], (L, LANES))
        b_rep = jnp.broadcast_to(b_all[:, h:h + 1], (L, LANES))
        m_prev = m_old[:, h:h + 1]
        big_m = jnp.maximum(jnp.max(jnp.where(causal, a_row, NEG), axis=1, keepdims=True), m_prev)
        d_w = jnp.where(causal, jnp.exp(jnp.minimum(a_row - big_m, 0.0)), 0.0)
        inter = jnp.exp(m_prev - big_m)
        qh = q_all[:, h * dk:(h + 1) * dk]
        v_parts = [v_all[:, h * dv + j * LANES:h * dv + (j + 1) * LANES] for j in range(v_blocks)]
        kt = kt_all[h * dk:(h + 1) * dk, :]
        sd = (_dot(qh, kt) * d_w).astype(BF16)
        st = st_scr[h]
        q_int = (inter * qh.astype(F32)).astype(BF16)
        x = _dot(jnp.concatenate([sd, q_int], axis=1),
                 jnp.concatenate([jnp.concatenate(v_parts + [ones_blk], axis=1), st.astype(BF16)], axis=0))
        den = x[:, dv:]
        inv = 1.0 / jnp.maximum(jnp.abs(den), jnp.exp(-(b_rep + big_m)) * (1.0 / scale))
        sq = jnp.sum(x[:, :dv] * x[:, :dv], axis=1, keepdims=True) * (1.0 / dv)
        sc = (inv * lax.rsqrt(inv * inv * sq + EPS))[:lv]
        for j in range(v_blocks):
            cols = slice(h * dv + j * LANES, h * dv + (j + 1) * LANES)
            gate_o = 0.5 * jnp.tanh(0.5 * o_ref[:, cols].astype(F32)) + 0.5
            a_ref[:, cols] = (x[:lv, j * LANES:(j + 1) * LANES] * sc * (mhg_ref[:, cols] * gate_o)).astype(BF16)
        m_last = big_m[L - 1:L, :]
        decay = jnp.exp(m_prev - m_last)
        w_rep = jnp.exp(a_rep - m_last)
        vw = jnp.concatenate([(p.astype(F32) * w_rep).astype(BF16) for p in v_parts] + [w_rep.astype(BF16)], axis=1)
        st_scr[h] = decay * st + _dot(kt, vw)
        m_new = jnp.where(lane1 == h, b_all[L - 1:L, h:h + 1] + m_last, m_new)
    m_scr[...] = m_new

    @pl.when(c == pl.num_programs(1) - 1)
    def _fin():
        for h in range(h_n):
            cout_ref[h] = st_scr[h, :, :dv].T * scale
            nout_ref[h:h + 1, :] = st_scr[h, :, dv:].T[0:1, :] * scale
        mout_ref[...] = m_scr[...]


def _mlstm(dm, proj, gates, bias, mhg, c0, n0, m0, ls, *, sample):
    if sample:
        b_n, n_grid, nc, lv, row0, n_rows = dm.Bs, ROW_TILE // dm.DS, 1, dm.DS, dm.Tp // dm.DS, ROW_TILE
    else:
        b_n, n_grid, nc, lv, row0, n_rows = dm.Bp, dm.Bp, dm.S // MLSTM_CHUNK, MLSTM_CHUNK, 0, dm.Tp
    h_n, dk, dv, qk, v = dm.H, dm.DK, dm.DV, dm.QK, dm.V
    rb = lambda b, c: row0 + b * nc + c
    sq = lambda b: jnp.minimum(b, b_n - 1)
    in_specs = [
        pl.BlockSpec((lv, qk), lambda b, c: (rb(b, c), 0)),
        pl.BlockSpec((lv, qk), lambda b, c: (rb(b, c), 1)),
        pl.BlockSpec((lv, v), lambda b, c: (rb(b, c), dm.off_v // v)),
        pl.BlockSpec((lv, v), lambda b, c: (rb(b, c), dm.off_o // v)),
        pl.BlockSpec((lv, LANES), lambda b, c: (rb(b, c), 0)),
        pl.BlockSpec((1, LANES), lambda b, c: (0, 0)),
        pl.BlockSpec((1, v), lambda b, c: (0, 0)),
        pl.BlockSpec((None, None, h_n, dv, dk), lambda b, c: (ls, sq(b), 0, 0, 0)),
        pl.BlockSpec((None, None, h_n, dk), lambda b, c: (ls, sq(b), 0, 0)),
        pl.BlockSpec((None, None, 1, LANES), lambda b, c: (ls, sq(b), 0, 0)),
    ]
    return pl.pallas_call(
        functools.partial(_mlstm_kernel, dm=dm, lv=lv, n_seq=b_n),
        grid=(n_grid, nc),
        in_specs=in_specs,
        out_specs=[pl.BlockSpec((lv, v), lambda b, c: (b * nc + c, 0)),
                   pl.BlockSpec((None, h_n, dv, dk), lambda b, c: (sq(b), 0, 0, 0)),
                   pl.BlockSpec((None, h_n, dk), lambda b, c: (sq(b), 0, 0)),
                   pl.BlockSpec((None, 1, LANES), lambda b, c: (sq(b), 0, 0))],
        out_shape=[jax.ShapeDtypeStruct((n_rows, v), BF16),
                   jax.ShapeDtypeStruct((b_n, h_n, dv, dk), F32),
                   jax.ShapeDtypeStruct((b_n, h_n, dk), F32),
                   jax.ShapeDtypeStruct((b_n, 1, LANES), F32)],
        scratch_shapes=[pltpu.VMEM((h_n, dk, dv + LANES), F32), pltpu.VMEM((1, LANES), F32)],
        compiler_params=_cparams(("arbitrary", "arbitrary"), 40),
        name="mlstm_sample" if sample else "mlstm_prompt",
    )(proj, proj, proj, proj, gates, bias, mhg, c0, n0, m0)


def _sgu_kernel(*refs, dm, lv, n_seq, emit_v):
    g_ref = refs[5]
    seq_ok = pl.program_id(0) < n_seq

    @pl.when(seq_ok)
    def _run():
        _sgu_step(*refs, dm=dm, lv=lv, emit_v=emit_v)

    @pl.when(jnp.logical_not(seq_ok))
    def _fill():
        g_ref[...] = jnp.zeros_like(g_ref)


def _sgu_step(*refs, dm, lv, emit_v):
    z_ref, lng_ref, lnb_ref, ws_ref, bst_ref, g_ref = refs[:6]
    gh, gch = dm.GH, dm.GH // dm.G
    z = jax.nn.gelu(z_ref[...].astype(F32))
    u, v = z[:, :gh], z[:, gh:]
    xc = v - jnp.mean(v, axis=-1, keepdims=True)
    vn = xc * lax.rsqrt(jnp.mean(xc * xc, axis=-1, keepdims=True) + EPS) * lng_ref[...] + lnb_ref[...]
    if emit_v:
        refs[6][...] = vn
    row = lax.broadcasted_iota(I32, (lv, lv), 0)
    col = lax.broadcasted_iota(I32, (lv, lv), 1)
    vb = vn.astype(BF16)
    for g in range(dm.G):
        w = jnp.where(col <= row, ws_ref[g, :lv, :lv], 0.0).astype(BF16)
        cols = slice(g * gch, (g + 1) * gch)
        for c in range(z.shape[0] // lv):
            rows = slice(c * lv, (c + 1) * lv)
            mixed = _dot(w, vb[rows, cols]) + bst_ref[:lv, g:g + 1]
            g_ref[rows, cols] = (u[rows, cols] * mixed).astype(BF16)


def _sgu(dm, proj, lng, lnb, ws, bst, *, sample):
    if sample:
        b_n, n_grid, nc, lv, br, row0, n_rows = dm.Bs, ROW_TILE // dm.DS, 1, dm.DS, dm.DS, dm.Tp // dm.DS, ROW_TILE
    else:
        br = SGU_CHUNKS_PER_STEP * dm.GC
        b_n, n_grid, nc, lv, row0, n_rows = dm.Bp, dm.Bp, dm.S // br, dm.GC, 0, dm.Tp
    gh = dm.GH
    rb = lambda b, c: row0 + b * nc + c
    in_specs = [
        pl.BlockSpec((br, 2 * gh), lambda b, c: (rb(b, c), dm.off_z // (2 * gh))),
        pl.BlockSpec((1, gh), lambda b, c: (0, 0)),
        pl.BlockSpec((1, gh), lambda b, c: (0, 0)),
        pl.BlockSpec((dm.G, dm.GC, dm.GC), lambda b, c: (0, 0, 0)),
        pl.BlockSpec((dm.GC, LANES), lambda b, c: (0, 0)),
    ]
    out_specs = [pl.BlockSpec((br, gh), lambda b, c: (b * nc + c, 0))]
    out_shape = [jax.ShapeDtypeStruct((n_rows, gh), BF16)]
    if sample:
        out_specs.append(pl.BlockSpec((None, lv, gh), lambda b, c: (jnp.minimum(b, b_n - 1), 0, 0)))
        out_shape.append(jax.ShapeDtypeStruct((b_n, lv, gh), F32))
    return pl.pallas_call(
        functools.partial(_sgu_kernel, dm=dm, lv=lv, n_seq=b_n, emit_v=sample),
        grid=(n_grid, nc),
        in_specs=in_specs,
        out_specs=out_specs,
        out_shape=out_shape,
        compiler_params=_cparams(("arbitrary", "arbitrary"), 24),
        name="sgu_sample" if sample else "sgu_prompt",
    )(proj, lng, lnb, ws, bst)


def _merge_kernel(ap_ref, as_ref, gp_ref, gs_ref, ga_ref, gb_ref, wa_ref, wb_ref, o_ref):
    is_s = pl.program_id(1) == pl.num_programs(1) - 1
    pa = _dot(jnp.where(is_s, as_ref[...], ap_ref[...]), wa_ref[...])
    pb = _dot(jnp.where(is_s, gs_ref[...], gp_ref[...]), wb_ref[...])
    sa = 0.5 * jnp.tanh(0.5 * ga_ref[...].astype(F32)) + 0.5
    sb = 0.5 * jnp.tanh(0.5 * gb_ref[...].astype(F32)) + 0.5
    o_ref[...] = (sa * pa + sb * pb).astype(o_ref.dtype)


def _merge(dm, a_p, a_s, g_p, g_s, proj, wa, wb):
    tc, d, v, gh = dm.Tc, dm.D, dm.V, dm.GH
    tm = ROW_TILE
    tn = _pick_tile(d, 1024)
    np_tiles = dm.Tp // tm
    prompt = lambda j, i: (jnp.minimum(i, np_tiles - 1), 0)
    const = lambda j, i: (0, 0)
    blocks = 2 * tm * (v + gh) * 2 + 3 * tm * tn * 2 + (v + gh) * tn * 2
    return pl.pallas_call(
        _merge_kernel,
        grid=(d // tn, tc // tm),
        in_specs=[pl.BlockSpec((tm, v), prompt),
                  pl.BlockSpec((tm, v), const),
                  pl.BlockSpec((tm, gh), prompt),
                  pl.BlockSpec((tm, gh), const),
                  pl.BlockSpec((tm, tn), lambda j, i: (i, dm.off_ga // tn + j)),
                  pl.BlockSpec((tm, tn), lambda j, i: (i, dm.off_gb // tn + j)),
                  pl.BlockSpec((v, tn), lambda j, i: (0, j)),
                  pl.BlockSpec((gh, tn), lambda j, i: (0, j))],
        out_specs=pl.BlockSpec((tm, tn), lambda j, i: (i, j)),
        out_shape=jax.ShapeDtypeStruct((tc, d), BF16),
        compiler_params=_cparams(("arbitrary", "arbitrary"), (2 * blocks + 6 * tm * tn * 4) / MIB + 8),
        name="merge",
    )(a_p, a_s, g_p, g_s, proj, proj, wa, wb)


def _route(logits, eg, ne):
    epg = ne // eg
    lane = lax.broadcasted_iota(I32, logits.shape, 1)
    gmask = lane < eg
    gmax = jnp.max(jnp.where(gmask, logits, NEG), axis=1, keepdims=True)
    gexp = jnp.where(gmask, jnp.exp(jnp.minimum(logits - gmax, 0.0)), 0.0)
    pg = gexp / jnp.sum(gexp, axis=1, keepdims=True)
    p_grp = jnp.max(pg, axis=1, keepdims=True)
    g_sel = jnp.min(jnp.where(gmask & (pg == p_grp), lane, LANES), axis=1, keepdims=True)
    lo = eg + g_sel * epg
    emask = (lane >= lo) & (lane < lo + epg)
    emax = jnp.max(jnp.where(emask, logits, NEG), axis=1, keepdims=True)
    eexp = jnp.where(emask, jnp.exp(jnp.minimum(logits - emax, 0.0)), 0.0)
    pe = eexp / jnp.sum(eexp, axis=1, keepdims=True)
    p1 = jnp.max(jnp.where(emask, pe, -1.0), axis=1, keepdims=True)
    i1 = jnp.min(jnp.where(emask & (pe == p1), lane, LANES), axis=1, keepdims=True)
    rest = emask & (lane != i1)
    p2 = jnp.max(jnp.where(rest, pe, -1.0), axis=1, keepdims=True)
    i2 = jnp.min(jnp.where(rest & (pe == p2), lane, LANES), axis=1, keepdims=True)
    psum = p1 + p2
    eid = jnp.where(lane == 0, i1 - eg, jnp.where(lane == 1, i2 - eg, 0))
    gate = jnp.where(lane == 0, p_grp * (p1 / psum), jnp.where(lane == 1, p_grp * (p2 / psum), 0.0))
    return eid, gate


def _outproj_kernel(*refs, dm, first):
    if first:
        m_ref, xp_ref, xs_ref = refs[:3]
        refs = refs[3:]
    else:
        m_ref, x_ref = refs[:2]
        refs = refs[2:]
    w_ref, g_ref, mod_ref, wr_ref, rb_ref, x1_ref, hp_ref, eid_ref, gate_ref, acc_scr, hi_scr, lo_scr = refs
    i = pl.program_id(0)
    t = jnp.maximum(i - 1, 0)
    is_s = t == dm.n_tiles - 1
    bp = jnp.minimum(t // (dm.S // ROW_TILE), dm.Bp - 1)
    cur, prev = i % 2, (i + 1) % 2
    g = g_ref[...]

    @pl.when(i == 0)
    def _no_previous_tile():
        acc_scr[1] = jnp.zeros(acc_scr.shape[1:], F32)

    def chunk(j):
        rows = pl.ds(j * ROW_CHUNK, ROW_CHUNK)
        idx = jnp.where(is_s, dm.Bp + j, bp)
        x = jnp.where(is_s, xs_ref[rows, :], xp_ref[rows, :]) if first else x_ref[rows, :]
        x1 = x + mod_ref[GT1, pl.ds(idx, 1), :] * acc_scr[prev, rows, :]
        x1_ref[rows, :] = x1
        h = _rms_mod(x1, g, mod_ref[SC2, pl.ds(idx, 1), :], mod_ref[SH2, pl.ds(idx, 1), :])
        hi, lo = _split_bf16(h)
        hi_scr[rows, :] = hi
        lo_scr[rows, :] = lo
        hp_ref[rows, :] = _pack_pair(h)

    n_chunks = ROW_TILE // ROW_CHUNK
    n_slices = min(dm.D // 256, n_chunks)
    for s in range(n_slices):
        cols = slice(s * dm.D // n_slices, (s + 1) * dm.D // n_slices)
        acc_scr[cur, :, cols] = _dot(m_ref[...], w_ref[:, cols])
        for j in range(s * n_chunks // n_slices, (s + 1) * n_chunks // n_slices):
            chunk(j)
    w_hi, w_lo = _split_bf16(wr_ref[...])
    logits = _dot3(hi_scr[...], lo_scr[...], w_hi, w_lo) + rb_ref[...]
    eid, gate = _route(logits, dm.EG, dm.NE)
    eid_ref[...] = eid
    gate_ref[...] = gate


def _outproj(dm, l, merged, x, w_out, g2, mod, w_router, b_router):
    d, tc = dm.D, dm.Tc
    tile = lambda i: (jnp.maximum(i - 1, 0), 0)
    const = lambda i: (0, 0)
    first = isinstance(x, tuple)
    if first:
        np_tiles = dm.Tp // ROW_TILE
        x_args = list(x)
        x_specs = [pl.BlockSpec((ROW_TILE, d), lambda i: (jnp.clip(i - 1, 0, np_tiles - 1), 0)),
                   pl.BlockSpec((ROW_TILE, d), const)]
    else:
        x_args = [x]
        x_specs = [pl.BlockSpec((ROW_TILE, d), tile)]
    return pl.pallas_call(
        functools.partial(_outproj_kernel, dm=dm, first=first),
        grid=(dm.n_tiles + 1,),
        in_specs=[pl.BlockSpec((ROW_TILE, d), lambda i: (jnp.minimum(i, dm.n_tiles - 1), 0))] + x_specs + [
                  pl.BlockSpec((d, d), const),
                  pl.BlockSpec((None, 1, d), lambda i: (l, 0, 0)),
                  pl.BlockSpec((None, 6, dm.n_mod, d), lambda i: (l, 0, 0, 0)),
                  pl.BlockSpec((d, LANES), const),
                  pl.BlockSpec((1, LANES), const)],
        out_specs=[pl.BlockSpec((ROW_TILE, d), tile),
                   pl.BlockSpec((ROW_TILE, d // 2), tile),
                   pl.BlockSpec((ROW_TILE, LANES), tile),
                   pl.BlockSpec((ROW_TILE, LANES), tile)],
        out_shape=[jax.ShapeDtypeStruct((tc, d), F32),
                   jax.ShapeDtypeStruct((tc, d // 2), U32),
                   jax.ShapeDtypeStruct((tc, LANES), I32),
                   jax.ShapeDtypeStruct((tc, LANES), F32)],
        scratch_shapes=[pltpu.VMEM((2, ROW_TILE, d), F32), pltpu.VMEM((ROW_TILE, d), BF16),
                        pltpu.VMEM((ROW_TILE, d), BF16)],
        compiler_params=_cparams(("arbitrary",), (4 * d * d + ROW_TILE * d * 48) / MIB + 12),
        name="out_proj",
    )(merged, *x_args, w_out, g2, mod, w_router, b_router)


def _rank_kernel(eid_ref, rank_ref, cnt_ref, run_scr, start_scr):
    sweep, i = pl.program_id(0), pl.program_id(1)

    @pl.when(i == 0)
    def _init():
        @pl.when(sweep == 1)
        def _starts():
            padded = jnp.ceil(run_scr[...] * (1.0 / MOE_BLOCK)) * MOE_BLOCK
            earlier = (lax.broadcasted_iota(I32, (LANES, LANES), 0)
                       < lax.broadcasted_iota(I32, (LANES, LANES), 1)).astype(BF16)
            p_hi, p_lo = _split_bf16(jnp.broadcast_to(padded, (8, LANES)))
            start_scr[...] = (_dot(p_hi, earlier) + _dot(p_lo, earlier))[0:1, :]
            cnt_ref[...] = run_scr[...].astype(I32)

        @pl.when(sweep == 0)
        def _no_starts_yet():
            start_scr[...] = jnp.zeros_like(start_scr)

        run_scr[...] = jnp.zeros_like(run_scr)

    eid = eid_ref[...]
    lane = lax.broadcasted_iota(I32, eid.shape, 1)
    e0 = lane == eid[:, 0:1]
    e1 = lane == eid[:, 1:2]
    hot = (e0 | e1).astype(BF16)
    n = eid.shape[0]
    strict = (lax.broadcasted_iota(I32, (n, n), 1) < lax.broadcasted_iota(I32, (n, n), 0)).astype(BF16)
    before = _dot(strict, hot) + run_scr[...]
    slot = before + start_scr[...]
    s0 = jnp.sum(jnp.where(e0, slot, 0.0), axis=1, keepdims=True)
    s1 = jnp.sum(jnp.where(e1, slot, 0.0), axis=1, keepdims=True)
    rank_ref[...] = jnp.where(lane == 0, s0, jnp.where(lane == 1, s1, 0.0)).astype(I32)
    run_scr[...] = run_scr[...] + jnp.sum(hot.astype(F32), axis=0, keepdims=True)


def _rank(dm, eid):
    return pl.pallas_call(
        _rank_kernel,
        grid=(2, dm.n_tiles),
        in_specs=[pl.BlockSpec((ROW_TILE, LANES), lambda s, i: (i, 0))],
        out_specs=[pl.BlockSpec((ROW_TILE, LANES), lambda s, i: (s * i, 0)),
                   pl.BlockSpec((1, LANES), lambda s, i: (0, 0))],
        out_shape=[jax.ShapeDtypeStruct((dm.Tc, LANES), I32), jax.ShapeDtypeStruct((1, LANES), I32)],
        scratch_shapes=[pltpu.VMEM((1, LANES), F32), pltpu.VMEM((1, LANES), F32)],
        compiler_params=_cparams(("arbitrary", "arbitrary"), 16),
        name="moe_rank",
    )(eid)


def _dispatch_kernel(tail_ref, dest_ref, h_ref, xs_ref, zero_scr, sem):
    @pl.when(pl.program_id(0) == 0)
    def _clear_tails():
        zero_scr[...] = jnp.zeros_like(zero_scr)
        n_e = tail_ref.shape[0]

        def tail_copy(e):
            start = pl.multiple_of(tail_ref[e], MOE_BLOCK)
            return pltpu.make_async_copy(zero_scr, xs_ref.at[pl.ds(start, MOE_BLOCK)], sem)

        for e in range(n_e):
            tail_copy(e).start()
        for e in range(n_e):
            tail_copy(e).wait()

    for r in range(ROW_TILE):
        for k in range(TOP_K):
            pltpu.make_async_copy(h_ref.at[pl.ds(r, 1)], xs_ref.at[pl.ds(dest_ref[TOP_K * r + k], 1)],
                                  sem).start(priority=k)
    for k in range(TOP_K):
        pltpu.make_async_copy(h_ref, xs_ref.at[pl.ds(0, ROW_TILE)], sem).wait()


def _dispatch(dm, tails, dest, hp):
    d2 = dm.D // 2
    return pl.pallas_call(
        _dispatch_kernel,
        grid_spec=pltpu.PrefetchScalarGridSpec(
            num_scalar_prefetch=1,
            grid=(dm.n_tiles,),
            in_specs=[pl.BlockSpec((ROW_TILE * TOP_K,), lambda i, t: (i,), memory_space=pltpu.SMEM),
                      pl.BlockSpec((ROW_TILE, d2), lambda i, t: (i, 0))],
            out_specs=pl.BlockSpec(memory_space=pl.ANY),
            scratch_shapes=[pltpu.VMEM((MOE_BLOCK, d2), U32), pltpu.SemaphoreType.DMA]),
        out_shape=jax.ShapeDtypeStruct((dm.cap + dm.NE * MOE_BLOCK, d2), U32),
        compiler_params=_cparams(("arbitrary",), 16),
        name="moe_dispatch",
    )(tails, dest, hp)


def _expert_kernel(be_ref, nu_ref, nx_ref, sl_ref, xs_ref, wg_hbm, wu_hbm, wd_hbm, ys_ref,
                   wg_f, wu_f, wd_f, wg_s, wu_s, wd_s, sem, *, layer):
    j = pl.program_id(0)
    e, slot = be_ref[j], sl_ref[j]
    used = j < nu_ref[0]
    first = used & ((j == 0) | (e != be_ref[jnp.maximum(j - 1, 0)]))

    def weight_copies(expert, s):
        return [pltpu.make_async_copy(src.at[layer, expert], dst.at[s], sem.at[s])
                for src, dst in ((wg_hbm, wg_f), (wu_hbm, wu_f), (wd_hbm, wd_f))]

    @pl.when(j == 0)
    def _first_expert():
        for cp in weight_copies(e, slot):
            cp.start()

    @pl.when(first)
    def _switch_expert():
        for cp in weight_copies(e, slot):
            cp.wait()
        wg_s[...] = wg_f[slot].astype(BF16)
        wu_s[...] = wu_f[slot].astype(BF16)
        wd_s[...] = wd_f[slot].astype(BF16)

        @pl.when(nx_ref[j] >= 0)
        def _prefetch_next():
            for cp in weight_copies(nx_ref[j], 1 - slot):
                cp.start()

    @pl.when(used)
    def _compute():
        d2 = xs_ref.shape[1]
        lo, hi = _unpack_pair(xs_ref[...])
        lo, hi = lo.astype(BF16), hi.astype(BF16)
        hg = _dot(lo, wg_s[:d2, :]) + _dot(hi, wg_s[d2:, :])
        hu = _dot(lo, wu_s[:d2, :]) + _dot(hi, wu_s[d2:, :])
        act = (hg * jax.nn.sigmoid(hg) * hu).astype(BF16)
        ys_ref[...] = _pack_pair(_dot(act, wd_s[...]))


def _experts(dm, l, blk_e, n_used, blk_next, blk_slot, xs, w_gate, w_up, w_down):
    d, de, d2 = dm.D, dm.DE, dm.D // 2
    nb = dm.cap // MOE_BLOCK
    rows = lambda j, be, nu, nx, sl: (jnp.minimum(j, nu[0] - 1), 0)
    hbm = pl.BlockSpec(memory_space=pl.ANY)
    return pl.pallas_call(
        functools.partial(_expert_kernel, layer=l),
        grid_spec=pltpu.PrefetchScalarGridSpec(
            num_scalar_prefetch=4,
            grid=(nb,),
            in_specs=[pl.BlockSpec((MOE_BLOCK, d2), rows), hbm, hbm, hbm],
            out_specs=pl.BlockSpec((MOE_BLOCK, d2), rows),
            scratch_shapes=[pltpu.VMEM((2, d, de), F32), pltpu.VMEM((2, d, de), F32), pltpu.VMEM((2, de, d), F32),
                            pltpu.VMEM((d, de), BF16), pltpu.VMEM((d, de), BF16), pltpu.VMEM((de, d), BF16),
                            pltpu.SemaphoreType.DMA((2,))]),
        out_shape=jax.ShapeDtypeStruct((dm.cap, d2), U32),
        compiler_params=_cparams(("arbitrary",), (3 * d * de * (2 * 4 + 2) + MOE_BLOCK * d * 28) / MIB + 8),
        name="moe_experts",
    )(blk_e, n_used, blk_next, blk_slot, xs, w_gate, w_up, w_down)


def _combine_kernel(*refs, dm, last):
    if last:
        dest_ref, destn_ref, ys_ref, x_ref, gate_ref, mod_ref, g_ref, yp_ref, ysm_ref, ybuf, y_scr, sem = refs
    else:
        (dest_ref, destn_ref, ys_ref, x_ref, gate_ref, mod_ref, g_ref, modn_ref, w_ref,
         o_ref, h_ref, p_ref, ybuf, lo_scr, sem) = refs
    i = pl.program_id(0)
    is_s = i == dm.n_tiles - 1
    bp = jnp.minimum(i // (dm.S // ROW_TILE), dm.Bp - 1)
    d2 = dm.D // 2
    g = g_ref[...]
    slot = i % 2

    def issue(idx_ref, to_slot, r0, n):
        for r in range(n):
            for k in range(TOP_K):
                pltpu.make_async_copy(ys_ref.at[pl.ds(idx_ref[TOP_K * (r0 + r) + k], 1)],
                                      ybuf.at[to_slot, k, pl.ds(r0 + r, 1)], sem.at[to_slot]).start(priority=k)

    def wait_slot(s):
        for k in range(TOP_K):
            pltpu.make_async_copy(ys_ref.at[pl.ds(0, ROW_TILE)], ybuf.at[s, k], sem.at[s]).wait()

    @pl.when(i == 0)
    def _first_tile():
        issue(dest_ref, 0, 0, ROW_TILE)

    wait_slot(slot)

    def chunk(j):
        rows = pl.ds(j * ROW_CHUNK, ROW_CHUNK)
        issue(destn_ref, 1 - slot, j * ROW_CHUNK, ROW_CHUNK)
        idx = jnp.where(is_s, dm.Bp + j, bp)
        gt = mod_ref[GT2, pl.ds(idx, 1), :]
        gates = gate_ref[rows, :]
        g0, g1 = gates[:, 0:1], gates[:, 1:2]
        lo0, hi0 = _unpack_pair(ybuf[slot, 0, rows, :])
        lo1, hi1 = _unpack_pair(ybuf[slot, 1, rows, :])
        x_lo = x_ref[rows, :d2] + gt[:, :d2] * (g0 * lo0 + g1 * lo1)
        x_hi = x_ref[rows, d2:] + gt[:, d2:] * (g0 * hi0 + g1 * hi1)
        ms = (jnp.sum(x_lo * x_lo, axis=-1, keepdims=True)
              + jnp.sum(x_hi * x_hi, axis=-1, keepdims=True)) * (1.0 / dm.D)
        rs = lax.rsqrt(ms + EPS)
        if last:
            y_scr[rows, :d2] = x_lo * rs * g[:, :d2]
            y_scr[rows, d2:] = x_hi * rs * g[:, d2:]
        else:
            o_ref[rows, :d2] = x_lo
            o_ref[rows, d2:] = x_hi
            sc = modn_ref[SC1, pl.ds(idx, 1), :]
            sh = modn_ref[SH1, pl.ds(idx, 1), :]
            for half, xh in ((slice(0, d2), x_lo), (slice(d2, dm.D), x_hi)):
                h = xh * rs * g[:, half] * (1.0 + sc[:, half]) + sh[:, half]
                hi, lo = _split_bf16(h)
                h_ref[rows, half] = hi
                lo_scr[rows, half] = lo

    for j in range(ROW_TILE // ROW_CHUNK):
        chunk(j)

    @pl.when(is_s)
    def _drain():
        wait_slot(1 - slot)

    if last:
        @pl.when(is_s)
        def _sample_rows():
            ysm_ref[...] = y_scr[...]

        @pl.when(jnp.logical_not(is_s))
        def _prompt_rows():
            yp_ref[...] = y_scr[...]
    else:
        w_hi, w_lo = _gate_weight(w_ref)
        p_ref[...] = _dot3(h_ref[...], lo_scr[...], w_hi, w_lo)


def _combine(dm, l, dest, ys, x1, gate, mod, g_next, w_in=None):
    d = dm.D
    last = w_in is None
    tile = lambda i: (i, 0)
    np_tiles = dm.Tp // ROW_TILE
    in_specs = [pl.BlockSpec((ROW_TILE * TOP_K,), lambda i: (i,), memory_space=pltpu.SMEM),
                pl.BlockSpec((ROW_TILE * TOP_K,), lambda i: (jnp.minimum(i + 1, dm.n_tiles - 1),),
                             memory_space=pltpu.SMEM),
                pl.BlockSpec(memory_space=pl.ANY),
                pl.BlockSpec((ROW_TILE, d), tile),
                pl.BlockSpec((ROW_TILE, LANES), tile),
                pl.BlockSpec((None, 6, dm.n_mod, d), lambda i: (l, 0, 0, 0))]
    scratch = [pltpu.VMEM((2, TOP_K, ROW_TILE, d // 2), U32)]
    if last:
        args = (dest, dest, ys, x1, gate, mod, g_next)
        in_specs.append(pl.BlockSpec((1, d), lambda i: (0, 0)))
        out_specs = [pl.BlockSpec((ROW_TILE, d), lambda i: (jnp.minimum(i, np_tiles - 1), 0)),
                     pl.BlockSpec((ROW_TILE, d), lambda i: (0, 0))]
        out_shape = [jax.ShapeDtypeStruct((dm.Tp, d), F32), jax.ShapeDtypeStruct((ROW_TILE, d), F32)]
        scratch.append(pltpu.VMEM((ROW_TILE, d), F32))
    else:
        args = (dest, dest, ys, x1, gate, mod, g_next, mod, w_in)
        in_specs += [pl.BlockSpec((None, 1, d), lambda i: (l + 1, 0, 0)),
                     pl.BlockSpec((None, 6, dm.n_mod, d), lambda i: (l + 1, 0, 0, 0)),
                     pl.BlockSpec((None, 2 * dm.H, d), lambda i: (l + 1, dm.off_z // (2 * dm.H), 0))]
        out_specs = [pl.BlockSpec((ROW_TILE, d), tile), pl.BlockSpec((ROW_TILE, d), tile),
                     pl.BlockSpec((ROW_TILE, LANES), tile)]
        out_shape = [jax.ShapeDtypeStruct((dm.Tc, d), F32), jax.ShapeDtypeStruct((dm.Tc, d), BF16),
                     jax.ShapeDtypeStruct((dm.Tc, LANES), F32)]
        scratch.append(pltpu.VMEM((ROW_TILE, d), BF16))
    scratch.append(pltpu.SemaphoreType.DMA((2,)))
    return pl.pallas_call(
        functools.partial(_combine_kernel, dm=dm, last=last),
        grid=(dm.n_tiles,),
        in_specs=in_specs,
        out_specs=out_specs,
        out_shape=out_shape,
        scratch_shapes=scratch,
        compiler_params=_cparams(("arbitrary",), ROW_TILE * d * 40 / MIB + 16),
        name="moe_combine_last" if last else "moe_combine",
    )(*args)


def _pad_lanes(x, n=LANES):
    return jnp.pad(x, [(0, 0)] * (x.ndim - 1) + [(0, n - x.shape[-1])])


def _moe_plan(dm, slots, counts):
    counts = counts[0, :dm.NE]
    padded = (counts + MOE_BLOCK - 1) // MOE_BLOCK * MOE_BLOCK
    pad_end = jnp.cumsum(padded)
    dest = slots[:, :TOP_K].reshape(-1)
    nb = dm.cap // MOE_BLOCK
    n_used = (pad_end[-1] // MOE_BLOCK).astype(I32)
    starts = jnp.minimum(jnp.arange(nb, dtype=I32), n_used - 1) * MOE_BLOCK
    blk_e = jnp.minimum(jnp.sum(pad_end[None, :] <= starts[:, None], axis=1), dm.NE - 1).astype(I32)
    ids = jnp.arange(dm.NE, dtype=I32)
    spare = dm.cap + ids * MOE_BLOCK
    tails = jnp.where(padded > 0, pad_end - MOE_BLOCK, spare).astype(I32)
    live = jnp.where(counts > 0, ids, dm.NE)
    nxt = lax.cummin(jnp.concatenate([live[1:], jnp.full((1,), dm.NE, I32)]), reverse=True)
    blk_next = jnp.where(nxt[blk_e] < dm.NE, nxt[blk_e], -1).astype(I32)
    blk_slot = ((jnp.cumsum(counts > 0) - 1)[blk_e] % 2).astype(I32)
    return dest, blk_e, n_used.reshape(1), tails, blk_next, blk_slot


def kernel(x_prompt, x_sample, c_prompt, c_sample, state_mlstm_C, state_mlstm_n, state_mlstm_m, w_ada, b_ada, norm1_g, norm2_g, w_in, b_igate, b_fgate, mh_norm_g, sgu_ln_g, sgu_ln_b, w_spatial, b_spatial, w_branch_a, w_branch_b, w_out, w_router_group, b_router_group, w_router_expert, b_router_expert, w_expert_gate, w_expert_up, w_expert_down, final_norm_g):
    bp, s, d = x_prompt.shape
    bs, ds_, _ = x_sample.shape
    depth = w_in.shape[0]
    h_n = b_igate.shape[1]
    dv, dk = state_mlstm_C.shape[-2:]
    qk, v = h_n * dk, h_n * dv
    gh = (w_in.shape[2] - (2 * qk + 2 * v + 2 * h_n + 2 * d)) // 2
    dm = Dims(Bp=bp, S=s, Bs=bs, DS=ds_, D=d, DEPTH=depth, H=h_n, DK=dk, DV=dv, GH=gh,
              G=w_spatial.shape[1], GC=w_spatial.shape[2], EG=w_router_group.shape[-1],
              NE=w_router_expert.shape[-1], DE=w_expert_gate.shape[-1])
    assert dm.DS == ROW_CHUNK and dm.Ts <= ROW_TILE and s % ROW_TILE == 0 and s % MLSTM_CHUNK == 0
    assert dm.EG + dm.NE <= LANES and 2 * h_n <= LANES and (2 * h_n) % 8 == 0 and dm.Bp + dm.Bs <= 32
    w_in = jnp.swapaxes(w_in, 1, 2)
    assert dm.off_v % v == 0 and dm.off_o % v == 0 and dm.off_z % (2 * gh) == 0 and d % 256 == 0

    c_all = jnp.concatenate([c_prompt, c_sample, jnp.zeros((32 - bp - bs, d), F32)], axis=0)
    mod = _adaln(c_all, w_ada, b_ada)[:, :bp + bs].reshape(depth, bp + bs, 6, d)
    mod = jnp.pad(mod, ((0, 0), (0, dm.n_mod - bp - bs), (0, 0), (0, 0))).transpose(0, 2, 1, 3)

    xs_rows = jnp.pad(x_sample.reshape(dm.Ts, d), ((0, ROW_TILE - dm.Ts), (0, 0)))
    x = (x_prompt.reshape(dm.Tp, d), xs_rows)
    zeros_c = jnp.zeros((1, bp, h_n, dv, dk), F32)
    zeros_n = jnp.zeros((1, bp, h_n, dk), F32)
    zeros_m = jnp.zeros((1, bp, 1, LANES), F32)
    m_in = _pad_lanes(state_mlstm_m)[:, :, None, :]
    g1 = norm1_g.reshape(depth, 1, d)
    out_c_p, out_n_p, out_m_p, out_c_s, out_n_s, out_m_s, out_v = [], [], [], [], [], [], []

    for l in range(depth):
        if l == 0:
            hb, gates = _norm_proj(dm, l, x, g1, mod, w_in, SC1, SH1)
        proj = _in_proj(dm, l, hb, w_in)

        bias = _pad_lanes(jnp.concatenate([b_igate[l], b_fgate[l]])[None, :])
        mhg = mh_norm_g[l][None, :]
        a_p, c_p, n_p, m_p = _mlstm(dm, proj, gates, bias, mhg, zeros_c, zeros_n, zeros_m, 0, sample=False)
        a_s, c_s, n_s, m_s = _mlstm(dm, proj, gates, bias, mhg, state_mlstm_C, state_mlstm_n, m_in, l,
                                    sample=True)
        lng, lnb = sgu_ln_g[l][None, :], sgu_ln_b[l][None, :]
        bst = _pad_lanes(b_spatial[l].T)
        (g_p,) = _sgu(dm, proj, lng, lnb, w_spatial[l], bst, sample=False)
        g_s, v_rows = _sgu(dm, proj, lng, lnb, w_spatial[l], bst, sample=True)

        merged = _merge(dm, a_p, a_s, g_p, g_s, proj, w_branch_a[l].astype(BF16), w_branch_b[l].astype(BF16))
        w_router = _pad_lanes(jnp.concatenate([w_router_group[l], w_router_expert[l]], axis=1))
        b_router = _pad_lanes(jnp.concatenate([b_router_group[l], b_router_expert[l]])[None, :])
        x1, hp, eid, gate = _outproj(dm, l, merged, x, w_out[l].astype(BF16), norm2_g.reshape(depth, 1, d),
                                     mod, w_router, b_router)

        slots, counts = _rank(dm, eid)
        dest, blk_e, n_used, tails, blk_next, blk_slot = _moe_plan(dm, slots, counts)
        xs = _dispatch(dm, tails, dest, hp)
        ys = _experts(dm, l, blk_e, n_used, blk_next, blk_slot, xs,
                      w_expert_gate, w_expert_up, w_expert_down)
        if l + 1 < depth:
            x, hb, gates = _combine(dm, l, dest, ys, x1, gate, mod, g1, w_in)
        else:
            y_p, y_s = _combine(dm, l, dest, ys, x1, gate, mod, final_norm_g[None, :])

        out_c_p.append(c_p)
        out_n_p.append(n_p)
        out_m_p.append(m_p[:, 0, :h_n])
        out_c_s.append(c_s)
        out_n_s.append(n_s)
        out_m_s.append(m_s[:, 0, :h_n])
        out_v.append(v_rows)

    return (y_p.reshape(bp, s, d), y_s[:dm.Ts].reshape(bs, ds_, d),
            jnp.stack(out_c_p), jnp.stack(out_n_p), jnp.stack(out_m_p),
            jnp.stack(out_c_s), jnp.stack(out_n_s), jnp.stack(out_m_s), jnp.stack(out_v))
```

```python
import functools
from typing import NamedTuple

import jax
import jax.numpy as jnp
from jax import lax
from jax.experimental import pallas as pl
from jax.experimental.pallas import tpu as pltpu

F32, BF16, U32, I32 = jnp.float32, jnp.bfloat16, jnp.uint32, jnp.int32
EPS = 1e-6
NEG = -1e30
LANES = 128
ROW_TILE = 512
ROW_CHUNK = 16
MLSTM_CHUNK = 256
MOE_BLOCK = 512
TOP_K = 2
CHUNK_UNROLL = 2
SGU_CHUNKS_PER_STEP = 4
MIB = 1024 * 1024
SH1, SC1, GT1, SH2, SC2, GT2 = range(6)


class Dims(NamedTuple):
    Bp: int
    S: int
    Bs: int
    DS: int
    D: int
    DEPTH: int
    H: int
    DK: int
    DV: int
    GH: int
    G: int
    GC: int
    EG: int
    NE: int
    DE: int

    @property
    def QK(self):
        return self.H * self.DK

    @property
    def V(self):
        return self.H * self.DV

    @property
    def Tp(self):
        return self.Bp * self.S

    @property
    def Ts(self):
        return self.Bs * self.DS

    @property
    def Tc(self):
        return self.Tp + ROW_TILE

    @property
    def n_tiles(self):
        return self.Tc // ROW_TILE

    @property
    def n_mod(self):
        return self.Bp + ROW_TILE // ROW_CHUNK

    @property
    def off_v(self):
        return 2 * self.QK

    @property
    def off_o(self):
        return 2 * self.QK + self.V

    @property
    def off_z(self):
        return 2 * self.QK + 2 * self.V

    @property
    def off_ga(self):
        return self.off_z + 2 * self.GH

    @property
    def off_gb(self):
        return self.off_ga + self.D

    @property
    def n_main(self):
        return self.off_gb + self.D

    @property
    def cap(self):
        tk = self.Tc * TOP_K
        return -(-(tk + self.NE * (MOE_BLOCK - 1)) // MOE_BLOCK) * MOE_BLOCK


def _cparams(semantics, vmem_mib):
    return pltpu.CompilerParams(dimension_semantics=semantics, vmem_limit_bytes=int(vmem_mib * MIB))


def _pick_tile(n, cap, unit=256):
    best = None
    for t in range(unit, min(n, cap) + 1, unit):
        if n % t == 0:
            best = t
    assert best is not None, (n, cap)
    return best


def _dot(a, b):
    return jnp.dot(a, b, preferred_element_type=F32)


def _split_bf16(x):
    hi = x.astype(BF16)
    lo = (x - hi.astype(F32)).astype(BF16)
    return hi, lo


def _dot3(a_hi, a_lo, w_hi, w_lo):
    return _dot(a_hi, w_hi) + _dot(a_hi, w_lo) + _dot(a_lo, w_hi)


def _gate_weight(w_ref):
    rows, d = w_ref.shape
    w = jnp.concatenate([w_ref[...], jnp.zeros((LANES - rows, d), F32)], axis=0).T
    return _split_bf16(w)


def _rms_mod(x, g, sc, sh):
    y = x * lax.rsqrt(jnp.mean(x * x, axis=-1, keepdims=True) + EPS) * g
    return y * (1.0 + sc) + sh


def _pack_pair(x):
    d2 = x.shape[-1] // 2
    bits = lax.bitcast_convert_type(x.astype(BF16).astype(F32), U32)
    return (bits[:, :d2] >> 16) | (bits[:, d2:] & jnp.uint32(0xFFFF0000))


def _unpack_pair(p):
    lo = lax.bitcast_convert_type(p << 16, F32)
    hi = lax.bitcast_convert_type(p & jnp.uint32(0xFFFF0000), F32)
    return lo, hi


def _adaln_kernel(c_ref, w_ref, b_ref, o_ref):
    c = c_ref[...]
    s = (c * jax.nn.sigmoid(c)).astype(BF16)
    o_ref[...] = _dot(s, w_ref[...].astype(BF16)) + b_ref[...]


def _adaln(c_all, w_ada, b_ada):
    depth, d, n6 = w_ada.shape
    r = c_all.shape[0]
    tn = _pick_tile(n6, 1024)
    return pl.pallas_call(
        _adaln_kernel,
        grid=(depth, n6 // tn),
        in_specs=[pl.BlockSpec((r, d), lambda l, j: (0, 0)),
                  pl.BlockSpec((None, d, tn), lambda l, j: (l, 0, j)),
                  pl.BlockSpec((None, 1, tn), lambda l, j: (l, 0, j))],
        out_specs=pl.BlockSpec((None, r, tn), lambda l, j: (l, 0, j)),
        out_shape=jax.ShapeDtypeStruct((depth, r, n6), F32),
        compiler_params=_cparams(("arbitrary", "arbitrary"), 2 * d * tn * 4 / MIB + 3 * d * tn * 2 / MIB + 8),
        name="adaln",
    )(c_all, w_ada, b_ada.reshape(depth, 1, n6))


def _norm_proj_kernel(*refs, first, dm, k_sc, k_sh):
    if first:
        xp_ref, xs_ref, g_ref, mod_ref, w_ref, h_ref, p_ref, lo_scr = refs
    else:
        x_ref, g_ref, mod_ref, w_ref, h_ref, p_ref, lo_scr = refs
    i = pl.program_id(0)
    is_s = i == dm.n_tiles - 1
    bp = jnp.minimum(i // (dm.S // ROW_TILE), dm.Bp - 1)
    g = g_ref[...]

    def chunk(j, carry):
        rows = pl.ds(pl.multiple_of(j * ROW_CHUNK, ROW_CHUNK), ROW_CHUNK)
        if first:
            x = jnp.where(is_s, xs_ref[rows, :], xp_ref[rows, :])
        else:
            x = x_ref[rows, :]
        idx = jnp.where(is_s, dm.Bp + j, bp)
        h = _rms_mod(x, g, mod_ref[k_sc, pl.ds(idx, 1), :], mod_ref[k_sh, pl.ds(idx, 1), :])
        hi, lo = _split_bf16(h)
        h_ref[rows, :] = hi
        lo_scr[rows, :] = lo
        return carry

    lax.fori_loop(0, ROW_TILE // ROW_CHUNK, chunk, 0, unroll=CHUNK_UNROLL)
    w_hi, w_lo = _gate_weight(w_ref)
    p_ref[...] = _dot3(h_ref[...], lo_scr[...], w_hi, w_lo)


def _norm_proj(dm, l, x_in, g, mod, w_in, k_sc, k_sh):
    d, tc, nt = dm.D, dm.Tc, dm.n_tiles
    first = isinstance(x_in, tuple)
    np_tiles = dm.Tp // ROW_TILE
    tile = lambda i: (i, 0)
    if first:
        x_args = list(x_in)
        x_specs = [pl.BlockSpec((ROW_TILE, d), lambda i: (jnp.minimum(i, np_tiles - 1), 0)),
                   pl.BlockSpec((ROW_TILE, d), lambda i: (0, 0))]
    else:
        x_args = [x_in]
        x_specs = [pl.BlockSpec((ROW_TILE, d), tile)]
    out_shape = [jax.ShapeDtypeStruct((tc, d), BF16), jax.ShapeDtypeStruct((tc, LANES), F32)]
    out_specs = [pl.BlockSpec((ROW_TILE, d), tile), pl.BlockSpec((ROW_TILE, LANES), tile)]
    return pl.pallas_call(
        functools.partial(_norm_proj_kernel, first=first, dm=dm, k_sc=k_sc, k_sh=k_sh),
        grid=(nt,),
        in_specs=x_specs + [pl.BlockSpec((None, 1, d), lambda i: (l, 0, 0)),
                            pl.BlockSpec((None, 6, dm.n_mod, d), lambda i: (l, 0, 0, 0)),
                            pl.BlockSpec((None, 2 * dm.H, d), lambda i: (l, dm.off_z // (2 * dm.H), 0))],
        out_specs=out_specs,
        out_shape=out_shape,
        scratch_shapes=[pltpu.VMEM((ROW_TILE, d), BF16)],
        compiler_params=_cparams(("arbitrary",), ROW_TILE * d * 40 / MIB + 16),
        name="norm_proj",
    )(*x_args, g, mod, w_in)


def _in_proj_kernel(a_ref, w_ref, wn_ref, o_ref, w_scr, *, n_aligned, shift):
    j, i = pl.program_id(0), pl.program_id(1)

    groups = w_ref.shape[0] // LANES

    @pl.when((i == 0) & (j < n_aligned))
    def _cast():
        for g in range(groups):
            w_scr[:, g * LANES:(g + 1) * LANES] = w_ref[g * LANES:(g + 1) * LANES, :].T.astype(BF16)

    @pl.when((i == 0) & (j >= n_aligned))
    def _cast_shifted():
        for g in range(groups):
            if g + 1 < groups:
                rows = w_ref[g * LANES + shift:(g + 1) * LANES + shift, :]
            else:
                rows = jnp.concatenate([w_ref[g * LANES + shift:, :], wn_ref[...]], axis=0)
            w_scr[:, g * LANES:(g + 1) * LANES] = rows.T.astype(BF16)

    o_ref[...] = _dot(a_ref[...], w_scr[...]).astype(o_ref.dtype)


def _in_proj(dm, l, a, w_in_t):
    m, k = a.shape
    off_if, n = dm.off_z, dm.n_main
    shift = 2 * dm.H
    tn = max(t for t in range(LANES, 1024 + 1, LANES) if off_if % t == 0 and n % t == 0)
    tm = _pick_tile(m, 1536)
    vmem = (2 * tm * k * 2 + 2 * k * (tn + shift) * 4 + k * tn * 2 + 2 * tm * tn * 2 + 2 * tm * tn * 4) / MIB + 8
    return pl.pallas_call(
        functools.partial(_in_proj_kernel, n_aligned=off_if // tn, shift=shift),
        grid=(n // tn, m // tm),
        in_specs=[pl.BlockSpec((tm, k), lambda j, i: (i, 0)),
                  pl.BlockSpec((None, tn, k), lambda j, i: (l, j, 0)),
                  pl.BlockSpec((None, shift, k), lambda j, i: (l, (j + 1) * (tn // shift), 0))],
        out_specs=pl.BlockSpec((tm, tn), lambda j, i: (i, j)),
        out_shape=jax.ShapeDtypeStruct((m, n), BF16),
        scratch_shapes=[pltpu.VMEM((k, tn), BF16)],
        compiler_params=_cparams(("arbitrary", "arbitrary"), vmem),
        name="in_proj",
    )(a, w_in_t, w_in_t)


def _pad_rows(x, rows):
    if x.shape[0] == rows:
        return x
    return jnp.concatenate([x, jnp.zeros((rows - x.shape[0], x.shape[1]), x.dtype)], axis=0)


def _mlstm_kernel(*refs, dm, lv, n_seq):
    a_ref = refs[10]
    seq_ok = pl.program_id(0) < n_seq

    @pl.when(seq_ok)
    def _run():
        _mlstm_step(*refs, dm=dm, lv=lv)

    @pl.when(jnp.logical_not(seq_ok))
    def _fill():
        a_ref[...] = jnp.zeros_like(a_ref)


def _mlstm_step(q_ref, k_ref, v_ref, o_ref, gt_ref, bias_ref, mhg_ref, c0_ref, n0_ref, m0_ref,
                a_ref, cout_ref, nout_ref, mout_ref, st_scr, m_scr, *, dm, lv):
    h_n, dk, dv = dm.H, dm.DK, dm.DV
    L = max(lv, LANES)
    c = pl.program_id(1)

    scale = dk ** -0.5

    @pl.when(c == 0)
    def _init():
        for h in range(h_n):
            st_scr[h, :, :dv] = c0_ref[h].T * (1.0 / scale)
            st_scr[h, :, dv:] = jnp.broadcast_to(n0_ref[h:h + 1, :] * (1.0 / scale), (LANES, dk)).T
        m_scr[...] = m0_ref[...]

    row = lax.broadcasted_iota(I32, (L, L), 0)
    col = lax.broadcasted_iota(I32, (L, L), 1)
    causal = col <= row
    lane = lax.broadcasted_iota(I32, (L, LANES), 1)
    rowl = lax.broadcasted_iota(I32, (L, LANES), 0)

    xg = _pad_rows(gt_ref[...], L) + bias_ref[...]
    f_log = jnp.minimum(xg, 0.0) - jnp.log1p(jnp.exp(-jnp.abs(xg)))
    gl = jnp.where(lane < h_n, xg, f_log)
    gl = jnp.where(rowl < lv, gl, 0.0)
    tri = causal.astype(BF16)
    g_hi = gl.astype(BF16)
    r1 = gl - g_hi.astype(F32)
    g_mid = r1.astype(BF16)
    g_lo = (r1 - g_mid.astype(F32)).astype(BF16)
    cum = _dot(tri, g_hi) + _dot(tri, g_mid) + _dot(tri, g_lo)
    b_all = pltpu.roll(cum, LANES - h_n, axis=1)
    a_all = jnp.where(rowl < lv, gl - b_all, NEG)

    kt_all = _pad_rows(k_ref[...], L).T
    q_all = _pad_rows(q_ref[...], L)
    v_all = _pad_rows(v_ref[...], L)
    a_rows = a_all.T
    m_old = m_scr[...]
    m_new = m_old
    lane1 = lax.broadcasted_iota(I32, (1, LANES), 1)
    ones_blk = jnp.ones((L, LANES), BF16)
    v_blocks = dv // LANES

    for h in range(h_n):
        a_row = a_rows[h:h + 1, :]
        a_rep = jnp.broadcast_to(a_all[:, h:h + 1], (L, LANES))
        b_rep = jnp.broadcast_to(b_all[:, h:h + 1], (L, LANES))
        m_prev = m_old[:, h:h + 1]
        big_m = jnp.maximum(jnp.max(jnp.where(causal, a_row, NEG), axis=1, keepdims=True), m_prev)
        d_w = jnp.where(causal, jnp.exp(jnp.minimum(a_row - big_m, 0.0)), 0.0)
        inter = jnp.exp(m_prev - big_m)
        qh = q_all[:, h * dk:(h + 1) * dk]
        v_parts = [v_all[:, h * dv + j * LANES:h * dv + (j + 1) * LANES] for j in range(v_blocks)]
        kt = kt_all[h * dk:(h + 1) * dk, :]
        sd = (_dot(qh, kt) * d_w).astype(BF16)
        st = st_scr[h]
        q_int = (inter * qh.astype(F32)).astype(BF16)
        x = _dot(jnp.concatenate([sd, q_int], axis=1),
                 jnp.concatenate([jnp.concatenate(v_parts + [ones_blk], axis=1), st.astype(BF16)], axis=0))
        den = x[:, dv:]
        inv = 1.0 / jnp.maximum(jnp.abs(den), jnp.exp(-(b_rep + big_m)) * (1.0 / scale))
        sq = jnp.sum(x[:, :dv] * x[:, :dv], axis=1, keepdims=True) * (1.0 / dv)
        sc = (inv * lax.rsqrt(inv * inv * sq + EPS))[:lv]
        for j in range(v_blocks):
            cols = slice(h * dv + j * LANES, h * dv + (j + 1) * LANES)
            gate_o = 0.5 * jnp.tanh(0.5 * o_ref[:, cols].astype(F32)) + 0.5
            a_ref[:, cols] = (x[:lv, j * LANES:(j + 1) * LANES] * sc * (mhg_ref[:, cols] * gate_o)).astype(BF16)
        m_last = big_m[L - 1:L, :]
        decay = jnp.exp(m_prev - m_last)
        w_rep = jnp.exp(a_rep - m_last)
        vw = jnp.concatenate([(p.astype(F32) * w_rep).astype(BF16) for p in v_parts] + [w_rep.astype(BF16)], axis=1)
        st_scr[h] = decay * st + _dot(kt, vw)
        m_new = jnp.where(lane1 == h, b_all[L - 1:L, h:h + 1] + m_last, m_new)
    m_scr[...] = m_new

    @pl.when(c == pl.num_programs(1) - 1)
    def _fin():
        for h in range(h_n):
            cout_ref[h] = st_scr[h, :, :dv].T * scale
            nout_ref[h:h + 1, :] = st_scr[h, :, dv:].T[0:1, :] * scale
        mout_ref[...] = m_scr[...]


def _mlstm(dm, proj, gates, bias, mhg, c0, n0, m0, ls, *, sample):
    if sample:
        b_n, n_grid, nc, lv, row0, n_rows = dm.Bs, ROW_TILE // dm.DS, 1, dm.DS, dm.Tp // dm.DS, ROW_TILE
    else:
        b_n, n_grid, nc, lv, row0, n_rows = dm.Bp, dm.Bp, dm.S // MLSTM_CHUNK, MLSTM_CHUNK, 0, dm.Tp
    h_n, dk, dv, qk, v = dm.H, dm.DK, dm.DV, dm.QK, dm.V
    rb = lambda b, c: row0 + b * nc + c
    sq = lambda b: jnp.minimum(b, b_n - 1)
    in_specs = [
        pl.BlockSpec((lv, qk), lambda b, c: (rb(b, c), 0)),
        pl.BlockSpec((lv, qk), lambda b, c: (rb(b, c), 1)),
        pl.BlockSpec((lv, v), lambda b, c: (rb(b, c), dm.off_v // v)),
        pl.BlockSpec((lv, v), lambda b, c: (rb(b, c), dm.off_o // v)),
        pl.BlockSpec((lv, LANES), lambda b, c: (rb(b, c), 0)),
        pl.BlockSpec((1, LANES), lambda b, c: (0, 0)),
        pl.BlockSpec((1, v), lambda b, c: (0, 0)),
        pl.BlockSpec((None, None, h_n, dv, dk), lambda b, c: (ls, sq(b), 0, 0, 0)),
        pl.BlockSpec((None, None, h_n, dk), lambda b, c: (ls, sq(b), 0, 0)),
        pl.BlockSpec((None, None, 1, LANES), lambda b, c: (ls, sq(b), 0, 0)),
    ]
    return pl.pallas_call(
        functools.partial(_mlstm_kernel, dm=dm, lv=lv, n_seq=b_n),
        grid=(n_grid, nc),
        in_specs=in_specs,
        out_specs=[pl.BlockSpec((lv, v), lambda b, c: (b * nc + c, 0)),
                   pl.BlockSpec((None, h_n, dv, dk), lambda b, c: (sq(b), 0, 0, 0)),
                   pl.BlockSpec((None, h_n, dk), lambda b, c: (sq(b), 0, 0)),
                   pl.BlockSpec((None, 1, LANES), lambda b, c: (sq(b), 0, 0))],
        out_shape=[jax.ShapeDtypeStruct((n_rows, v), BF16),
                   jax.ShapeDtypeStruct((b_n, h_n, dv, dk), F32),
                   jax.ShapeDtypeStruct((b_n, h_n, dk), F32),
                   jax.ShapeDtypeStruct((b_n, 1, LANES), F32)],
        scratch_shapes=[pltpu.VMEM((h_n, dk, dv + LANES), F32), pltpu.VMEM((1, LANES), F32)],
        compiler_params=_cparams(("arbitrary", "arbitrary"), 40),
        name="mlstm_sample" if sample else "mlstm_prompt",
    )(proj, proj, proj, proj, gates, bias, mhg, c0, n0, m0)


def _sgu_kernel(*refs, dm, lv, n_seq, emit_v):
    g_ref = refs[5]
    seq_ok = pl.program_id(0) < n_seq

    @pl.when(seq_ok)
    def _run():
        _sgu_step(*refs, dm=dm, lv=lv, emit_v=emit_v)

    @pl.when(jnp.logical_not(seq_ok))
    def _fill():
        g_ref[...] = jnp.zeros_like(g_ref)


def _sgu_step(*refs, dm, lv, emit_v):
    z_ref, lng_ref, lnb_ref, ws_ref, bst_ref, g_ref = refs[:6]
    gh, gch = dm.GH, dm.GH // dm.G
    z = jax.nn.gelu(z_ref[...].astype(F32))
    u, v = z[:, :gh], z[:, gh:]
    xc = v - jnp.mean(v, axis=-1, keepdims=True)
    vn = xc * lax.rsqrt(jnp.mean(xc * xc, axis=-1, keepdims=True) + EPS) * lng_ref[...] + lnb_ref[...]
    if emit_v:
        refs[6][...] = vn
    row = lax.broadcasted_iota(I32, (lv, lv), 0)
    col = lax.broadcasted_iota(I32, (lv, lv), 1)
    vb = vn.astype(BF16)
    for g in range(dm.G):
        w = jnp.where(col <= row, ws_ref[g, :lv, :lv], 0.0).astype(BF16)
        cols = slice(g * gch, (g + 1) * gch)
        for c in range(z.shape[0] // lv):
            rows = slice(c * lv, (c + 1) * lv)
            mixed = _dot(w, vb[rows, cols]) + bst_ref[:lv, g:g + 1]
            g_ref[rows, cols] = (u[rows, cols] * mixed).astype(BF16)


def _sgu(dm, proj, lng, lnb, ws, bst, *, sample):
    if sample:
        b_n, n_grid, nc, lv, br, row0, n_rows = dm.Bs, ROW_TILE // dm.DS, 1, dm.DS, dm.DS, dm.Tp // dm.DS, ROW_TILE
    else:
        br = SGU_CHUNKS_PER_STEP * dm.GC
        b_n, n_grid, nc, lv, row0, n_rows = dm.Bp, dm.Bp, dm.S // br, dm.GC, 0, dm.Tp
    gh = dm.GH
    rb = lambda b, c: row0 + b * nc + c
    in_specs = [
        pl.BlockSpec((br, 2 * gh), lambda b, c: (rb(b, c), dm.off_z // (2 * gh))),
        pl.BlockSpec((1, gh), lambda b, c: (0, 0)),
        pl.BlockSpec((1, gh), lambda b, c: (0, 0)),
        pl.BlockSpec((dm.G, dm.GC, dm.GC), lambda b, c: (0, 0, 0)),
        pl.BlockSpec((dm.GC, LANES), lambda b, c: (0, 0)),
    ]
    out_specs = [pl.BlockSpec((br, gh), lambda b, c: (b * nc + c, 0))]
    out_shape = [jax.ShapeDtypeStruct((n_rows, gh), BF16)]
    if sample:
        out_specs.append(pl.BlockSpec((None, lv, gh), lambda b, c: (jnp.minimum(b, b_n - 1), 0, 0)))
        out_shape.append(jax.ShapeDtypeStruct((b_n, lv, gh), F32))
    return pl.pallas_call(
        functools.partial(_sgu_kernel, dm=dm, lv=lv, n_seq=b_n, emit_v=sample),
        grid=(n_grid, nc),
        in_specs=in_specs,
        out_specs=out_specs,
        out_shape=out_shape,
        compiler_params=_cparams(("arbitrary", "arbitrary"), 24),
        name="sgu_sample" if sample else "sgu_prompt",
    )(proj, lng, lnb, ws, bst)


def _merge_kernel(ap_ref, as_ref, gp_ref, gs_ref, ga_ref, gb_ref, wa_ref, wb_ref, o_ref):
    is_s = pl.program_id(1) == pl.num_programs(1) - 1
    pa = _dot(jnp.where(is_s, as_ref[...], ap_ref[...]), wa_ref[...])
    pb = _dot(jnp.where(is_s, gs_ref[...], gp_ref[...]), wb_ref[...])
    sa = 0.5 * jnp.tanh(0.5 * ga_ref[...].astype(F32)) + 0.5
    sb = 0.5 * jnp.tanh(0.5 * gb_ref[...].astype(F32)) + 0.5
    o_ref[...] = (sa * pa + sb * pb).astype(o_ref.dtype)


def _merge(dm, a_p, a_s, g_p, g_s, proj, wa, wb):
    tc, d, v, gh = dm.Tc, dm.D, dm.V, dm.GH
    tm = ROW_TILE
    tn = _pick_tile(d, 1024)
    np_tiles = dm.Tp // tm
    prompt = lambda j, i: (jnp.minimum(i, np_tiles - 1), 0)
    const = lambda j, i: (0, 0)
    blocks = 2 * tm * (v + gh) * 2 + 3 * tm * tn * 2 + (v + gh) * tn * 2
    return pl.pallas_call(
        _merge_kernel,
        grid=(d // tn, tc // tm),
        in_specs=[pl.BlockSpec((tm, v), prompt),
                  pl.BlockSpec((tm, v), const),
                  pl.BlockSpec((tm, gh), prompt),
                  pl.BlockSpec((tm, gh), const),
                  pl.BlockSpec((tm, tn), lambda j, i: (i, dm.off_ga // tn + j)),
                  pl.BlockSpec((tm, tn), lambda j, i: (i, dm.off_gb // tn + j)),
                  pl.BlockSpec((v, tn), lambda j, i: (0, j)),
                  pl.BlockSpec((gh, tn), lambda j, i: (0, j))],
        out_specs=pl.BlockSpec((tm, tn), lambda j, i: (i, j)),
        out_shape=jax.ShapeDtypeStruct((tc, d), BF16),
        compiler_params=_cparams(("arbitrary", "arbitrary"), (2 * blocks + 6 * tm * tn * 4) / MIB + 8),
        name="merge",
    )(a_p, a_s, g_p, g_s, proj, proj, wa, wb)


def _route(logits, eg, ne):
    epg = ne // eg
    lane = lax.broadcasted_iota(I32, logits.shape, 1)
    gmask = lane < eg
    gmax = jnp.max(jnp.where(gmask, logits, NEG), axis=1, keepdims=True)
    gexp = jnp.where(gmask, jnp.exp(jnp.minimum(logits - gmax, 0.0)), 0.0)
    pg = gexp / jnp.sum(gexp, axis=1, keepdims=True)
    p_grp = jnp.max(pg, axis=1, keepdims=True)
    g_sel = jnp.min(jnp.where(gmask & (pg == p_grp), lane, LANES), axis=1, keepdims=True)
    lo = eg + g_sel * epg
    emask = (lane >= lo) & (lane < lo + epg)
    emax = jnp.max(jnp.where(emask, logits, NEG), axis=1, keepdims=True)
    eexp = jnp.where(emask, jnp.exp(jnp.minimum(logits - emax, 0.0)), 0.0)
    pe = eexp / jnp.sum(eexp, axis=1, keepdims=True)
    p1 = jnp.max(jnp.where(emask, pe, -1.0), axis=1, keepdims=True)
    i1 = jnp.min(jnp.where(emask & (pe == p1), lane, LANES), axis=1, keepdims=True)
    rest = emask & (lane != i1)
    p2 = jnp.max(jnp.where(rest, pe, -1.0), axis=1, keepdims=True)
    i2 = jnp.min(jnp.where(rest & (pe == p2), lane, LANES), axis=1, keepdims=True)
    psum = p1 + p2
    eid = jnp.where(lane == 0, i1 - eg, jnp.where(lane == 1, i2 - eg, 0))
    gate = jnp.where(lane == 0, p_grp * (p1 / psum), jnp.where(lane == 1, p_grp * (p2 / psum), 0.0))
    return eid, gate


def _outproj_kernel(*refs, dm, first):
    if first:
        m_ref, xp_ref, xs_ref = refs[:3]
        refs = refs[3:]
    else:
        m_ref, x_ref = refs[:2]
        refs = refs[2:]
    w_ref, g_ref, mod_ref, wr_ref, rb_ref, x1_ref, hp_ref, eid_ref, gate_ref, acc_scr, hi_scr, lo_scr = refs
    i = pl.program_id(0)
    t = jnp.maximum(i - 1, 0)
    is_s = t == dm.n_tiles - 1
    bp = jnp.minimum(t // (dm.S // ROW_TILE), dm.Bp - 1)
    cur, prev = i % 2, (i + 1) % 2
    g = g_ref[...]

    @pl.when(i == 0)
    def _no_previous_tile():
        acc_scr[1] = jnp.zeros(acc_scr.shape[1:], F32)

    def chunk(j):
        rows = pl.ds(j * ROW_CHUNK, ROW_CHUNK)
        idx = jnp.where(is_s, dm.Bp + j, bp)
        x = jnp.where(is_s, xs_ref[rows, :], xp_ref[rows, :]) if first else x_ref[rows, :]
        x1 = x + mod_ref[GT1, pl.ds(idx, 1), :] * acc_scr[prev, rows, :]
        x1_ref[rows, :] = x1
        h = _rms_mod(x1, g, mod_ref[SC2, pl.ds(idx, 1), :], mod_ref[SH2, pl.ds(idx, 1), :])
        hi, lo = _split_bf16(h)
        hi_scr[rows, :] = hi
        lo_scr[rows, :] = lo
        hp_ref[rows, :] = _pack_pair(h)

    n_chunks = ROW_TILE // ROW_CHUNK
    n_slices = min(dm.D // 256, n_chunks)
    for s in range(n_slices):
        cols = slice(s * dm.D // n_slices, (s + 1) * dm.D // n_slices)
        acc_scr[cur, :, cols] = _dot(m_ref[...], w_ref[:, cols])
        for j in range(s * n_chunks // n_slices, (s + 1) * n_chunks // n_slices):
            chunk(j)
    w_hi, w_lo = _split_bf16(wr_ref[...])
    logits = _dot3(hi_scr[...], lo_scr[...], w_hi, w_lo) + rb_ref[...]
    eid, gate = _route(logits, dm.EG, dm.NE)
    eid_ref[...] = eid
    gate_ref[...] = gate


def _outproj(dm, l, merged, x, w_out, g2, mod, w_router, b_router):
    d, tc = dm.D, dm.Tc
    tile = lambda i: (jnp.maximum(i - 1, 0), 0)
    const = lambda i: (0, 0)
    first = isinstance(x, tuple)
    if first:
        np_tiles = dm.Tp // ROW_TILE
        x_args = list(x)
        x_specs = [pl.BlockSpec((ROW_TILE, d), lambda i: (jnp.clip(i - 1, 0, np_tiles - 1), 0)),
                   pl.BlockSpec((ROW_TILE, d), const)]
    else:
        x_args = [x]
        x_specs = [pl.BlockSpec((ROW_TILE, d), tile)]
    return pl.pallas_call(
        functools.partial(_outproj_kernel, dm=dm, first=first),
        grid=(dm.n_tiles + 1,),
        in_specs=[pl.BlockSpec((ROW_TILE, d), lambda i: (jnp.minimum(i, dm.n_tiles - 1), 0))] + x_specs + [
                  pl.BlockSpec((d, d), const),
                  pl.BlockSpec((None, 1, d), lambda i: (l, 0, 0)),
                  pl.BlockSpec((None, 6, dm.n_mod, d), lambda i: (l, 0, 0, 0)),
                  pl.BlockSpec((d, LANES), const),
                  pl.BlockSpec((1, LANES), const)],
        out_specs=[pl.BlockSpec((ROW_TILE, d), tile),
                   pl.BlockSpec((ROW_TILE, d // 2), tile),
                   pl.BlockSpec((ROW_TILE, LANES), tile),
                   pl.BlockSpec((ROW_TILE, LANES), tile)],
        out_shape=[jax.ShapeDtypeStruct((tc, d), F32),
                   jax.ShapeDtypeStruct((tc, d // 2), U32),
                   jax.ShapeDtypeStruct((tc, LANES), I32),
                   jax.ShapeDtypeStruct((tc, LANES), F32)],
        scratch_shapes=[pltpu.VMEM((2, ROW_TILE, d), F32), pltpu.VMEM((ROW_TILE, d), BF16),
                        pltpu.VMEM((ROW_TILE, d), BF16)],
        compiler_params=_cparams(("arbitrary",), (4 * d * d + ROW_TILE * d * 48) / MIB + 12),
        name="out_proj",
    )(merged, *x_args, w_out, g2, mod, w_router, b_router)


def _rank_kernel(eid_ref, rank_ref, cnt_ref, run_scr, start_scr):
    sweep, i = pl.program_id(0), pl.program_id(1)

    @pl.when(i == 0)
    def _init():
        @pl.when(sweep == 1)
        def _starts():
            padded = jnp.ceil(run_scr[...] * (1.0 / MOE_BLOCK)) * MOE_BLOCK
            earlier = (lax.broadcasted_iota(I32, (LANES, LANES), 0)
                       < lax.broadcasted_iota(I32, (LANES, LANES), 1)).astype(BF16)
            p_hi, p_lo = _split_bf16(jnp.broadcast_to(padded, (8, LANES)))
            start_scr[...] = (_dot(p_hi, earlier) + _dot(p_lo, earlier))[0:1, :]
            cnt_ref[...] = run_scr[...].astype(I32)

        @pl.when(sweep == 0)
        def _no_starts_yet():
            start_scr[...] = jnp.zeros_like(start_scr)

        run_scr[...] = jnp.zeros_like(run_scr)

    eid = eid_ref[...]
    lane = lax.broadcasted_iota(I32, eid.shape, 1)
    e0 = lane == eid[:, 0:1]
    e1 = lane == eid[:, 1:2]
    hot = (e0 | e1).astype(BF16)
    n = eid.shape[0]
    strict = (lax.broadcasted_iota(I32, (n, n), 1) < lax.broadcasted_iota(I32, (n, n), 0)).astype(BF16)
    before = _dot(strict, hot) + run_scr[...]
    slot = before + start_scr[...]
    s0 = jnp.sum(jnp.where(e0, slot, 0.0), axis=1, keepdims=True)
    s1 = jnp.sum(jnp.where(e1, slot, 0.0), axis=1, keepdims=True)
    rank_ref[...] = jnp.where(lane == 0, s0, jnp.where(lane == 1, s1, 0.0)).astype(I32)
    run_scr[...] = run_scr[...] + jnp.sum(hot.astype(F32), axis=0, keepdims=True)


def _rank(dm, eid):
    return pl.pallas_call(
        _rank_kernel,
        grid=(2, dm.n_tiles),
        in_specs=[pl.BlockSpec((ROW_TILE, LANES), lambda s, i: (i, 0))],
        out_specs=[pl.BlockSpec((ROW_TILE, LANES), lambda s, i: (s * i, 0)),
                   pl.BlockSpec((1, LANES), lambda s, i: (0, 0))],
        out_shape=[jax.ShapeDtypeStruct((dm.Tc, LANES), I32), jax.ShapeDtypeStruct((1, LANES), I32)],
        scratch_shapes=[pltpu.VMEM((1, LANES), F32), pltpu.VMEM((1, LANES), F32)],
        compiler_params=_cparams(("arbitrary", "arbitrary"), 16),
        name="moe_rank",
    )(eid)


def _dispatch_kernel(clear_ref, dest_ref, h_ref, xs_ref, zero_scr, sem):
    @pl.when(pl.program_id(0) == 0)
    def _clear_blocks():
        zero_scr[...] = jnp.zeros_like(zero_scr)

        def clear_copy(start):
            return pltpu.make_async_copy(zero_scr, xs_ref.at[pl.ds(pl.multiple_of(start, MOE_BLOCK), MOE_BLOCK)], sem)

        for n in range(clear_ref.shape[0]):
            @pl.when(clear_ref[n] >= 0)
            def _start():
                clear_copy(clear_ref[n]).start()
        for n in range(clear_ref.shape[0]):
            @pl.when(clear_ref[n] >= 0)
            def _wait():
                clear_copy(clear_ref[n]).wait()

    for r in range(ROW_TILE):
        for k in range(TOP_K):
            pltpu.make_async_copy(h_ref.at[pl.ds(r, 1)], xs_ref.at[pl.ds(dest_ref[TOP_K * r + k], 1)],
                                  sem).start(priority=k)
    for k in range(TOP_K):
        pltpu.make_async_copy(h_ref, xs_ref.at[pl.ds(0, ROW_TILE)], sem).wait()


def _dispatch(dm, clears, dest, hp):
    d2 = dm.D // 2
    return pl.pallas_call(
        _dispatch_kernel,
        grid_spec=pltpu.PrefetchScalarGridSpec(
            num_scalar_prefetch=1,
            grid=(dm.n_tiles,),
            in_specs=[pl.BlockSpec((ROW_TILE * TOP_K,), lambda i, t: (i,), memory_space=pltpu.SMEM),
                      pl.BlockSpec((ROW_TILE, d2), lambda i, t: (i, 0))],
            out_specs=pl.BlockSpec(memory_space=pl.ANY),
            scratch_shapes=[pltpu.VMEM((MOE_BLOCK, d2), U32), pltpu.SemaphoreType.DMA]),
        out_shape=jax.ShapeDtypeStruct((dm.cap, d2), U32),
        compiler_params=_cparams(("arbitrary",), 16),
        name="moe_dispatch",
    )(clears, dest, hp)


def _expert_kernel(be_ref, nu_ref, nx_ref, sl_ref, xs_ref, wg_hbm, wu_hbm, wd_hbm, ys_ref,
                   wg_f, wu_f, wd_f, wg_s, wu_s, wd_s, sem, *, layer):
    j = pl.program_id(0)
    e, slot = be_ref[j], sl_ref[j]
    used = j < nu_ref[0]
    first = used & ((j == 0) | (e != be_ref[jnp.maximum(j - 1, 0)]))

    def weight_copies(expert, s):
        return [pltpu.make_async_copy(src.at[layer, expert], dst.at[s], sem.at[s])
                for src, dst in ((wg_hbm, wg_f), (wu_hbm, wu_f), (wd_hbm, wd_f))]

    @pl.when(j == 0)
    def _first_expert():
        for cp in weight_copies(e, slot):
            cp.start()

    @pl.when(first)
    def _switch_expert():
        for cp in weight_copies(e, slot):
            cp.wait()
        wg_s[...] = wg_f[slot].astype(BF16)
        wu_s[...] = wu_f[slot].astype(BF16)
        wd_s[...] = wd_f[slot].astype(BF16)

        @pl.when(nx_ref[j] >= 0)
        def _prefetch_next():
            for cp in weight_copies(nx_ref[j], 1 - slot):
                cp.start()

    @pl.when(used)
    def _compute():
        d2 = xs_ref.shape[1]
        lo, hi = _unpack_pair(xs_ref[...])
        lo, hi = lo.astype(BF16), hi.astype(BF16)
        hg = _dot(lo, wg_s[:d2, :]) + _dot(hi, wg_s[d2:, :])
        hu = _dot(lo, wu_s[:d2, :]) + _dot(hi, wu_s[d2:, :])
        act = (hg * jax.nn.sigmoid(hg) * hu).astype(BF16)
        ys_ref[...] = _pack_pair(_dot(act, wd_s[...]))

    @pl.when(jnp.logical_not(used))
    def _idle():
        ys_ref[...] = jnp.zeros_like(ys_ref)


def _experts(dm, l, blk_e, n_used, blk_next, blk_slot, xs, w_gate, w_up, w_down):
    d, de, d2 = dm.D, dm.DE, dm.D // 2
    nb = dm.cap // MOE_BLOCK
    rows = lambda j, be, nu, nx, sl: (jnp.minimum(j, nu[0] - 1), 0)
    hbm = pl.BlockSpec(memory_space=pl.ANY)
    return pl.pallas_call(
        functools.partial(_expert_kernel, layer=l),
        grid_spec=pltpu.PrefetchScalarGridSpec(
            num_scalar_prefetch=4,
            grid=(nb,),
            in_specs=[pl.BlockSpec((MOE_BLOCK, d2), rows), hbm, hbm, hbm],
            out_specs=pl.BlockSpec((MOE_BLOCK, d2), lambda j, be, nu, nx, sl: (j, 0)),
            scratch_shapes=[pltpu.VMEM((2, d, de), F32), pltpu.VMEM((2, d, de), F32), pltpu.VMEM((2, de, d), F32),
                            pltpu.VMEM((d, de), BF16), pltpu.VMEM((d, de), BF16), pltpu.VMEM((de, d), BF16),
                            pltpu.SemaphoreType.DMA((2,))]),
        out_shape=jax.ShapeDtypeStruct((dm.cap, d2), U32),
        compiler_params=_cparams(("arbitrary",), (3 * d * de * (2 * 4 + 2) + MOE_BLOCK * d * 28) / MIB + 8),
        name="moe_experts",
    )(blk_e, n_used, blk_next, blk_slot, xs, w_gate, w_up, w_down)


def _combine_kernel(*refs, dm, last):
    if last:
        dest_ref, destn_ref, ys_ref, x_ref, gate_ref, mod_ref, g_ref, yp_ref, ysm_ref, ybuf, y_scr, sem = refs
    else:
        (dest_ref, destn_ref, ys_ref, x_ref, gate_ref, mod_ref, g_ref, modn_ref, w_ref,
         o_ref, h_ref, p_ref, ybuf, lo_scr, sem) = refs
    i = pl.program_id(0)
    is_s = i == dm.n_tiles - 1
    bp = jnp.minimum(i // (dm.S // ROW_TILE), dm.Bp - 1)
    d2 = dm.D // 2
    g = g_ref[...]
    slot = i % 2

    def issue(idx_ref, to_slot, r0, n):
        for r in range(n):
            for k in range(TOP_K):
                pltpu.make_async_copy(ys_ref.at[pl.ds(idx_ref[TOP_K * (r0 + r) + k], 1)],
                                      ybuf.at[to_slot, k, pl.ds(r0 + r, 1)], sem.at[to_slot]).start(priority=k)

    def wait_slot(s):
        for k in range(TOP_K):
            pltpu.make_async_copy(ys_ref.at[pl.ds(0, ROW_TILE)], ybuf.at[s, k], sem.at[s]).wait()

    @pl.when(i == 0)
    def _first_tile():
        issue(dest_ref, 0, 0, ROW_TILE)

    wait_slot(slot)

    def chunk(j):
        rows = pl.ds(j * ROW_CHUNK, ROW_CHUNK)
        issue(destn_ref, 1 - slot, j * ROW_CHUNK, ROW_CHUNK)
        idx = jnp.where(is_s, dm.Bp + j, bp)
        gt = mod_ref[GT2, pl.ds(idx, 1), :]
        gates = gate_ref[rows, :]
        g0, g1 = gates[:, 0:1], gates[:, 1:2]
        lo0, hi0 = _unpack_pair(ybuf[slot, 0, rows, :])
        lo1, hi1 = _unpack_pair(ybuf[slot, 1, rows, :])
        x_lo = x_ref[rows, :d2] + gt[:, :d2] * (g0 * lo0 + g1 * lo1)
        x_hi = x_ref[rows, d2:] + gt[:, d2:] * (g0 * hi0 + g1 * hi1)
        ms = (jnp.sum(x_lo * x_lo, axis=-1, keepdims=True)
              + jnp.sum(x_hi * x_hi, axis=-1, keepdims=True)) * (1.0 / dm.D)
        rs = lax.rsqrt(ms + EPS)
        if last:
            y_scr[rows, :d2] = x_lo * rs * g[:, :d2]
            y_scr[rows, d2:] = x_hi * rs * g[:, d2:]
        else:
            o_ref[rows, :d2] = x_lo
            o_ref[rows, d2:] = x_hi
            sc = modn_ref[SC1, pl.ds(idx, 1), :]
            sh = modn_ref[SH1, pl.ds(idx, 1), :]
            for half, xh in ((slice(0, d2), x_lo), (slice(d2, dm.D), x_hi)):
                h = xh * rs * g[:, half] * (1.0 + sc[:, half]) + sh[:, half]
                hi, lo = _split_bf16(h)
                h_ref[rows, half] = hi
                lo_scr[rows, half] = lo

    for j in range(ROW_TILE // ROW_CHUNK):
        chunk(j)

    @pl.when(is_s)
    def _drain():
        wait_slot(1 - slot)

    if last:
        @pl.when(is_s)
        def _sample_rows():
            ysm_ref[...] = y_scr[...]

        @pl.when(jnp.logical_not(is_s))
        def _prompt_rows():
            yp_ref[...] = y_scr[...]
    else:
        w_hi, w_lo = _gate_weight(w_ref)
        p_ref[...] = _dot3(h_ref[...], lo_scr[...], w_hi, w_lo)


def _combine(dm, l, dest, ys, x1, gate, mod, g_next, w_in=None):
    d = dm.D
    last = w_in is None
    tile = lambda i: (i, 0)
    np_tiles = dm.Tp // ROW_TILE
    in_specs = [pl.BlockSpec((ROW_TILE * TOP_K,), lambda i: (i,), memory_space=pltpu.SMEM),
                pl.BlockSpec((ROW_TILE * TOP_K,), lambda i: (jnp.minimum(i + 1, dm.n_tiles - 1),),
                             memory_space=pltpu.SMEM),
                pl.BlockSpec(memory_space=pl.ANY),
                pl.BlockSpec((ROW_TILE, d), tile),
                pl.BlockSpec((ROW_TILE, LANES), tile),
                pl.BlockSpec((None, 6, dm.n_mod, d), lambda i: (l, 0, 0, 0))]
    scratch = [pltpu.VMEM((2, TOP_K, ROW_TILE, d // 2), U32)]
    if last:
        args = (dest, dest, ys, x1, gate, mod, g_next)
        in_specs.append(pl.BlockSpec((1, d), lambda i: (0, 0)))
        out_specs = [pl.BlockSpec((ROW_TILE, d), lambda i: (jnp.minimum(i, np_tiles - 1), 0)),
                     pl.BlockSpec((ROW_TILE, d), lambda i: (0, 0))]
        out_shape = [jax.ShapeDtypeStruct((dm.Tp, d), F32), jax.ShapeDtypeStruct((ROW_TILE, d), F32)]
        scratch.append(pltpu.VMEM((ROW_TILE, d), F32))
    else:
        args = (dest, dest, ys, x1, gate, mod, g_next, mod, w_in)
        in_specs += [pl.BlockSpec((None, 1, d), lambda i: (l + 1, 0, 0)),
                     pl.BlockSpec((None, 6, dm.n_mod, d), lambda i: (l + 1, 0, 0, 0)),
                     pl.BlockSpec((None, 2 * dm.H, d), lambda i: (l + 1, dm.off_z // (2 * dm.H), 0))]
        out_specs = [pl.BlockSpec((ROW_TILE, d), tile), pl.BlockSpec((ROW_TILE, d), tile),
                     pl.BlockSpec((ROW_TILE, LANES), tile)]
        out_shape = [jax.ShapeDtypeStruct((dm.Tc, d), F32), jax.ShapeDtypeStruct((dm.Tc, d), BF16),
                     jax.ShapeDtypeStruct((dm.Tc, LANES), F32)]
        scratch.append(pltpu.VMEM((ROW_TILE, d), BF16))
    scratch.append(pltpu.SemaphoreType.DMA((2,)))
    return pl.pallas_call(
        functools.partial(_combine_kernel, dm=dm, last=last),
        grid=(dm.n_tiles,),
        in_specs=in_specs,
        out_specs=out_specs,
        out_shape=out_shape,
        scratch_shapes=scratch,
        compiler_params=_cparams(("arbitrary",), ROW_TILE * d * 40 / MIB + 16),
        name="moe_combine_last" if last else "moe_combine",
    )(*args)


def _pad_lanes(x, n=LANES):
    return jnp.pad(x, [(0, 0)] * (x.ndim - 1) + [(0, n - x.shape[-1])])


def _moe_plan(dm, slots, counts):
    counts = counts[0, :dm.NE]
    padded = (counts + MOE_BLOCK - 1) // MOE_BLOCK * MOE_BLOCK
    pad_end = jnp.cumsum(padded)
    dest = slots[:, :TOP_K].reshape(-1)
    nb = dm.cap // MOE_BLOCK
    n_used = (pad_end[-1] // MOE_BLOCK).astype(I32)
    starts = jnp.minimum(jnp.arange(nb, dtype=I32), n_used - 1) * MOE_BLOCK
    blk_e = jnp.minimum(jnp.sum(pad_end[None, :] <= starts[:, None], axis=1), dm.NE - 1).astype(I32)
    ids = jnp.arange(dm.NE, dtype=I32)
    blocks = jnp.arange(nb, dtype=I32)
    clears = jnp.concatenate([jnp.where(padded > 0, pad_end - MOE_BLOCK, -1),
                              jnp.where(blocks >= n_used, blocks * MOE_BLOCK, -1)]).astype(I32)
    live = jnp.where(counts > 0, ids, dm.NE)
    nxt = lax.cummin(jnp.concatenate([live[1:], jnp.full((1,), dm.NE, I32)]), reverse=True)
    blk_next = jnp.where(nxt[blk_e] < dm.NE, nxt[blk_e], -1).astype(I32)
    blk_slot = ((jnp.cumsum(counts > 0) - 1)[blk_e] % 2).astype(I32)
    return dest, blk_e, n_used.reshape(1), clears, blk_next, blk_slot


def kernel(x_prompt, x_sample, c_prompt, c_sample, state_mlstm_C, state_mlstm_n, state_mlstm_m, w_ada, b_ada, norm1_g, norm2_g, w_in, b_igate, b_fgate, mh_norm_g, sgu_ln_g, sgu_ln_b, w_spatial, b_spatial, w_branch_a, w_branch_b, w_out, w_router_group, b_router_group, w_router_expert, b_router_expert, w_expert_gate, w_expert_up, w_expert_down, final_norm_g):
    bp, s, d = x_prompt.shape
    bs, ds_, _ = x_sample.shape
    depth = w_in.shape[0]
    h_n = b_igate.shape[1]
    dv, dk = state_mlstm_C.shape[-2:]
    qk, v = h_n * dk, h_n * dv
    gh = (w_in.shape[2] - (2 * qk + 2 * v + 2 * h_n + 2 * d)) // 2
    dm = Dims(Bp=bp, S=s, Bs=bs, DS=ds_, D=d, DEPTH=depth, H=h_n, DK=dk, DV=dv, GH=gh,
              G=w_spatial.shape[1], GC=w_spatial.shape[2], EG=w_router_group.shape[-1],
              NE=w_router_expert.shape[-1], DE=w_expert_gate.shape[-1])
    assert dm.DS == ROW_CHUNK and dm.Ts <= ROW_TILE and s % ROW_TILE == 0 and s % MLSTM_CHUNK == 0
    assert dm.EG + dm.NE <= LANES and 2 * h_n <= LANES and (2 * h_n) % 8 == 0 and dm.Bp + dm.Bs <= 32
    w_in = jnp.swapaxes(w_in, 1, 2)
    assert dm.off_v % v == 0 and dm.off_o % v == 0 and dm.off_z % (2 * gh) == 0 and d % 256 == 0

    c_all = jnp.concatenate([c_prompt, c_sample, jnp.zeros((32 - bp - bs, d), F32)], axis=0)
    mod = _adaln(c_all, w_ada, b_ada)[:, :bp + bs].reshape(depth, bp + bs, 6, d)
    mod = jnp.pad(mod, ((0, 0), (0, dm.n_mod - bp - bs), (0, 0), (0, 0))).transpose(0, 2, 1, 3)

    xs_rows = jnp.pad(x_sample.reshape(dm.Ts, d), ((0, ROW_TILE - dm.Ts), (0, 0)))
    x = (x_prompt.reshape(dm.Tp, d), xs_rows)
    zeros_c = jnp.zeros((1, bp, h_n, dv, dk), F32)
    zeros_n = jnp.zeros((1, bp, h_n, dk), F32)
    zeros_m = jnp.zeros((1, bp, 1, LANES), F32)
    m_in = _pad_lanes(state_mlstm_m)[:, :, None, :]
    g1 = norm1_g.reshape(depth, 1, d)
    out_c_p, out_n_p, out_m_p, out_c_s, out_n_s, out_m_s, out_v = [], [], [], [], [], [], []

    for l in range(depth):
        if l == 0:
            hb, gates = _norm_proj(dm, l, x, g1, mod, w_in, SC1, SH1)
        proj = _in_proj(dm, l, hb, w_in)

        bias = _pad_lanes(jnp.concatenate([b_igate[l], b_fgate[l]])[None, :])
        mhg = mh_norm_g[l][None, :]
        a_p, c_p, n_p, m_p = _mlstm(dm, proj, gates, bias, mhg, zeros_c, zeros_n, zeros_m, 0, sample=False)
        a_s, c_s, n_s, m_s = _mlstm(dm, proj, gates, bias, mhg, state_mlstm_C, state_mlstm_n, m_in, l,
                                    sample=True)
        lng, lnb = sgu_ln_g[l][None, :], sgu_ln_b[l][None, :]
        bst = _pad_lanes(b_spatial[l].T)
        (g_p,) = _sgu(dm, proj, lng, lnb, w_spatial[l], bst, sample=False)
        g_s, v_rows = _sgu(dm, proj, lng, lnb, w_spatial[l], bst, sample=True)

        merged = _merge(dm, a_p, a_s, g_p, g_s, proj, w_branch_a[l].astype(BF16), w_branch_b[l].astype(BF16))
        w_router = _pad_lanes(jnp.concatenate([w_router_group[l], w_router_expert[l]], axis=1))
        b_router = _pad_lanes(jnp.concatenate([b_router_group[l], b_router_expert[l]])[None, :])
        x1, hp, eid, gate = _outproj(dm, l, merged, x, w_out[l].astype(BF16), norm2_g.reshape(depth, 1, d),
                                     mod, w_router, b_router)

        slots, counts = _rank(dm, eid)
        dest, blk_e, n_used, clears, blk_next, blk_slot = _moe_plan(dm, slots, counts)
        xs = _dispatch(dm, clears, dest, hp)
        ys = _experts(dm, l, blk_e, n_used, blk_next, blk_slot, xs,
                      w_expert_gate, w_expert_up, w_expert_down)
        if l + 1 < depth:
            x, hb, gates = _combine(dm, l, dest, ys, x1, gate, mod, g1, w_in)
        else:
            y_p, y_s = _combine(dm, l, dest, ys, x1, gate, mod, final_norm_g[None, :])

        out_c_p.append(c_p)
        out_n_p.append(n_p)
        out_m_p.append(m_p[:, 0, :h_n])
        out_c_s.append(c_s)
        out_n_s.append(n_s)
        out_m_s.append(m_s[:, 0, :h_n])
        out_v.append(v_rows)

    return (y_p.reshape(bp, s, d), y_s[:dm.Ts].reshape(bs, ds_, d),
            jnp.stack(out_c_p), jnp.stack(out_n_p), jnp.stack(out_m_p),
            jnp.stack(out_c_s), jnp.stack(out_n_s), jnp.stack(out_m_s), jnp.stack(out_v))
```

```python
import functools
from typing import NamedTuple

import jax
import jax.numpy as jnp
from jax import lax
from jax.experimental import pallas as pl
from jax.experimental.pallas import tpu as pltpu

F32, BF16, U32, I32 = jnp.float32, jnp.bfloat16, jnp.uint32, jnp.int32
EPS = 1e-6
NEG = -1e30
LANES = 128
ROW_TILE = 512
ROW_CHUNK = 16
MLSTM_CHUNK = 256
MOE_BLOCK = 512
TOP_K = 2
CHUNK_UNROLL = 2
SGU_CHUNKS_PER_STEP = 4
MIB = 1024 * 1024
SH1, SC1, GT1, SH2, SC2, GT2 = range(6)


class Dims(NamedTuple):
    Bp: int
    S: int
    Bs: int
    DS: int
    D: int
    DEPTH: int
    H: int
    DK: int
    DV: int
    GH: int
    G: int
    GC: int
    EG: int
    NE: int
    DE: int

    @property
    def QK(self):
        return self.H * self.DK

    @property
    def V(self):
        return self.H * self.DV

    @property
    def Tp(self):
        return self.Bp * self.S

    @property
    def Ts(self):
        return self.Bs * self.DS

    @property
    def Tc(self):
        return self.Tp + ROW_TILE

    @property
    def n_tiles(self):
        return self.Tc // ROW_TILE

    @property
    def n_mod(self):
        return self.Bp + ROW_TILE // ROW_CHUNK

    @property
    def off_v(self):
        return 2 * self.QK

    @property
    def off_o(self):
        return 2 * self.QK + self.V

    @property
    def off_z(self):
        return 2 * self.QK + 2 * self.V

    @property
    def off_ga(self):
        return self.off_z + 2 * self.GH

    @property
    def off_gb(self):
        return self.off_ga + self.D

    @property
    def n_main(self):
        return self.off_gb + self.D

    @property
    def cap(self):
        tk = self.Tc * TOP_K
        return -(-(tk + self.NE * (MOE_BLOCK - 1)) // MOE_BLOCK) * MOE_BLOCK


def _cparams(semantics, vmem_mib):
    return pltpu.CompilerParams(dimension_semantics=semantics, vmem_limit_bytes=int(vmem_mib * MIB))


def _pick_tile(n, cap, unit=256):
    best = None
    for t in range(unit, min(n, cap) + 1, unit):
        if n % t == 0:
            best = t
    assert best is not None, (n, cap)
    return best


def _dot(a, b):
    return jnp.dot(a, b, preferred_element_type=F32)


def _split_bf16(x):
    hi = x.astype(BF16)
    lo = (x - hi.astype(F32)).astype(BF16)
    return hi, lo


def _dot3(a_hi, a_lo, w_hi, w_lo):
    return _dot(a_hi, w_hi) + _dot(a_hi, w_lo) + _dot(a_lo, w_hi)


def _gate_weight(w_ref):
    rows, d = w_ref.shape
    w = jnp.concatenate([w_ref[...], jnp.zeros((LANES - rows, d), F32)], axis=0).T
    return _split_bf16(w)


def _rms_mod(x, g, sc, sh):
    y = x * lax.rsqrt(jnp.mean(x * x, axis=-1, keepdims=True) + EPS) * g
    return y * (1.0 + sc) + sh


def _pack_pair(x):
    d2 = x.shape[-1] // 2
    bits = lax.bitcast_convert_type(x.astype(BF16).astype(F32), U32)
    return (bits[:, :d2] >> 16) | (bits[:, d2:] & jnp.uint32(0xFFFF0000))


def _unpack_pair(p):
    lo = lax.bitcast_convert_type(p << 16, F32)
    hi = lax.bitcast_convert_type(p & jnp.uint32(0xFFFF0000), F32)
    return lo, hi


def _adaln_kernel(c_ref, w_ref, b_ref, o_ref):
    c = c_ref[...]
    s = (c * jax.nn.sigmoid(c)).astype(BF16)
    o_ref[...] = _dot(s, w_ref[...].astype(BF16)) + b_ref[...]


def _adaln(c_all, w_ada, b_ada):
    depth, d, n6 = w_ada.shape
    r = c_all.shape[0]
    tn = _pick_tile(n6, 1024)
    return pl.pallas_call(
        _adaln_kernel,
        grid=(depth, n6 // tn),
        in_specs=[pl.BlockSpec((r, d), lambda l, j: (0, 0)),
                  pl.BlockSpec((None, d, tn), lambda l, j: (l, 0, j)),
                  pl.BlockSpec((None, 1, tn), lambda l, j: (l, 0, j))],
        out_specs=pl.BlockSpec((None, r, tn), lambda l, j: (l, 0, j)),
        out_shape=jax.ShapeDtypeStruct((depth, r, n6), F32),
        compiler_params=_cparams(("arbitrary", "arbitrary"), 2 * d * tn * 4 / MIB + 3 * d * tn * 2 / MIB + 8),
        name="adaln",
    )(c_all, w_ada, b_ada.reshape(depth, 1, n6))


def _norm_proj_kernel(*refs, first, dm, k_sc, k_sh):
    if first:
        xp_ref, xs_ref, g_ref, mod_ref, w_ref, h_ref, p_ref, lo_scr = refs
    else:
        x_ref, g_ref, mod_ref, w_ref, h_ref, p_ref, lo_scr = refs
    i = pl.program_id(0)
    is_s = i == dm.n_tiles - 1
    bp = jnp.minimum(i // (dm.S // ROW_TILE), dm.Bp - 1)
    g = g_ref[...]

    def chunk(j, carry):
        rows = pl.ds(pl.multiple_of(j * ROW_CHUNK, ROW_CHUNK), ROW_CHUNK)
        if first:
            x = jnp.where(is_s, xs_ref[rows, :], xp_ref[rows, :])
        else:
            x = x_ref[rows, :]
        idx = jnp.where(is_s, dm.Bp + j, bp)
        h = _rms_mod(x, g, mod_ref[k_sc, pl.ds(idx, 1), :], mod_ref[k_sh, pl.ds(idx, 1), :])
        hi, lo = _split_bf16(h)
        h_ref[rows, :] = hi
        lo_scr[rows, :] = lo
        return carry

    lax.fori_loop(0, ROW_TILE // ROW_CHUNK, chunk, 0, unroll=CHUNK_UNROLL)
    w_hi, w_lo = _gate_weight(w_ref)
    p_ref[...] = _dot3(h_ref[...], lo_scr[...], w_hi, w_lo)


def _norm_proj(dm, l, x_in, g, mod, w_in, k_sc, k_sh):
    d, tc, nt = dm.D, dm.Tc, dm.n_tiles
    first = isinstance(x_in, tuple)
    np_tiles = dm.Tp // ROW_TILE
    tile = lambda i: (i, 0)
    if first:
        x_args = list(x_in)
        x_specs = [pl.BlockSpec((ROW_TILE, d), lambda i: (jnp.minimum(i, np_tiles - 1), 0)),
                   pl.BlockSpec((ROW_TILE, d), lambda i: (0, 0))]
    else:
        x_args = [x_in]
        x_specs = [pl.BlockSpec((ROW_TILE, d), tile)]
    out_shape = [jax.ShapeDtypeStruct((tc, d), BF16), jax.ShapeDtypeStruct((tc, LANES), F32)]
    out_specs = [pl.BlockSpec((ROW_TILE, d), tile), pl.BlockSpec((ROW_TILE, LANES), tile)]
    return pl.pallas_call(
        functools.partial(_norm_proj_kernel, first=first, dm=dm, k_sc=k_sc, k_sh=k_sh),
        grid=(nt,),
        in_specs=x_specs + [pl.BlockSpec((None, 1, d), lambda i: (l, 0, 0)),
                            pl.BlockSpec((None, 6, dm.n_mod, d), lambda i: (l, 0, 0, 0)),
                            pl.BlockSpec((None, 2 * dm.H, d), lambda i: (l, dm.off_z // (2 * dm.H), 0))],
        out_specs=out_specs,
        out_shape=out_shape,
        scratch_shapes=[pltpu.VMEM((ROW_TILE, d), BF16)],
        compiler_params=_cparams(("arbitrary",), ROW_TILE * d * 40 / MIB + 16),
        name="norm_proj",
    )(*x_args, g, mod, w_in)


def _in_proj_kernel(a_ref, w_ref, wn_ref, o_ref, w_scr, *, n_aligned, shift):
    j, i = pl.program_id(0), pl.program_id(1)

    groups = w_ref.shape[0] // LANES

    @pl.when((i == 0) & (j < n_aligned))
    def _cast():
        for g in range(groups):
            w_scr[:, g * LANES:(g + 1) * LANES] = w_ref[g * LANES:(g + 1) * LANES, :].T.astype(BF16)

    @pl.when((i == 0) & (j >= n_aligned))
    def _cast_shifted():
        for g in range(groups):
            if g + 1 < groups:
                rows = w_ref[g * LANES + shift:(g + 1) * LANES + shift, :]
            else:
                rows = jnp.concatenate([w_ref[g * LANES + shift:, :], wn_ref[...]], axis=0)
            w_scr[:, g * LANES:(g + 1) * LANES] = rows.T.astype(BF16)

    o_ref[...] = _dot(a_ref[...], w_scr[...]).astype(o_ref.dtype)


def _in_proj(dm, l, a, w_in_t):
    m, k = a.shape
    off_if, n = dm.off_z, dm.n_main
    shift = 2 * dm.H
    tn = max(t for t in range(LANES, 1024 + 1, LANES) if off_if % t == 0 and n % t == 0)
    tm = _pick_tile(m, 1536)
    vmem = (2 * tm * k * 2 + 2 * k * (tn + shift) * 4 + k * tn * 2 + 2 * tm * tn * 2 + 2 * tm * tn * 4) / MIB + 8
    return pl.pallas_call(
        functools.partial(_in_proj_kernel, n_aligned=off_if // tn, shift=shift),
        grid=(n // tn, m // tm),
        in_specs=[pl.BlockSpec((tm, k), lambda j, i: (i, 0)),
                  pl.BlockSpec((None, tn, k), lambda j, i: (l, j, 0)),
                  pl.BlockSpec((None, shift, k), lambda j, i: (l, (j + 1) * (tn // shift), 0))],
        out_specs=pl.BlockSpec((tm, tn), lambda j, i: (i, j)),
        out_shape=jax.ShapeDtypeStruct((m, n), BF16),
        scratch_shapes=[pltpu.VMEM((k, tn), BF16)],
        compiler_params=_cparams(("arbitrary", "arbitrary"), vmem),
        name="in_proj",
    )(a, w_in_t, w_in_t)


def _pad_rows(x, rows):
    if x.shape[0] == rows:
        return x
    return jnp.concatenate([x, jnp.zeros((rows - x.shape[0], x.shape[1]), x.dtype)], axis=0)


def _mlstm_kernel(*refs, dm, lv, n_seq):
    a_ref = refs[10]
    seq_ok = pl.program_id(0) < n_seq

    @pl.when(seq_ok)
    def _run():
        _mlstm_step(*refs, dm=dm, lv=lv)

    @pl.when(jnp.logical_not(seq_ok))
    def _fill():
        a_ref[...] = jnp.zeros_like(a_ref)


def _mlstm_step(q_ref, k_ref, v_ref, o_ref, gt_ref, bias_ref, mhg_ref, c0_ref, n0_ref, m0_ref,
                a_ref, cout_ref, nout_ref, mout_ref, st_scr, m_scr, *, dm, lv):
    h_n, dk, dv = dm.H, dm.DK, dm.DV
    L = max(lv, LANES)
    c = pl.program_id(1)

    scale = dk ** -0.5

    @pl.when(c == 0)
    def _init():
        for h in range(h_n):
            st_scr[h, :, :dv] = c0_ref[h].T * (1.0 / scale)
            st_scr[h, :, dv:] = jnp.broadcast_to(n0_ref[h:h + 1, :] * (1.0 / scale), (LANES, dk)).T
        m_scr[...] = m0_ref[...]

    row = lax.broadcasted_iota(I32, (L, L), 0)
    col = lax.broadcasted_iota(I32, (L, L), 1)
    causal = col <= row
    lane = lax.broadcasted_iota(I32, (L, LANES), 1)
    rowl = lax.broadcasted_iota(I32, (L, LANES), 0)

    xg = _pad_rows(gt_ref[...], L) + bias_ref[...]
    f_log = jnp.minimum(xg, 0.0) - jnp.log1p(jnp.exp(-jnp.abs(xg)))
    gl = jnp.where(lane < h_n, xg, f_log)
    gl = jnp.where(rowl < lv, gl, 0.0)
    tri = causal.astype(BF16)
    g_hi = gl.astype(BF16)
    r1 = gl - g_hi.astype(F32)
    g_mid = r1.astype(BF16)
    g_lo = (r1 - g_mid.astype(F32)).astype(BF16)
    cum = _dot(tri, g_hi) + _dot(tri, g_mid) + _dot(tri, g_lo)
    b_all = pltpu.roll(cum, LANES - h_n, axis=1)
    a_all = jnp.where(rowl < lv, gl - b_all, NEG)

    kt_all = _pad_rows(k_ref[...], L).T
    q_all = _pad_rows(q_ref[...], L)
    v_all = _pad_rows(v_ref[...], L)
    a_rows = a_all.T
    m_old = m_scr[...]
    m_new = m_old
    lane1 = lax.broadcasted_iota(I32, (1, LANES), 1)
    ones_blk = jnp.ones((L, LANES), BF16)
    v_blocks = dv // LANES

    for h in range(h_n):
        a_row = a_rows[h:h + 1, :]
        a_rep = jnp.broadcast_to(a_all[:, h:h + 1], (L, LANES))
        b_rep = jnp.broadcast_to(b_all[:, h:h + 1], (L, LANES))
        m_prev = m_old[:, h:h + 1]
        big_m = jnp.maximum(jnp.max(jnp.where(causal, a_row, NEG), axis=1, keepdims=True), m_prev)
        d_w = jnp.where(causal, jnp.exp(jnp.minimum(a_row - big_m, 0.0)), 0.0)
        inter = jnp.exp(m_prev - big_m)
        qh = q_all[:, h * dk:(h + 1) * dk]
        v_parts = [v_all[:, h * dv + j * LANES:h * dv + (j + 1) * LANES] for j in range(v_blocks)]
        kt = kt_all[h * dk:(h + 1) * dk, :]
        sd = (_dot(qh, kt) * d_w).astype(BF16)
        st = st_scr[h]
        q_int = (inter * qh.astype(F32)).astype(BF16)
        x = _dot(jnp.concatenate([sd, q_int], axis=1),
                 jnp.concatenate([jnp.concatenate(v_parts + [ones_blk], axis=1), st.astype(BF16)], axis=0))
        den = x[:, dv:]
        inv = 1.0 / jnp.maximum(jnp.abs(den), jnp.exp(-(b_rep + big_m)) * (1.0 / scale))
        sq = jnp.sum(x[:, :dv] * x[:, :dv], axis=1, keepdims=True) * (1.0 / dv)
        sc = (inv * lax.rsqrt(inv * inv * sq + EPS))[:lv]
        for j in range(v_blocks):
            cols = slice(h * dv + j * LANES, h * dv + (j + 1) * LANES)
            gate_o = 0.5 * jnp.tanh(0.5 * o_ref[:, cols].astype(F32)) + 0.5
            a_ref[:, cols] = (x[:lv, j * LANES:(j + 1) * LANES] * sc * (mhg_ref[:, cols] * gate_o)).astype(BF16)
        m_last = big_m[L - 1:L, :]
        decay = jnp.exp(m_prev - m_last)
        w_rep = jnp.exp(a_rep - m_last)
        vw = jnp.concatenate([(p.astype(F32) * w_rep).astype(BF16) for p in v_parts] + [w_rep.astype(BF16)], axis=1)
        st_scr[h] = decay * st + _dot(kt, vw)
        m_new = jnp.where(lane1 == h, b_all[L - 1:L, h:h + 1] + m_last, m_new)
    m_scr[...] = m_new

    @pl.when(c == pl.num_programs(1) - 1)
    def _fin():
        for h in range(h_n):
            cout_ref[h] = st_scr[h, :, :dv].T * scale
            nout_ref[h:h + 1, :] = st_scr[h, :, dv:].T[0:1, :] * scale
        mout_ref[...] = m_scr[...]


def _mlstm(dm, proj, gates, bias, mhg, c0, n0, m0, ls, *, sample):
    if sample:
        b_n, n_grid, nc, lv, row0, n_rows = dm.Bs, ROW_TILE // dm.DS, 1, dm.DS, dm.Tp // dm.DS, ROW_TILE
    else:
        b_n, n_grid, nc, lv, row0, n_rows = dm.Bp, dm.Bp, dm.S // MLSTM_CHUNK, MLSTM_CHUNK, 0, dm.Tp
    h_n, dk, dv, qk, v = dm.H, dm.DK, dm.DV, dm.QK, dm.V
    rb = lambda b, c: row0 + b * nc + c
    sq = lambda b: jnp.minimum(b, b_n - 1)
    in_specs = [
        pl.BlockSpec((lv, qk), lambda b, c: (rb(b, c), 0)),
        pl.BlockSpec((lv, qk), lambda b, c: (rb(b, c), 1)),
        pl.BlockSpec((lv, v), lambda b, c: (rb(b, c), dm.off_v // v)),
        pl.BlockSpec((lv, v), lambda b, c: (rb(b, c), dm.off_o // v)),
        pl.BlockSpec((lv, LANES), lambda b, c: (rb(b, c), 0)),
        pl.BlockSpec((1, LANES), lambda b, c: (0, 0)),
        pl.BlockSpec((1, v), lambda b, c: (0, 0)),
        pl.BlockSpec((None, None, h_n, dv, dk), lambda b, c: (ls, sq(b), 0, 0, 0)),
        pl.BlockSpec((None, None, h_n, dk), lambda b, c: (ls, sq(b), 0, 0)),
        pl.BlockSpec((None, None, 1, LANES), lambda b, c: (ls, sq(b), 0, 0)),
    ]
    return pl.pallas_call(
        functools.partial(_mlstm_kernel, dm=dm, lv=lv, n_seq=b_n),
        grid=(n_grid, nc),
        in_specs=in_specs,
        out_specs=[pl.BlockSpec((lv, v), lambda b, c: (b * nc + c, 0)),
                   pl.BlockSpec((None, h_n, dv, dk), lambda b, c: (sq(b), 0, 0, 0)),
                   pl.BlockSpec((None, h_n, dk), lambda b, c: (sq(b), 0, 0)),
                   pl.BlockSpec((None, 1, LANES), lambda b, c: (sq(b), 0, 0))],
        out_shape=[jax.ShapeDtypeStruct((n_rows, v), BF16),
                   jax.ShapeDtypeStruct((b_n, h_n, dv, dk), F32),
                   jax.ShapeDtypeStruct((b_n, h_n, dk), F32),
                   jax.ShapeDtypeStruct((b_n, 1, LANES), F32)],
        scratch_shapes=[pltpu.VMEM((h_n, dk, dv + LANES), F32), pltpu.VMEM((1, LANES), F32)],
        compiler_params=_cparams(("arbitrary", "arbitrary"), 40),
        name="mlstm_sample" if sample else "mlstm_prompt",
    )(proj, proj, proj, proj, gates, bias, mhg, c0, n0, m0)


def _sgu_kernel(*refs, dm, lv, n_seq, emit_v):
    g_ref = refs[5]
    seq_ok = pl.program_id(0) < n_seq

    @pl.when(seq_ok)
    def _run():
        _sgu_step(*refs, dm=dm, lv=lv, emit_v=emit_v)

    @pl.when(jnp.logical_not(seq_ok))
    def _fill():
        g_ref[...] = jnp.zeros_like(g_ref)


def _sgu_step(*refs, dm, lv, emit_v):
    z_ref, lng_ref, lnb_ref, ws_ref, bst_ref, g_ref = refs[:6]
    gh, gch = dm.GH, dm.GH // dm.G
    z = jax.nn.gelu(z_ref[...].astype(F32))
    u, v = z[:, :gh], z[:, gh:]
    xc = v - jnp.mean(v, axis=-1, keepdims=True)
    vn = xc * lax.rsqrt(jnp.mean(xc * xc, axis=-1, keepdims=True) + EPS) * lng_ref[...] + lnb_ref[...]
    if emit_v:
        refs[6][...] = vn
    row = lax.broadcasted_iota(I32, (lv, lv), 0)
    col = lax.broadcasted_iota(I32, (lv, lv), 1)
    vb = vn.astype(BF16)
    for g in range(dm.G):
        w = jnp.where(col <= row, ws_ref[g, :lv, :lv], 0.0).astype(BF16)
        cols = slice(g * gch, (g + 1) * gch)
        for c in range(z.shape[0] // lv):
            rows = slice(c * lv, (c + 1) * lv)
            mixed = _dot(w, vb[rows, cols]) + bst_ref[:lv, g:g + 1]
            g_ref[rows, cols] = (u[rows, cols] * mixed).astype(BF16)


def _sgu(dm, proj, lng, lnb, ws, bst, *, sample):
    if sample:
        b_n, n_grid, nc, lv, br, row0, n_rows = dm.Bs, ROW_TILE // dm.DS, 1, dm.DS, dm.DS, dm.Tp // dm.DS, ROW_TILE
    else:
        br = SGU_CHUNKS_PER_STEP * dm.GC
        b_n, n_grid, nc, lv, row0, n_rows = dm.Bp, dm.Bp, dm.S // br, dm.GC, 0, dm.Tp
    gh = dm.GH
    rb = lambda b, c: row0 + b * nc + c
    in_specs = [
        pl.BlockSpec((br, 2 * gh), lambda b, c: (rb(b, c), dm.off_z // (2 * gh))),
        pl.BlockSpec((1, gh), lambda b, c: (0, 0)),
        pl.BlockSpec((1, gh), lambda b, c: (0, 0)),
        pl.BlockSpec((dm.G, dm.GC, dm.GC), lambda b, c: (0, 0, 0)),
        pl.BlockSpec((dm.GC, LANES), lambda b, c: (0, 0)),
    ]
    out_specs = [pl.BlockSpec((br, gh), lambda b, c: (b * nc + c, 0))]
    out_shape = [jax.ShapeDtypeStruct((n_rows, gh), BF16)]
    if sample:
        out_specs.append(pl.BlockSpec((None, lv, gh), lambda b, c: (jnp.minimum(b, b_n - 1), 0, 0)))
        out_shape.append(jax.ShapeDtypeStruct((b_n, lv, gh), F32))
    return pl.pallas_call(
        functools.partial(_sgu_kernel, dm=dm, lv=lv, n_seq=b_n, emit_v=sample),
        grid=(n_grid, nc),
        in_specs=in_specs,
        out_specs=out_specs,
        out_shape=out_shape,
        compiler_params=_cparams(("arbitrary", "arbitrary"), 24),
        name="sgu_sample" if sample else "sgu_prompt",
    )(proj, lng, lnb, ws, bst)


def _merge_kernel(ap_ref, as_ref, gp_ref, gs_ref, ga_ref, gb_ref, wa_ref, wb_ref, o_ref):
    is_s = pl.program_id(0) == pl.num_programs(0) - 1
    a = jnp.where(is_s, as_ref[...], ap_ref[...])
    g = jnp.where(is_s, gs_ref[...], gp_ref[...])
    d = o_ref.shape[1]
    cw = min(d, 1024)
    for c in range(d // cw):
        cols = slice(c * cw, (c + 1) * cw)
        pa = _dot(a, wa_ref[:, cols])
        pb = _dot(g, wb_ref[:, cols])
        sa = 0.5 * jnp.tanh(0.5 * ga_ref[:, cols].astype(F32)) + 0.5
        sb = 0.5 * jnp.tanh(0.5 * gb_ref[:, cols].astype(F32)) + 0.5
        o_ref[:, cols] = (sa * pa + sb * pb).astype(o_ref.dtype)


def _merge(dm, a_p, a_s, g_p, g_s, proj, wa, wb):
    tc, d, v, gh = dm.Tc, dm.D, dm.V, dm.GH
    tm = ROW_TILE
    assert dm.off_ga % d == 0 and dm.off_gb % d == 0
    np_tiles = dm.Tp // tm
    prompt = lambda i: (jnp.minimum(i, np_tiles - 1), 0)
    const = lambda i: (0, 0)
    once = pl.Buffered(1)
    vmem = (2 * tm * (v + gh) * 2 + tm * (v + gh) * 2 + 6 * tm * d * 2 + (v + gh) * d * 2
            + 6 * tm * min(d, 1024) * 4 + tm * (v + gh) * 2) / MIB + 8
    return pl.pallas_call(
        _merge_kernel,
        grid=(tc // tm,),
        in_specs=[pl.BlockSpec((tm, v), prompt),
                  pl.BlockSpec((tm, v), const, pipeline_mode=once),
                  pl.BlockSpec((tm, gh), prompt),
                  pl.BlockSpec((tm, gh), const, pipeline_mode=once),
                  pl.BlockSpec((tm, d), lambda i: (i, dm.off_ga // d)),
                  pl.BlockSpec((tm, d), lambda i: (i, dm.off_gb // d)),
                  pl.BlockSpec((v, d), const, pipeline_mode=once),
                  pl.BlockSpec((gh, d), const, pipeline_mode=once)],
        out_specs=pl.BlockSpec((tm, d), lambda i: (i, 0)),
        out_shape=jax.ShapeDtypeStruct((tc, d), BF16),
        compiler_params=_cparams(("arbitrary",), vmem),
        name="merge",
    )(a_p, a_s, g_p, g_s, proj, proj, wa, wb)


def _route(logits, eg, ne):
    epg = ne // eg
    lane = lax.broadcasted_iota(I32, logits.shape, 1)
    gmask = lane < eg
    gmax = jnp.max(jnp.where(gmask, logits, NEG), axis=1, keepdims=True)
    gexp = jnp.where(gmask, jnp.exp(jnp.minimum(logits - gmax, 0.0)), 0.0)
    pg = gexp / jnp.sum(gexp, axis=1, keepdims=True)
    p_grp = jnp.max(pg, axis=1, keepdims=True)
    g_sel = jnp.min(jnp.where(gmask & (pg == p_grp), lane, LANES), axis=1, keepdims=True)
    lo = eg + g_sel * epg
    emask = (lane >= lo) & (lane < lo + epg)
    emax = jnp.max(jnp.where(emask, logits, NEG), axis=1, keepdims=True)
    eexp = jnp.where(emask, jnp.exp(jnp.minimum(logits - emax, 0.0)), 0.0)
    pe = eexp / jnp.sum(eexp, axis=1, keepdims=True)
    p1 = jnp.max(jnp.where(emask, pe, -1.0), axis=1, keepdims=True)
    i1 = jnp.min(jnp.where(emask & (pe == p1), lane, LANES), axis=1, keepdims=True)
    rest = emask & (lane != i1)
    p2 = jnp.max(jnp.where(rest, pe, -1.0), axis=1, keepdims=True)
    i2 = jnp.min(jnp.where(rest & (pe == p2), lane, LANES), axis=1, keepdims=True)
    psum = p1 + p2
    eid = jnp.where(lane == 0, i1 - eg, jnp.where(lane == 1, i2 - eg, 0))
    gate = jnp.where(lane == 0, p_grp * (p1 / psum), jnp.where(lane == 1, p_grp * (p2 / psum), 0.0))
    return eid, gate


def _outproj_kernel(*refs, dm, first):
    if first:
        m_ref, xp_ref, xs_ref = refs[:3]
        refs = refs[3:]
    else:
        m_ref, x_ref = refs[:2]
        refs = refs[2:]
    w_ref, g_ref, mod_ref, wr_ref, rb_ref, x1_ref, hp_ref, eid_ref, gate_ref, acc_scr, hi_scr, lo_scr = refs
    i = pl.program_id(0)
    t = jnp.maximum(i - 1, 0)
    is_s = t == dm.n_tiles - 1
    bp = jnp.minimum(t // (dm.S // ROW_TILE), dm.Bp - 1)
    cur, prev = i % 2, (i + 1) % 2
    g = g_ref[...]

    @pl.when(i == 0)
    def _no_previous_tile():
        acc_scr[1] = jnp.zeros(acc_scr.shape[1:], F32)

    def chunk(j):
        rows = pl.ds(j * ROW_CHUNK, ROW_CHUNK)
        idx = jnp.where(is_s, dm.Bp + j, bp)
        x = jnp.where(is_s, xs_ref[rows, :], xp_ref[rows, :]) if first else x_ref[rows, :]
        x1 = x + mod_ref[GT1, pl.ds(idx, 1), :] * acc_scr[prev, rows, :]
        x1_ref[rows, :] = x1
        h = _rms_mod(x1, g, mod_ref[SC2, pl.ds(idx, 1), :], mod_ref[SH2, pl.ds(idx, 1), :])
        hi, lo = _split_bf16(h)
        hi_scr[rows, :] = hi
        lo_scr[rows, :] = lo
        hp_ref[rows, :] = _pack_pair(h)

    n_chunks = ROW_TILE // ROW_CHUNK
    n_slices = min(dm.D // 256, n_chunks)
    for s in range(n_slices):
        cols = slice(s * dm.D // n_slices, (s + 1) * dm.D // n_slices)
        acc_scr[cur, :, cols] = _dot(m_ref[...], w_ref[:, cols])
        for j in range(s * n_chunks // n_slices, (s + 1) * n_chunks // n_slices):
            chunk(j)
    w_hi, w_lo = _split_bf16(wr_ref[...])
    logits = _dot3(hi_scr[...], lo_scr[...], w_hi, w_lo) + rb_ref[...]
    eid, gate = _route(logits, dm.EG, dm.NE)
    eid_ref[...] = eid
    gate_ref[...] = gate


def _outproj(dm, l, merged, x, w_out, g2, mod, w_router, b_router):
    d, tc = dm.D, dm.Tc
    tile = lambda i: (jnp.maximum(i - 1, 0), 0)
    const = lambda i: (0, 0)
    first = isinstance(x, tuple)
    if first:
        np_tiles = dm.Tp // ROW_TILE
        x_args = list(x)
        x_specs = [pl.BlockSpec((ROW_TILE, d), lambda i: (jnp.clip(i - 1, 0, np_tiles - 1), 0)),
                   pl.BlockSpec((ROW_TILE, d), const)]
    else:
        x_args = [x]
        x_specs = [pl.BlockSpec((ROW_TILE, d), tile)]
    return pl.pallas_call(
        functools.partial(_outproj_kernel, dm=dm, first=first),
        grid=(dm.n_tiles + 1,),
        in_specs=[pl.BlockSpec((ROW_TILE, d), lambda i: (jnp.minimum(i, dm.n_tiles - 1), 0))] + x_specs + [
                  pl.BlockSpec((d, d), const),
                  pl.BlockSpec((None, 1, d), lambda i: (l, 0, 0)),
                  pl.BlockSpec((None, 6, dm.n_mod, d), lambda i: (l, 0, 0, 0)),
                  pl.BlockSpec((d, LANES), const),
                  pl.BlockSpec((1, LANES), const)],
        out_specs=[pl.BlockSpec((ROW_TILE, d), tile),
                   pl.BlockSpec((ROW_TILE, d // 2), tile),
                   pl.BlockSpec((ROW_TILE, LANES), tile),
                   pl.BlockSpec((ROW_TILE, LANES), tile)],
        out_shape=[jax.ShapeDtypeStruct((tc, d), F32),
                   jax.ShapeDtypeStruct((tc, d // 2), U32),
                   jax.ShapeDtypeStruct((tc, LANES), I32),
                   jax.ShapeDtypeStruct((tc, LANES), F32)],
        scratch_shapes=[pltpu.VMEM((2, ROW_TILE, d), F32), pltpu.VMEM((ROW_TILE, d), BF16),
                        pltpu.VMEM((ROW_TILE, d), BF16)],
        compiler_params=_cparams(("arbitrary",), (4 * d * d + ROW_TILE * d * 48) / MIB + 12),
        name="out_proj",
    )(merged, *x_args, w_out, g2, mod, w_router, b_router)


def _rank_kernel(eid_ref, rank_ref, cnt_ref, run_scr, start_scr):
    sweep, i = pl.program_id(0), pl.program_id(1)

    @pl.when(i == 0)
    def _init():
        @pl.when(sweep == 1)
        def _starts():
            padded = jnp.ceil(run_scr[...] * (1.0 / MOE_BLOCK)) * MOE_BLOCK
            earlier = (lax.broadcasted_iota(I32, (LANES, LANES), 0)
                       < lax.broadcasted_iota(I32, (LANES, LANES), 1)).astype(BF16)
            p_hi, p_lo = _split_bf16(jnp.broadcast_to(padded, (8, LANES)))
            start_scr[...] = (_dot(p_hi, earlier) + _dot(p_lo, earlier))[0:1, :]
            cnt_ref[...] = run_scr[...].astype(I32)

        @pl.when(sweep == 0)
        def _no_starts_yet():
            start_scr[...] = jnp.zeros_like(start_scr)

        run_scr[...] = jnp.zeros_like(run_scr)

    eid = eid_ref[...]
    lane = lax.broadcasted_iota(I32, eid.shape, 1)
    e0 = lane == eid[:, 0:1]
    e1 = lane == eid[:, 1:2]
    hot = (e0 | e1).astype(BF16)
    n = eid.shape[0]
    strict = (lax.broadcasted_iota(I32, (n, n), 1) < lax.broadcasted_iota(I32, (n, n), 0)).astype(BF16)
    before = _dot(strict, hot) + run_scr[...]
    slot = before + start_scr[...]
    s0 = jnp.sum(jnp.where(e0, slot, 0.0), axis=1, keepdims=True)
    s1 = jnp.sum(jnp.where(e1, slot, 0.0), axis=1, keepdims=True)
    rank_ref[...] = jnp.where(lane == 0, s0, jnp.where(lane == 1, s1, 0.0)).astype(I32)
    run_scr[...] = run_scr[...] + jnp.sum(hot.astype(F32), axis=0, keepdims=True)


def _rank(dm, eid):
    return pl.pallas_call(
        _rank_kernel,
        grid=(2, dm.n_tiles),
        in_specs=[pl.BlockSpec((ROW_TILE, LANES), lambda s, i: (i, 0))],
        out_specs=[pl.BlockSpec((ROW_TILE, LANES), lambda s, i: (s * i, 0)),
                   pl.BlockSpec((1, LANES), lambda s, i: (0, 0))],
        out_shape=[jax.ShapeDtypeStruct((dm.Tc, LANES), I32), jax.ShapeDtypeStruct((1, LANES), I32)],
        scratch_shapes=[pltpu.VMEM((1, LANES), F32), pltpu.VMEM((1, LANES), F32)],
        compiler_params=_cparams(("arbitrary", "arbitrary"), 16),
        name="moe_rank",
    )(eid)


def _dispatch_kernel(clear_ref, dest_ref, h_ref, xs_ref, zero_scr, sem):
    @pl.when(pl.program_id(0) == 0)
    def _clear_blocks():
        zero_scr[...] = jnp.zeros_like(zero_scr)

        def clear_copy(start):
            return pltpu.make_async_copy(zero_scr, xs_ref.at[pl.ds(pl.multiple_of(start, MOE_BLOCK), MOE_BLOCK)], sem)

        for n in range(clear_ref.shape[0]):
            @pl.when(clear_ref[n] >= 0)
            def _start():
                clear_copy(clear_ref[n]).start()
        for n in range(clear_ref.shape[0]):
            @pl.when(clear_ref[n] >= 0)
            def _wait():
                clear_copy(clear_ref[n]).wait()

    for r in range(ROW_TILE):
        for k in range(TOP_K):
            pltpu.make_async_copy(h_ref.at[pl.ds(r, 1)], xs_ref.at[pl.ds(dest_ref[TOP_K * r + k], 1)],
                                  sem).start(priority=k)
    for k in range(TOP_K):
        pltpu.make_async_copy(h_ref, xs_ref.at[pl.ds(0, ROW_TILE)], sem).wait()


def _dispatch(dm, clears, dest, hp):
    d2 = dm.D // 2
    return pl.pallas_call(
        _dispatch_kernel,
        grid_spec=pltpu.PrefetchScalarGridSpec(
            num_scalar_prefetch=1,
            grid=(dm.n_tiles,),
            in_specs=[pl.BlockSpec((ROW_TILE * TOP_K,), lambda i, t: (i,), memory_space=pltpu.SMEM),
                      pl.BlockSpec((ROW_TILE, d2), lambda i, t: (i, 0))],
            out_specs=pl.BlockSpec(memory_space=pl.ANY),
            scratch_shapes=[pltpu.VMEM((MOE_BLOCK, d2), U32), pltpu.SemaphoreType.DMA]),
        out_shape=jax.ShapeDtypeStruct((dm.cap, d2), U32),
        compiler_params=_cparams(("arbitrary",), 16),
        name="moe_dispatch",
    )(clears, dest, hp)


def _expert_kernel(be_ref, nu_ref, nx_ref, sl_ref, xs_ref, wg_hbm, wu_hbm, wd_hbm, ys_ref,
                   wg_f, wu_f, wd_f, wg_s, wu_s, wd_s, sem, *, layer):
    j = pl.program_id(0)
    e, slot = be_ref[j], sl_ref[j]
    used = j < nu_ref[0]
    first = used & ((j == 0) | (e != be_ref[jnp.maximum(j - 1, 0)]))

    def weight_copies(expert, s):
        return [pltpu.make_async_copy(src.at[layer, expert], dst.at[s], sem.at[s])
                for src, dst in ((wg_hbm, wg_f), (wu_hbm, wu_f), (wd_hbm, wd_f))]

    @pl.when(j == 0)
    def _first_expert():
        for cp in weight_copies(e, slot):
            cp.start()

    @pl.when(first)
    def _switch_expert():
        for cp in weight_copies(e, slot):
            cp.wait()
        wg_s[...] = wg_f[slot].astype(BF16)
        wu_s[...] = wu_f[slot].astype(BF16)
        wd_s[...] = wd_f[slot].astype(BF16)

        @pl.when(nx_ref[j] >= 0)
        def _prefetch_next():
            for cp in weight_copies(nx_ref[j], 1 - slot):
                cp.start()

    @pl.when(used)
    def _compute():
        d2 = xs_ref.shape[1]
        lo, hi = _unpack_pair(xs_ref[...])
        lo, hi = lo.astype(BF16), hi.astype(BF16)
        hg = _dot(lo, wg_s[:d2, :]) + _dot(hi, wg_s[d2:, :])
        hu = _dot(lo, wu_s[:d2, :]) + _dot(hi, wu_s[d2:, :])
        act = (hg * jax.nn.sigmoid(hg) * hu).astype(BF16)
        ys_ref[...] = _pack_pair(_dot(act, wd_s[...]))

    @pl.when(jnp.logical_not(used))
    def _idle():
        ys_ref[...] = jnp.zeros_like(ys_ref)


def _experts(dm, l, blk_e, n_used, blk_next, blk_slot, xs, w_gate, w_up, w_down):
    d, de, d2 = dm.D, dm.DE, dm.D // 2
    nb = dm.cap // MOE_BLOCK
    rows = lambda j, be, nu, nx, sl: (jnp.minimum(j, nu[0] - 1), 0)
    hbm = pl.BlockSpec(memory_space=pl.ANY)
    return pl.pallas_call(
        functools.partial(_expert_kernel, layer=l),
        grid_spec=pltpu.PrefetchScalarGridSpec(
            num_scalar_prefetch=4,
            grid=(nb,),
            in_specs=[pl.BlockSpec((MOE_BLOCK, d2), rows), hbm, hbm, hbm],
            out_specs=pl.BlockSpec((MOE_BLOCK, d2), lambda j, be, nu, nx, sl: (j, 0)),
            scratch_shapes=[pltpu.VMEM((2, d, de), F32), pltpu.VMEM((2, d, de), F32), pltpu.VMEM((2, de, d), F32),
                            pltpu.VMEM((d, de), BF16), pltpu.VMEM((d, de), BF16), pltpu.VMEM((de, d), BF16),
                            pltpu.SemaphoreType.DMA((2,))]),
        out_shape=jax.ShapeDtypeStruct((dm.cap, d2), U32),
        compiler_params=_cparams(("arbitrary",), (3 * d * de * (2 * 4 + 2) + MOE_BLOCK * d * 28) / MIB + 8),
        name="moe_experts",
    )(blk_e, n_used, blk_next, blk_slot, xs, w_gate, w_up, w_down)


def _combine_kernel(*refs, dm, last):
    if last:
        dest_ref, destn_ref, ys_ref, x_ref, gate_ref, mod_ref, g_ref, yp_ref, ysm_ref, ybuf, y_scr, sem = refs
    else:
        (dest_ref, destn_ref, ys_ref, x_ref, gate_ref, mod_ref, g_ref, modn_ref, w_ref,
         o_ref, h_ref, p_ref, ybuf, lo_scr, sem) = refs
    i = pl.program_id(0)
    is_s = i == dm.n_tiles - 1
    bp = jnp.minimum(i // (dm.S // ROW_TILE), dm.Bp - 1)
    d2 = dm.D // 2
    g = g_ref[...]
    slot = i % 2

    def issue(idx_ref, to_slot, r0, n):
        for r in range(n):
            for k in range(TOP_K):
                pltpu.make_async_copy(ys_ref.at[pl.ds(idx_ref[TOP_K * (r0 + r) + k], 1)],
                                      ybuf.at[to_slot, k, pl.ds(r0 + r, 1)], sem.at[to_slot]).start(priority=k)

    def wait_slot(s):
        for k in range(TOP_K):
            pltpu.make_async_copy(ys_ref.at[pl.ds(0, ROW_TILE)], ybuf.at[s, k], sem.at[s]).wait()

    @pl.when(i == 0)
    def _first_tile():
        issue(dest_ref, 0, 0, ROW_TILE)

    wait_slot(slot)

    def chunk(j):
        rows = pl.ds(j * ROW_CHUNK, ROW_CHUNK)
        issue(destn_ref, 1 - slot, j * ROW_CHUNK, ROW_CHUNK)
        idx = jnp.where(is_s, dm.Bp + j, bp)
        gt = mod_ref[GT2, pl.ds(idx, 1), :]
        gates = gate_ref[rows, :]
        g0, g1 = gates[:, 0:1], gates[:, 1:2]
        lo0, hi0 = _unpack_pair(ybuf[slot, 0, rows, :])
        lo1, hi1 = _unpack_pair(ybuf[slot, 1, rows, :])
        x_lo = x_ref[rows, :d2] + gt[:, :d2] * (g0 * lo0 + g1 * lo1)
        x_hi = x_ref[rows, d2:] + gt[:, d2:] * (g0 * hi0 + g1 * hi1)
        ms = (jnp.sum(x_lo * x_lo, axis=-1, keepdims=True)
              + jnp.sum(x_hi * x_hi, axis=-1, keepdims=True)) * (1.0 / dm.D)
        rs = lax.rsqrt(ms + EPS)
        if last:
            y_scr[rows, :d2] = x_lo * rs * g[:, :d2]
            y_scr[rows, d2:] = x_hi * rs * g[:, d2:]
        else:
            o_ref[rows, :d2] = x_lo
            o_ref[rows, d2:] = x_hi
            sc = modn_ref[SC1, pl.ds(idx, 1), :]
            sh = modn_ref[SH1, pl.ds(idx, 1), :]
            for half, xh in ((slice(0, d2), x_lo), (slice(d2, dm.D), x_hi)):
                h = xh * rs * g[:, half] * (1.0 + sc[:, half]) + sh[:, half]
                hi, lo = _split_bf16(h)
                h_ref[rows, half] = hi
                lo_scr[rows, half] = lo

    for j in range(ROW_TILE // ROW_CHUNK):
        chunk(j)

    @pl.when(is_s)
    def _drain():
        wait_slot(1 - slot)

    if last:
        @pl.when(is_s)
        def _sample_rows():
            ysm_ref[...] = y_scr[...]

        @pl.when(jnp.logical_not(is_s))
        def _prompt_rows():
            yp_ref[...] = y_scr[...]
    else:
        w_hi, w_lo = _gate_weight(w_ref)
        p_ref[...] = _dot3(h_ref[...], lo_scr[...], w_hi, w_lo)


def _combine(dm, l, dest, ys, x1, gate, mod, g_next, w_in=None):
    d = dm.D
    last = w_in is None
    tile = lambda i: (i, 0)
    np_tiles = dm.Tp // ROW_TILE
    in_specs = [pl.BlockSpec((ROW_TILE * TOP_K,), lambda i: (i,), memory_space=pltpu.SMEM),
                pl.BlockSpec((ROW_TILE * TOP_K,), lambda i: (jnp.minimum(i + 1, dm.n_tiles - 1),),
                             memory_space=pltpu.SMEM),
                pl.BlockSpec(memory_space=pl.ANY),
                pl.BlockSpec((ROW_TILE, d), tile),
                pl.BlockSpec((ROW_TILE, LANES), tile),
                pl.BlockSpec((None, 6, dm.n_mod, d), lambda i: (l, 0, 0, 0))]
    scratch = [pltpu.VMEM((2, TOP_K, ROW_TILE, d // 2), U32)]
    if last:
        args = (dest, dest, ys, x1, gate, mod, g_next)
        in_specs.append(pl.BlockSpec((1, d), lambda i: (0, 0)))
        out_specs = [pl.BlockSpec((ROW_TILE, d), lambda i: (jnp.minimum(i, np_tiles - 1), 0)),
                     pl.BlockSpec((ROW_TILE, d), lambda i: (0, 0))]
        out_shape = [jax.ShapeDtypeStruct((dm.Tp, d), F32), jax.ShapeDtypeStruct((ROW_TILE, d), F32)]
        scratch.append(pltpu.VMEM((ROW_TILE, d), F32))
    else:
        args = (dest, dest, ys, x1, gate, mod, g_next, mod, w_in)
        in_specs += [pl.BlockSpec((None, 1, d), lambda i: (l + 1, 0, 0)),
                     pl.BlockSpec((None, 6, dm.n_mod, d), lambda i: (l + 1, 0, 0, 0)),
                     pl.BlockSpec((None, 2 * dm.H, d), lambda i: (l + 1, dm.off_z // (2 * dm.H), 0))]
        out_specs = [pl.BlockSpec((ROW_TILE, d), tile), pl.BlockSpec((ROW_TILE, d), tile),
                     pl.BlockSpec((ROW_TILE, LANES), tile)]
        out_shape = [jax.ShapeDtypeStruct((dm.Tc, d), F32), jax.ShapeDtypeStruct((dm.Tc, d), BF16),
                     jax.ShapeDtypeStruct((dm.Tc, LANES), F32)]
        scratch.append(pltpu.VMEM((ROW_TILE, d), BF16))
    scratch.append(pltpu.SemaphoreType.DMA((2,)))
    return pl.pallas_call(
        functools.partial(_combine_kernel, dm=dm, last=last),
        grid=(dm.n_tiles,),
        in_specs=in_specs,
        out_specs=out_specs,
        out_shape=out_shape,
        scratch_shapes=scratch,
        compiler_params=_cparams(("arbitrary",), ROW_TILE * d * 40 / MIB + 16),
        name="moe_combine_last" if last else "moe_combine",
    )(*args)


def _pad_lanes(x, n=LANES):
    return jnp.pad(x, [(0, 0)] * (x.ndim - 1) + [(0, n - x.shape[-1])])


def _moe_plan(dm, slots, counts):
    counts = counts[0, :dm.NE]
    padded = (counts + MOE_BLOCK - 1) // MOE_BLOCK * MOE_BLOCK
    pad_end = jnp.cumsum(padded)
    dest = slots[:, :TOP_K].reshape(-1)
    nb = dm.cap // MOE_BLOCK
    n_used = (pad_end[-1] // MOE_BLOCK).astype(I32)
    starts = jnp.minimum(jnp.arange(nb, dtype=I32), n_used - 1) * MOE_BLOCK
    blk_e = jnp.minimum(jnp.sum(pad_end[None, :] <= starts[:, None], axis=1), dm.NE - 1).astype(I32)
    ids = jnp.arange(dm.NE, dtype=I32)
    blocks = jnp.arange(nb, dtype=I32)
    clears = jnp.concatenate([jnp.where(padded > 0, pad_end - MOE_BLOCK, -1),
                              jnp.where(blocks >= n_used, blocks * MOE_BLOCK, -1)]).astype(I32)
    live = jnp.where(counts > 0, ids, dm.NE)
    nxt = lax.cummin(jnp.concatenate([live[1:], jnp.full((1,), dm.NE, I32)]), reverse=True)
    blk_next = jnp.where(nxt[blk_e] < dm.NE, nxt[blk_e], -1).astype(I32)
    blk_slot = ((jnp.cumsum(counts > 0) - 1)[blk_e] % 2).astype(I32)
    return dest, blk_e, n_used.reshape(1), clears, blk_next, blk_slot


def kernel(x_prompt, x_sample, c_prompt, c_sample, state_mlstm_C, state_mlstm_n, state_mlstm_m, w_ada, b_ada, norm1_g, norm2_g, w_in, b_igate, b_fgate, mh_norm_g, sgu_ln_g, sgu_ln_b, w_spatial, b_spatial, w_branch_a, w_branch_b, w_out, w_router_group, b_router_group, w_router_expert, b_router_expert, w_expert_gate, w_expert_up, w_expert_down, final_norm_g):
    bp, s, d = x_prompt.shape
    bs, ds_, _ = x_sample.shape
    depth = w_in.shape[0]
    h_n = b_igate.shape[1]
    dv, dk = state_mlstm_C.shape[-2:]
    qk, v = h_n * dk, h_n * dv
    gh = (w_in.shape[2] - (2 * qk + 2 * v + 2 * h_n + 2 * d)) // 2
    dm = Dims(Bp=bp, S=s, Bs=bs, DS=ds_, D=d, DEPTH=depth, H=h_n, DK=dk, DV=dv, GH=gh,
              G=w_spatial.shape[1], GC=w_spatial.shape[2], EG=w_router_group.shape[-1],
              NE=w_router_expert.shape[-1], DE=w_expert_gate.shape[-1])
    assert dm.DS == ROW_CHUNK and dm.Ts <= ROW_TILE and s % ROW_TILE == 0 and s % MLSTM_CHUNK == 0
    assert dm.EG + dm.NE <= LANES and 2 * h_n <= LANES and (2 * h_n) % 8 == 0 and dm.Bp + dm.Bs <= 32
    w_in = jnp.swapaxes(w_in, 1, 2)
    assert dm.off_v % v == 0 and dm.off_o % v == 0 and dm.off_z % (2 * gh) == 0 and d % 256 == 0

    c_all = jnp.concatenate([c_prompt, c_sample, jnp.zeros((32 - bp - bs, d), F32)], axis=0)
    mod = _adaln(c_all, w_ada, b_ada)[:, :bp + bs].reshape(depth, bp + bs, 6, d)
    mod = jnp.pad(mod, ((0, 0), (0, dm.n_mod - bp - bs), (0, 0), (0, 0))).transpose(0, 2, 1, 3)

    xs_rows = jnp.pad(x_sample.reshape(dm.Ts, d), ((0, ROW_TILE - dm.Ts), (0, 0)))
    x = (x_prompt.reshape(dm.Tp, d), xs_rows)
    zeros_c = jnp.zeros((1, bp, h_n, dv, dk), F32)
    zeros_n = jnp.zeros((1, bp, h_n, dk), F32)
    zeros_m = jnp.zeros((1, bp, 1, LANES), F32)
    m_in = _pad_lanes(state_mlstm_m)[:, :, None, :]
    g1 = norm1_g.reshape(depth, 1, d)
    out_c_p, out_n_p, out_m_p, out_c_s, out_n_s, out_m_s, out_v = [], [], [], [], [], [], []

    for l in range(depth):
        if l == 0:
            hb, gates = _norm_proj(dm, l, x, g1, mod, w_in, SC1, SH1)
        proj = _in_proj(dm, l, hb, w_in)

        bias = _pad_lanes(jnp.concatenate([b_igate[l], b_fgate[l]])[None, :])
        mhg = mh_norm_g[l][None, :]
        a_p, c_p, n_p, m_p = _mlstm(dm, proj, gates, bias, mhg, zeros_c, zeros_n, zeros_m, 0, sample=False)
        a_s, c_s, n_s, m_s = _mlstm(dm, proj, gates, bias, mhg, state_mlstm_C, state_mlstm_n, m_in, l,
                                    sample=True)
        lng, lnb = sgu_ln_g[l][None, :], sgu_ln_b[l][None, :]
        bst = _pad_lanes(b_spatial[l].T)
        (g_p,) = _sgu(dm, proj, lng, lnb, w_spatial[l], bst, sample=False)
        g_s, v_rows = _sgu(dm, proj, lng, lnb, w_spatial[l], bst, sample=True)

        merged = _merge(dm, a_p, a_s, g_p, g_s, proj, w_branch_a[l].astype(BF16), w_branch_b[l].astype(BF16))
        w_router = _pad_lanes(jnp.concatenate([w_router_group[l], w_router_expert[l]], axis=1))
        b_router = _pad_lanes(jnp.concatenate([b_router_group[l], b_router_expert[l]])[None, :])
        x1, hp, eid, gate = _outproj(dm, l, merged, x, w_out[l].astype(BF16), norm2_g.reshape(depth, 1, d),
                                     mod, w_router, b_router)

        slots, counts = _rank(dm, eid)
        dest, blk_e, n_used, clears, blk_next, blk_slot = _moe_plan(dm, slots, counts)
        xs = _dispatch(dm, clears, dest, hp)
        ys = _experts(dm, l, blk_e, n_used, blk_next, blk_slot, xs,
                      w_expert_gate, w_expert_up, w_expert_down)
        if l + 1 < depth:
            x, hb, gates = _combine(dm, l, dest, ys, x1, gate, mod, g1, w_in)
        else:
            y_p, y_s = _combine(dm, l, dest, ys, x1, gate, mod, final_norm_g[None, :])

        out_c_p.append(c_p)
        out_n_p.append(n_p)
        out_m_p.append(m_p[:, 0, :h_n])
        out_c_s.append(c_s)
        out_n_s.append(n_s)
        out_m_s.append(m_s[:, 0, :h_n])
        out_v.append(v_rows)

    return (y_p.reshape(bp, s, d), y_s[:dm.Ts].reshape(bs, ds_, d),
            jnp.stack(out_c_p), jnp.stack(out_n_p), jnp.stack(out_m_p),
            jnp.stack(out_c_s), jnp.stack(out_n_s), jnp.stack(out_m_s), jnp.stack(out_v))
```
